```python
import jax, jax.numpy as jnp
from jax import lax
import numpy as np

D_MODEL = 2048
BATCH = 2
SEQ = 16384
DEPTH = 1

CHUNK = 64
RWKV_HEADS = 16
RWKV_HEAD_DIM = 64
RWKV_WIDTH = RWKV_HEADS * RWKV_HEAD_DIM
DECAY_RANK = 96
ICLR_RANK = 96
GATE_RANK = 256
GN_EPS = 64e-5
FOX_HEADS = 8
FOX_HEAD_DIM = 128
FOX_WIDTH = FOX_HEADS * FOX_HEAD_DIM
Q_BLOCK = 128
N_EXPERTS = 32
TOP_K = 4
D_FF = 2048
SWIGLU_LIMIT = 7.0
SWIGLU_ALPHA = 1.702
EXPERT_BLOCK = 512
NORM_EPS = 1e-6
RWKV_COLS = 3 * RWKV_WIDTH + DECAY_RANK + ICLR_RANK + GATE_RANK
FOX_COLS = 3 * FOX_WIDTH + FOX_HEADS + FOX_WIDTH
GATE_COLS = 2 * D_MODEL
IN_COLS = RWKV_COLS + FOX_COLS + GATE_COLS

kernel_name = "hybrid_rwkv7_fox_moe_adaln_block"


def _split(t, sizes):
    cuts = [int(v) for v in np.cumsum(sizes)[:-1]]
    return jnp.split(t, cuts, axis=-1)


def rms_norm(t, g, eps=NORM_EPS):
    tf = t.astype(jnp.float32)
    y = tf * lax.rsqrt(jnp.mean(tf * tf, axis=-1, keepdims=True) + eps)
    return (y * g.astype(jnp.float32)).astype(t.dtype)


def rwkv7_scan(r, w, k, v, a, b):
    bsz, _, nh, n = r.shape
    xs = tuple(jnp.moveaxis(t.astype(jnp.float32), 1, 0) for t in (r, w, k, v, a, b))

    def step(state, inp):
        r_t, w_t, k_t, v_t, a_t, b_t = inp
        sa = jnp.einsum('bhij,bhj->bhi', state, a_t)
        state = state * w_t[:, :, None, :] + sa[..., None] * b_t[:, :, None, :] + v_t[..., None] * k_t[:, :, None, :]
        return state, jnp.einsum('bhij,bhj->bhi', state, r_t)

    s0 = jnp.zeros((bsz, nh, n, n), jnp.float32)
    _, ys = lax.scan(step, s0, xs)
    return jnp.moveaxis(ys, 0, 1)


def rwkv7_mixer(p, mu, w_decay_up, decay_w0, w_iclr_up, iclr_a0, w_gate_up_rwkv,
                k_k, k_a, r_k, gn_g, gn_b):
    bsz, seq, _ = p.shape
    f32 = jnp.float32
    prev = jnp.pad(p, ((0, 0), (1, 0), (0, 0)))[:, :-1]
    p = p + mu * (prev - p)
    r, k, v, d_lo, a_lo, g_lo = _split(p, [RWKV_WIDTH] * 3 + [DECAY_RANK, ICLR_RANK, GATE_RANK])
    w_pre = (decay_w0 + jnp.tanh(d_lo) @ w_decay_up).astype(f32)
    decay = jnp.exp(-jnp.exp(-jax.nn.softplus(-w_pre) - 0.5))
    a = jax.nn.sigmoid(iclr_a0 + a_lo @ w_iclr_up)
    g = jax.nn.sigmoid(g_lo) @ w_gate_up_rwkv
    heads = lambda t: t.reshape(bsz, seq, RWKV_HEADS, RWKV_HEAD_DIM)
    kk = heads(k * k_k).astype(f32)
    kk = kk / jnp.maximum(jnp.linalg.norm(kk, axis=-1, keepdims=True), 1e-12)
    k = k * (1.0 + (a - 1.0) * k_a)
    r_h, k_h, v_h = heads(r), heads(k), heads(v)
    a_h = heads(a).astype(f32)
    y = rwkv7_scan(r_h, heads(decay), k_h, v_h, -kk, kk * a_h)
    mean = jnp.mean(y, axis=-1, keepdims=True)
    var = jnp.mean(jnp.square(y - mean), axis=-1, keepdims=True)
    y = ((y - mean) * lax.rsqrt(var + GN_EPS)).reshape(bsz, seq, RWKV_WIDTH)
    y = y * gn_g.astype(f32) + gn_b.astype(f32)
    bonus = jnp.sum(r_h.astype(f32) * k_h.astype(f32) * r_k.astype(f32), axis=-1, keepdims=True) * v_h.astype(f32)
    out = y + bonus.reshape(bsz, seq, RWKV_WIDTH)
    return out.astype(p.dtype) * g


def fox_mixer(p, b_f, q_norm, k_norm):
    bsz, seq, _ = p.shape
    q, k, v, f_pre, og = _split(p, [FOX_WIDTH] * 3 + [FOX_HEADS, FOX_WIDTH])
    heads = lambda t: t.reshape(bsz, seq, FOX_HEADS, FOX_HEAD_DIM).transpose(0, 2, 1, 3)
    q = rms_norm(heads(q), q_norm)
    k = rms_norm(heads(k), k_norm)
    v = heads(v)
    log_f = jax.nn.log_sigmoid((f_pre + b_f).astype(jnp.float32)).transpose(0, 2, 1)
    cum = jnp.cumsum(log_f, axis=-1)
    scale = FOX_HEAD_DIM ** -0.5
    kpos = jnp.arange(seq)

    def attend(i):
        start = i * Q_BLOCK
        qb = lax.dynamic_slice_in_dim(q, start, Q_BLOCK, axis=2)
        cb = lax.dynamic_slice_in_dim(cum, start, Q_BLOCK, axis=2)
        logits = jnp.einsum('bhqd,bhkd->bhqk', qb, k).astype(jnp.float32) * scale + cb[..., None] - cum[:, :, None, :]
        qpos = start + jnp.arange(Q_BLOCK)
        logits = jnp.where(kpos[None, :] <= qpos[:, None], logits, -jnp.inf)
        probs = jax.nn.softmax(logits, axis=-1)
        return jnp.einsum('bhqk,bhkd->bhqd', probs.astype(v.dtype), v)

    o = lax.map(attend, jnp.arange(seq // Q_BLOCK))
    o = o.transpose(1, 0, 3, 2, 4).reshape(bsz, seq, FOX_WIDTH)
    return o * jax.nn.sigmoid(og)


def moe_ffn(h, w_router, b_router, w_gu, b_gu, w_dn, b_dn):
    n_tok, d = h.shape
    logits = (h @ w_router + b_router).astype(jnp.float32)
    top_val, top_idx = lax.top_k(logits, TOP_K)
    weights = jax.nn.softmax(top_val, axis=-1)
    n_assign = n_tok * TOP_K
    flat_e = top_idx.reshape(n_assign)
    flat_tok = jnp.repeat(jnp.arange(n_tok, dtype=jnp.int32), TOP_K)
    flat_w = weights.reshape(n_assign)
    order = jnp.argsort(flat_e)
    s_e, s_tok, s_w = flat_e[order], flat_tok[order], flat_w[order]
    counts = jnp.bincount(flat_e, length=N_EXPERTS)
    padded = (counts + EXPERT_BLOCK - 1) // EXPERT_BLOCK * EXPERT_BLOCK
    pad_end = jnp.cumsum(padded)
    pad_start = pad_end - padded
    grp_start = jnp.cumsum(counts) - counts
    slot = pad_start[s_e] + jnp.arange(n_assign) - grp_start[s_e]
    n_blocks = -(-n_assign // EXPERT_BLOCK) + N_EXPERTS
    n_slots = n_blocks * EXPERT_BLOCK
    buf_tok = jnp.full((n_slots,), n_tok, jnp.int32).at[slot].set(s_tok)
    buf_w = jnp.zeros((n_slots,), jnp.float32).at[slot].set(s_w)
    blk_e = jnp.minimum(jnp.searchsorted(pad_end, jnp.arange(n_blocks) * EXPERT_BLOCK, side='right'), N_EXPERTS - 1)
    h_pad = jnp.concatenate([h, jnp.zeros((1, d), h.dtype)], axis=0)

    def expert_block(args):
        tok, e = args
        gu = h_pad[tok] @ w_gu[e] + b_gu[e]
        gate, lin = jnp.split(gu, 2, axis=-1)
        gate = jnp.minimum(gate, SWIGLU_LIMIT)
        lin = jnp.clip(lin, -SWIGLU_LIMIT, SWIGLU_LIMIT)
        act = gate * jax.nn.sigmoid(SWIGLU_ALPHA * gate) * (lin + 1.0)
        return act @ w_dn[e] + b_dn[e]

    y = lax.map(expert_block, (buf_tok.reshape(n_blocks, EXPERT_BLOCK), blk_e)).reshape(n_slots, d)
    y = y * buf_w[:, None].astype(y.dtype)
    return jax.ops.segment_sum(y, buf_tok, num_segments=n_tok + 1)[:n_tok]


def setup_inputs(seed: int = 0) -> dict:
    key = jax.random.key(seed)
    ks = jax.random.split(key, 32)
    f32 = jnp.float32
    D = D_MODEL

    def nrm(k, shape, scale=1.0):
        return jax.random.normal(k, shape, f32) * scale

    def unif(k, shape, lo, hi):
        return jax.random.uniform(k, shape, f32, lo, hi)

    return {
        'x': nrm(ks[0], (BATCH, SEQ, D)),
        'c': nrm(ks[1], (BATCH, D)),
        'w_ada': nrm(ks[2], (D, 6 * D), D ** -0.5),
        'b_ada': nrm(ks[3], (6 * D,), 0.01),
        'norm1_g': 1.0 + nrm(ks[4], (D,), 0.02),
        'w_in': nrm(ks[5], (D, IN_COLS), D ** -0.5),
        'rwkv_mu': unif(ks[6], (RWKV_COLS,), 0.0, 1.0),
        'w_decay_up': nrm(ks[7], (DECAY_RANK, RWKV_WIDTH), 0.5 * DECAY_RANK ** -0.5),
        'decay_w0': unif(ks[8], (RWKV_WIDTH,), -5.0, -1.0),
        'w_iclr_up': nrm(ks[9], (ICLR_RANK, RWKV_WIDTH), 0.5 * ICLR_RANK ** -0.5),
        'iclr_a0': nrm(ks[10], (RWKV_WIDTH,), 0.1),
        'w_gate_up_rwkv': nrm(ks[11], (GATE_RANK, RWKV_WIDTH), GATE_RANK ** -0.5),
        'rwkv_k_k': 0.85 + nrm(ks[12], (RWKV_WIDTH,), 0.02),
        'rwkv_k_a': 1.0 + nrm(ks[13], (RWKV_WIDTH,), 0.02),
        'rwkv_r_k': nrm(ks[14], (RWKV_HEADS, RWKV_HEAD_DIM), 0.1),
        'rwkv_gn_g': 1.0 + nrm(ks[15], (RWKV_WIDTH,), 0.02),
        'rwkv_gn_b': nrm(ks[16], (RWKV_WIDTH,), 0.01),
        'w_out_a': nrm(ks[17], (RWKV_WIDTH, D), RWKV_WIDTH ** -0.5),
        'fox_b_f': unif(ks[18], (FOX_HEADS,), 1.0, 5.0),
        'fox_q_norm': 1.0 + nrm(ks[19], (FOX_HEAD_DIM,), 0.02),
        'fox_k_norm': 1.0 + nrm(ks[20], (FOX_HEAD_DIM,), 0.02),
        'w_out_b': nrm(ks[21], (FOX_WIDTH, D), FOX_WIDTH ** -0.5),
        'w_out': nrm(ks[22], (D, D), D ** -0.5),
        'norm2_g': 1.0 + nrm(ks[23], (D,), 0.02),
        'w_router': nrm(ks[24], (D, N_EXPERTS), D ** -0.5),
        'b_router': nrm(ks[25], (N_EXPERTS,), 0.01),
        'w_expert_gu': nrm(ks[26], (N_EXPERTS, D, 2 * D_FF), D ** -0.5),
        'b_expert_gu': nrm(ks[27], (N_EXPERTS, 2 * D_FF), 0.01),
        'w_expert_down': nrm(ks[28], (N_EXPERTS, D_FF, D), D_FF ** -0.5),
        'b_expert_down': nrm(ks[29], (N_EXPERTS, D), 0.01),
        'norm_final_g': 1.0 + nrm(ks[30], (D,), 0.02),
    }


def reference(x, c, w_ada, b_ada, norm1_g, w_in, rwkv_mu, w_decay_up, decay_w0, w_iclr_up,
              iclr_a0, w_gate_up_rwkv, rwkv_k_k, rwkv_k_a, rwkv_r_k, rwkv_gn_g, rwkv_gn_b,
              w_out_a, fox_b_f, fox_q_norm, fox_k_norm, w_out_b, w_out, norm2_g, w_router,
              b_router, w_expert_gu, b_expert_gu, w_expert_down, b_expert_down, norm_final_g):
    mod = jax.nn.silu(c) @ w_ada + b_ada
    shift1, scale1, gate1, shift2, scale2, gate2 = (m[:, None, :] for m in jnp.split(mod, 6, axis=-1))
    for _ in range(DEPTH):
        h = rms_norm(x, norm1_g) * (1.0 + scale1) + shift1
        p_rwkv, p_fox, p_gate = _split(h @ w_in, [RWKV_COLS, FOX_COLS, GATE_COLS])
        y_a = rwkv7_mixer(p_rwkv, rwkv_mu, w_decay_up, decay_w0, w_iclr_up, iclr_a0,
                          w_gate_up_rwkv, rwkv_k_k, rwkv_k_a, rwkv_r_k, rwkv_gn_g, rwkv_gn_b) @ w_out_a
        y_b = fox_mixer(p_fox, fox_b_f, fox_q_norm, fox_k_norm) @ w_out_b
        gate_a, gate_b = jnp.split(jax.nn.sigmoid(p_gate), 2, axis=-1)
        x = x + gate1 * ((gate_a * y_a + gate_b * y_b) @ w_out)
        h = rms_norm(x, norm2_g) * (1.0 + scale2) + shift2
        y = moe_ffn(h.reshape(-1, D_MODEL), w_router, b_router, w_expert_gu, b_expert_gu,
                    w_expert_down, b_expert_down)
        x = x + gate2 * y.reshape(x.shape)
    return rms_norm(x, norm_final_g)
```

```python
import functools

import jax
import jax.numpy as jnp
from jax import lax
from jax.experimental import pallas as pl
from jax.experimental.pallas import tpu as pltpu

F32 = jnp.float32
BF16 = jnp.bfloat16
HIGHEST = lax.Precision.HIGHEST

TOP_K = 4
NORM_EPS = 1e-6
GN_EPS = 64e-5
SWIGLU_LIMIT = 7.0
SWIGLU_ALPHA = 1.702
LANES = 128
RWKV_CHUNK = 64
VMEM_LIMIT_BYTES = 56 * 1024 * 1024
EXPERT_ROWS = 512
NEG_BIG = -1e30


def _tile(n, pref, mult=8):
    t = min(pref, n)
    t -= t % mult
    while t >= mult:
        if n % t == 0:
            return t
        t -= mult
    return n


def _params(*sem):
    return pltpu.CompilerParams(dimension_semantics=sem, vmem_limit_bytes=VMEM_LIMIT_BYTES)


def _sigmoid(x):
    return 1.0 / (1.0 + jnp.exp(-x))


def _rms_mod(x, g, scale, shift):
    ms = jnp.mean(x * x, axis=-1, keepdims=True)
    return x * lax.rsqrt(ms + NORM_EPS) * g * (1.0 + scale) + shift


def _ada_kernel(c_ref, w_ref, b_ref, o_ref):
    c = c_ref[...]
    s = c * _sigmoid(c)
    o_ref[...] = jnp.dot(s, w_ref[...], preferred_element_type=F32, precision=HIGHEST) + b_ref[...]


def _ada(c, w_ada, b_ada):
    bsz, d = c.shape
    n = w_ada.shape[1]
    rows = 8
    c_pad = jnp.zeros((rows, d), F32).at[:bsz].set(c)
    tn = _tile(n, 1024, LANES)
    out = pl.pallas_call(
        _ada_kernel,
        grid=(n // tn,),
        in_specs=[pl.BlockSpec((rows, d), lambda j: (0, 0)),
                  pl.BlockSpec((d, tn), lambda j: (0, j)),
                  pl.BlockSpec((1, tn), lambda j: (0, j))],
        out_specs=pl.BlockSpec((rows, tn), lambda j: (0, j)),
        out_shape=jax.ShapeDtypeStruct((rows, n), F32),
        compiler_params=_params("parallel"),
        name="adaln",
    )(c_pad, w_ada, b_ada.reshape(1, n))
    return out[:bsz]


def _inproj_kernel(x_ref, g_ref, sc_ref, sh_ref, w_ref, e_ref, o_ref, h_scr, *, epilogue):
    @pl.when(pl.program_id(2) == 0)
    def _():
        h = _rms_mod(x_ref[0], g_ref[...], sc_ref[0], sh_ref[0])
        h_scr[...] = h.astype(BF16)

    acc = jnp.dot(h_scr[...], w_ref[...], preferred_element_type=F32)
    o_ref[0] = epilogue(acc, e_ref[...]).astype(o_ref.dtype)


def _ep_identity(acc, extra):
    return acc


def _ep_sigmoid(acc, extra):
    return _sigmoid(acc)


def _ep_headnorm(acc, extra):
    outs = []
    for h in range(acc.shape[1] // LANES):
        a = acc[:, h * LANES:(h + 1) * LANES]
        ms = jnp.mean(a * a, axis=-1, keepdims=True)
        outs.append(a * lax.rsqrt(ms + NORM_EPS))
    return jnp.concatenate(outs, axis=1) * extra


def _inproj(x, g, scale, shift, w, extra, out_dtype, epilogue, name):
    bsz, seq, d = x.shape
    n = w.shape[1]
    tm = _tile(seq, 1024)
    tn = _tile(n, 512, LANES)
    return pl.pallas_call(
        functools.partial(_inproj_kernel, epilogue=epilogue),
        grid=(bsz, seq // tm, n // tn),
        in_specs=[pl.BlockSpec((1, tm, d), lambda b, i, j: (b, i, 0)),
                  pl.BlockSpec((1, d), lambda b, i, j: (0, 0)),
                  pl.BlockSpec((1, 1, d), lambda b, i, j: (b, 0, 0)),
                  pl.BlockSpec((1, 1, d), lambda b, i, j: (b, 0, 0)),
                  pl.BlockSpec((d, tn), lambda b, i, j: (0, j)),
                  pl.BlockSpec((1, tn), lambda b, i, j: (0, j))],
        out_specs=pl.BlockSpec((1, tm, tn), lambda b, i, j: (b, i, j)),
        out_shape=jax.ShapeDtypeStruct((bsz, seq, n), out_dtype),
        scratch_shapes=[pltpu.VMEM((tm, d), BF16)],
        compiler_params=_params("parallel", "parallel", "arbitrary"),
        name=name,
    )(x, g.reshape(1, d), scale, shift, w, extra)


def _rwkv_prep_kernel(p_ref, mu_ref, wd_ref, w0_ref, wi_ref, a0_ref, wg_ref, kk_ref, ka_ref,
                      r_o, k_o, v_o, lw_o, kk_o, a_o, g_o, carry, *, width, gate_rank):
    @pl.when(pl.program_id(1) == 0)
    def _():
        carry[...] = jnp.zeros_like(carry)

    p = p_ref[0]
    tt = p.shape[0]
    row = lax.broadcasted_iota(jnp.int32, p.shape, 0)
    prev = jnp.where(row == 0, carry[...], pltpu.roll(p, 1, axis=0))
    carry[...] = p[tt - 1:tt, :]
    pm = p + mu_ref[...] * (prev - p)
    w = width
    r = pm[:, 0:w]
    k = pm[:, w:2 * w]
    v = pm[:, 2 * w:3 * w]
    d_lo = pm[:, 3 * w:3 * w + LANES]
    a_lo = pm[:, 3 * w + LANES:3 * w + 2 * LANES]
    g_lo = pm[:, 3 * w + 2 * LANES:3 * w + 2 * LANES + gate_rank]
    w_pre = w0_ref[...] + jnp.dot(jnp.tanh(d_lo), wd_ref[...], preferred_element_type=F32,
                                  precision=HIGHEST)
    lw_o[0] = -jnp.exp(-0.5) * _sigmoid(w_pre)
    a = _sigmoid(a0_ref[...] + jnp.dot(a_lo, wi_ref[...], preferred_element_type=F32,
                                       precision=HIGHEST))
    g_o[0] = jnp.dot(_sigmoid(g_lo).astype(BF16), wg_ref[...], preferred_element_type=F32)
    r_o[0] = r
    v_o[0] = v
    a_o[0] = a
    kk_o[0] = k * kk_ref[...]
    k_o[0] = k * (1.0 + (a - 1.0) * ka_ref[...])


def _rwkv_prep(p, mu, wd, w0, wi, a0, wg, k_k, k_a, width, gate_rank):
    bsz, seq, n = p.shape
    tt = _tile(seq, 256)
    row = lambda arr: arr.reshape(1, -1)
    full = lambda shape: pl.BlockSpec(shape, lambda b, i: (0,) * len(shape))
    out_sds = jax.ShapeDtypeStruct((bsz, seq, width), F32)
    out_spec = pl.BlockSpec((1, tt, width), lambda b, i: (b, i, 0))
    return pl.pallas_call(
        functools.partial(_rwkv_prep_kernel, width=width, gate_rank=gate_rank),
        grid=(bsz, seq // tt),
        in_specs=[pl.BlockSpec((1, tt, n), lambda b, i: (b, i, 0)),
                  full((1, n)), full(wd.shape), full((1, width)), full(wi.shape), full((1, width)),
                  full(wg.shape), full((1, width)), full((1, width))],
        out_specs=[out_spec] * 7,
        out_shape=[out_sds] * 7,
        scratch_shapes=[pltpu.VMEM((1, n), F32)],
        compiler_params=_params("parallel", "arbitrary"),
        name="rwkv_prep",
    )(p, row(mu), wd, row(w0), wi, row(a0), wg, row(k_k), row(k_a))


def _rwkv_scan_kernel(r_ref, k_ref, v_ref, lw_ref, kk_ref, a_ref, g_ref, rk_ref, gg_ref, gb_ref,
                      o_ref, s_scr, *, n_chunks, head_dim):
    L = RWKV_CHUNK
    L2 = 2 * L

    @pl.when(pl.program_id(2) == 0)
    def _():
        s_scr[...] = jnp.zeros_like(s_scr)

    lane = lax.broadcasted_iota(jnp.int32, (1, LANES), 1)
    m0 = (lane < head_dim).astype(F32)
    m1 = 1.0 - m0
    r2 = lax.broadcasted_iota(jnp.int32, (L2, L2), 0)
    c2 = lax.broadcasted_iota(jnp.int32, (L2, L2), 1)
    same = (r2 < L) == (c2 < L)
    strict = jnp.logical_and(same, c2 < r2)
    incl = jnp.logical_and(same, c2 <= r2)
    eye = (r2 == c2).astype(F32)
    rowi = lax.broadcasted_iota(jnp.int32, (L, LANES), 0)
    rk = rk_ref[...]
    gn_g = gg_ref[...]
    gn_b = gb_ref[...]

    def stack(x):
        return jnp.concatenate([x * m0, x * m1], axis=0).astype(BF16)

    def head_sum(x):
        s0 = jnp.sum(x * m0, axis=-1, keepdims=True)
        s1 = jnp.sum(x * m1, axis=-1, keepdims=True)
        return s0 * m0 + s1 * m1

    def nt(a, b):
        return lax.dot_general(a, b, (((1,), (1,)), ((), ())), preferred_element_type=F32)

    def mm(a, b):
        return jnp.dot(a.astype(BF16), b.astype(BF16), preferred_element_type=F32)

    def mmf(a, b):
        return jnp.dot(a, b, preferred_element_type=F32, precision=HIGHEST)

    blk_masks = []
    size = 8
    while size <= L:
        blk_masks.append(jnp.bitwise_xor(r2, c2) < size)
        size *= 2
    blk8 = blk_masks[0]

    for c in range(n_chunks):
        sl = pl.ds(c * L, L)
        r = r_ref[0, sl, :]
        k = k_ref[0, sl, :]
        v = v_ref[0, sl, :]
        lw = lw_ref[0, sl, :]
        kk = kk_ref[0, sl, :]
        aic = a_ref[0, sl, :]

        cum = lw
        sh = 1
        while sh < L:
            cum = cum + jnp.where(rowi >= sh, pltpu.roll(cum, sh, axis=0), 0.0)
            sh *= 2
        tot = cum[L - 1:L, :]
        w_inc = jnp.exp(cum)
        w_exc = jnp.exp(cum - lw)
        w_inv = jnp.exp(-cum)
        w_rem = jnp.exp(tot - cum)

        kk = kk / jnp.maximum(jnp.sqrt(head_sum(kk * kk)), 1e-12)
        b_vec = kk * aic
        a_st = stack(-kk * w_exc)
        r_st = stack(r * w_inc)
        bh_st = stack(b_vec * w_inv)
        kh_st = stack(k * w_inv)
        v_st = stack(v)
        ar = jnp.concatenate([a_st, r_st], axis=0)
        bk = jnp.concatenate([bh_st, kh_st], axis=0)
        big = nt(ar, bk)
        n_ab = jnp.where(strict, big[0:L2, 0:L2], 0.0)
        m_ak = jnp.where(strict, big[0:L2, L2:2 * L2], 0.0)
        m_rb = jnp.where(incl, big[L2:2 * L2, 0:L2], 0.0)
        m_rk = jnp.where(incl, big[L2:2 * L2, L2:2 * L2], 0.0)

        d8 = jnp.where(blk8, n_ab, 0.0)
        pw = mmf(d8, d8)
        t_inv = eye + d8
        t_inv = t_inv + mmf(t_inv, pw)
        t_inv = t_inv + mmf(t_inv, mmf(pw, pw))
        for lo, hi in zip(blk_masks[:-1], blk_masks[1:]):
            n_off = jnp.where(jnp.logical_and(hi, jnp.logical_not(lo)), n_ab, 0.0)
            t_inv = t_inv + mmf(t_inv, mmf(n_off, t_inv))

        s_bf = s_scr[...].astype(BF16)
        a_s = nt(ar, s_bf)
        m_v = mm(jnp.concatenate([m_ak, m_rk], axis=0), v_st)
        u = mm(t_inv, a_s[0:L2] + m_v[0:L2])
        y_st = a_s[L2:2 * L2] + m_v[L2:2 * L2] + mm(m_rb, u)
        y = y_st[0:L] + y_st[L:L2]

        uv_t = jnp.concatenate([u, v_st.astype(F32)], axis=0).T.astype(BF16)
        bk_rem = jnp.concatenate([stack(b_vec * w_rem), stack(k * w_rem)], axis=0)
        s_scr[...] = s_scr[...] * jnp.exp(tot) + jnp.dot(uv_t, bk_rem, preferred_element_type=F32)

        inv_n = 1.0 / head_dim
        mean = head_sum(y) * inv_n
        yc = y - mean
        var = head_sum(yc * yc) * inv_n
        y_gn = yc * lax.rsqrt(var + GN_EPS) * gn_g + gn_b
        bonus = head_sum(r * k * rk) * v
        o_ref[0, sl, :] = ((y_gn + bonus) * g_ref[0, sl, :]).astype(o_ref.dtype)


def _rwkv_scan(r, k, v, lw, kk, a, g, r_k, gn_g, gn_b, head_dim):
    bsz, seq, width = r.shape
    assert 2 * head_dim == LANES and width % LANES == 0
    tc = _tile(seq, 8 * RWKV_CHUNK, RWKV_CHUNK)
    tok = pl.BlockSpec((1, tc, LANES), lambda b, h, i: (b, i, h))
    par = pl.BlockSpec((1, LANES), lambda b, h, i: (0, h))
    row = lambda arr: arr.reshape(1, width)
    return pl.pallas_call(
        functools.partial(_rwkv_scan_kernel, n_chunks=tc // RWKV_CHUNK, head_dim=head_dim),
        grid=(bsz, width // LANES, seq // tc),
        in_specs=[tok] * 7 + [par] * 3,
        out_specs=tok,
        out_shape=jax.ShapeDtypeStruct((bsz, seq, width), BF16),
        scratch_shapes=[pltpu.VMEM((LANES, LANES), F32)],
        compiler_params=_params("parallel", "parallel", "arbitrary"),
        name="rwkv_scan",
    )(r, k, v, lw, kk, a, g, row(r_k), row(gn_g), row(gn_b))


def _logf_cumsum_kernel(f_ref, b_ref, o_ref, carry):
    @pl.when(pl.program_id(1) == 0)
    def _():
        carry[...] = jnp.zeros_like(carry)

    z = f_ref[0] + b_ref[...]
    lf = jnp.minimum(z, 0.0) - jnp.log(1.0 + jnp.exp(-jnp.abs(z)))
    ts = z.shape[1]
    ri = lax.broadcasted_iota(jnp.int32, (LANES, LANES), 0)
    ci = lax.broadcasted_iota(jnp.int32, (LANES, LANES), 1)
    upper = (ri <= ci).astype(F32)
    run = carry[...]
    for j in range(ts // LANES):
        blk = jnp.dot(lf[:, j * LANES:(j + 1) * LANES], upper, preferred_element_type=F32,
                      precision=HIGHEST) + run
        o_ref[0, :, j * LANES:(j + 1) * LANES] = blk
        run = blk[:, LANES - 1:LANES]
    carry[...] = run


def _logf_cumsum(f_t, b_f):
    bsz, nh, seq = f_t.shape
    ts = _tile(seq, 2048, LANES)
    return pl.pallas_call(
        _logf_cumsum_kernel,
        grid=(bsz, seq // ts),
        in_specs=[pl.BlockSpec((1, nh, ts), lambda b, i: (b, 0, i)),
                  pl.BlockSpec((nh, 1), lambda b, i: (0, 0))],
        out_specs=pl.BlockSpec((1, nh, ts), lambda b, i: (b, 0, i)),
        out_shape=jax.ShapeDtypeStruct((bsz, nh, seq), F32),
        scratch_shapes=[pltpu.VMEM((nh, 1), F32)],
        compiler_params=_params("parallel", "arbitrary"),
        name="fox_logf_cumsum",
    )(f_t, b_f.reshape(nh, 1))


def _fox_kernel(qi_ref, ki_ref, q_ref, k_ref, v_ref, cq_ref, ck_ref, og_ref, o_ref,
                m_scr, l_scr, acc_scr):
    p = pl.program_id(2)
    qi = qi_ref[p]
    ki = ki_ref[p]

    @pl.when(ki == 0)
    def _():
        m_scr[...] = jnp.full_like(m_scr, NEG_BIG)
        l_scr[...] = jnp.zeros_like(l_scr)
        acc_scr[...] = jnp.zeros_like(acc_scr)

    def logits():
        s = lax.dot_general(q_ref[0], k_ref[0], (((1,), (1,)), ((), ())),
                            preferred_element_type=F32)
        return s + (cq_ref[0, 0] - ck_ref[0, 0])

    def update(s):
        m_prev = m_scr[...]
        m_new = jnp.maximum(m_prev, jnp.max(s, axis=-1, keepdims=True))
        alpha = jnp.exp(m_prev - m_new)
        pr = jnp.exp(s - m_new)
        l_scr[...] = alpha * l_scr[...] + jnp.sum(pr, axis=-1, keepdims=True)
        acc_scr[...] = alpha * acc_scr[...] + jnp.dot(pr.astype(BF16), v_ref[0],
                                                      preferred_element_type=F32)
        m_scr[...] = m_new

    @pl.when(ki < qi)
    def _():
        update(logits())

    @pl.when(ki == qi)
    def _():
        s = logits()
        ri = lax.broadcasted_iota(jnp.int32, s.shape, 0)
        ci = lax.broadcasted_iota(jnp.int32, s.shape, 1)
        update(jnp.where(ci <= ri, s, NEG_BIG))
        o = acc_scr[...] / l_scr[...]
        o_ref[0] = (o * _sigmoid(og_ref[0])).astype(o_ref.dtype)


def _fox_attention(qk, v, cum, ogf, n_heads):
    bsz, seq, _ = v.shape
    t = _tile(seq, 512, LANES)
    nq = seq // t
    qi_list, ki_list = [], []
    for qi in range(nq):
        for ki in range(qi + 1):
            qi_list.append(qi)
            ki_list.append(ki)
    qi_arr = jnp.asarray(qi_list, jnp.int32)
    ki_arr = jnp.asarray(ki_list, jnp.int32)
    cum_col = cum[..., None]
    cum_row = cum[:, :, None, :]
    nh = n_heads
    grid_spec = pltpu.PrefetchScalarGridSpec(
        num_scalar_prefetch=2,
        grid=(bsz, nh, len(qi_list)),
        in_specs=[pl.BlockSpec((1, t, LANES), lambda b, h, p, qi, ki: (b, qi[p], h)),
                  pl.BlockSpec((1, t, LANES), lambda b, h, p, qi, ki: (b, ki[p], nh + h)),
                  pl.BlockSpec((1, t, LANES), lambda b, h, p, qi, ki: (b, ki[p], h)),
                  pl.BlockSpec((1, 1, t, 1), lambda b, h, p, qi, ki: (b, h, qi[p], 0)),
                  pl.BlockSpec((1, 1, 1, t), lambda b, h, p, qi, ki: (b, h, 0, ki[p])),
                  pl.BlockSpec((1, t, LANES), lambda b, h, p, qi, ki: (b, qi[p], h))],
        out_specs=pl.BlockSpec((1, t, LANES), lambda b, h, p, qi, ki: (b, qi[p], h)),
        scratch_shapes=[pltpu.VMEM((t, 1), F32), pltpu.VMEM((t, 1), F32),
                        pltpu.VMEM((t, LANES), F32)],
    )
    return pl.pallas_call(
        _fox_kernel,
        grid_spec=grid_spec,
        out_shape=jax.ShapeDtypeStruct((bsz, seq, nh * LANES), BF16),
        compiler_params=_params("parallel", "parallel", "arbitrary"),
        name="fox_attention",
    )(qi_arr, ki_arr, qk, qk, v, cum_col, cum_row, ogf)


def _merge_kernel(ya_ref, yb_ref, ga_ref, gb_ref, wa_ref, wb_ref, o_ref):
    pa = jnp.dot(ya_ref[0], wa_ref[...], preferred_element_type=F32)
    pb = jnp.dot(yb_ref[0], wb_ref[...], preferred_element_type=F32)
    o_ref[0] = (ga_ref[0].astype(F32) * pa + gb_ref[0].astype(F32) * pb).astype(o_ref.dtype)


def _merge(y_a, y_b, gates, w_a, w_b):
    bsz, seq, wa = y_a.shape
    wb = y_b.shape[2]
    d = w_a.shape[1]
    tm = _tile(seq, 512)
    return pl.pallas_call(
        _merge_kernel,
        grid=(bsz, seq // tm),
        in_specs=[pl.BlockSpec((1, tm, wa), lambda b, i: (b, i, 0)),
                  pl.BlockSpec((1, tm, wb), lambda b, i: (b, i, 0)),
                  pl.BlockSpec((1, tm, d), lambda b, i: (b, i, 0)),
                  pl.BlockSpec((1, tm, d), lambda b, i: (b, i, 1)),
                  pl.BlockSpec((wa, d), lambda b, i: (0, 0)),
                  pl.BlockSpec((wb, d), lambda b, i: (0, 0))],
        out_specs=pl.BlockSpec((1, tm, d), lambda b, i: (b, i, 0)),
        out_shape=jax.ShapeDtypeStruct((bsz, seq, d), BF16),
        compiler_params=_params("parallel", "parallel"),
        name="branch_merge",
    )(y_a, y_b, gates, gates, w_a, w_b)


def _route_kernel(m_ref, x_ref, g1_ref, w_ref, n2_ref, sc_ref, sh_ref, wr_ref, br_ref,
                  x1_ref, route_ref, cnt_ref, base, *, n_experts):
    @pl.when(jnp.logical_and(pl.program_id(0) == 0, pl.program_id(1) == 0))
    def _():
        base[...] = jnp.zeros_like(base)

    x1 = x_ref[0] + g1_ref[0] * jnp.dot(m_ref[0], w_ref[...], preferred_element_type=F32)
    x1_ref[0] = x1
    h = _rms_mod(x1, n2_ref[...], sc_ref[0], sh_ref[0])
    logits = jnp.dot(h, wr_ref[...], preferred_element_type=F32, precision=HIGHEST) + br_ref[...]
    tm = logits.shape[0]
    lane_i = lax.broadcasted_iota(jnp.int32, (tm, LANES), 1)
    lane = lane_i.astype(F32)
    vals = jnp.where(lane_i < n_experts, logits, -jnp.inf)
    top_v, top_i, hot = [], [], []
    for _ in range(TOP_K):
        mx = jnp.max(vals, axis=-1, keepdims=True)
        ix = jnp.min(jnp.where(vals == mx, lane, float(LANES)), axis=-1, keepdims=True)
        sel = lane == ix
        vals = jnp.where(sel, -jnp.inf, vals)
        top_v.append(mx)
        top_i.append(ix)
        hot.append(sel.astype(F32))
    ex = [jnp.exp(tv - top_v[0]) for tv in top_v]
    den = ex[0] + ex[1] + ex[2] + ex[3]
    cnt = hot[0] + hot[1] + hot[2] + hot[3]
    ri = lax.broadcasted_iota(jnp.int32, (tm, tm), 0)
    ci = lax.broadcasted_iota(jnp.int32, (tm, tm), 1)
    before = jnp.dot((ci < ri).astype(BF16), cnt.astype(BF16), preferred_element_type=F32)
    before = before + base[...]
    out = jnp.zeros((tm, LANES), F32)
    for kk in range(TOP_K):
        rank = jnp.sum(hot[kk] * before, axis=-1, keepdims=True)
        out = jnp.where(lane_i == kk, top_i[kk], out)
        out = jnp.where(lane_i == TOP_K + kk, ex[kk] / den, out)
        out = jnp.where(lane_i == 2 * TOP_K + kk, rank, out)
    route_ref[0] = out
    new_base = base[...] + jnp.sum(cnt, axis=0, keepdims=True)
    base[...] = new_base
    cnt_ref[...] = jnp.broadcast_to(new_base, cnt_ref.shape)


def _route(merged, x, gate1, w_out, norm2_g, scale2, shift2, w_router, b_router):
    bsz, seq, d = x.shape
    n_experts = w_router.shape[1]
    tm = _tile(seq, 512)
    wr = jnp.zeros((d, LANES), F32).at[:, :n_experts].set(w_router)
    br = jnp.zeros((1, LANES), F32).at[0, :n_experts].set(b_router)
    mod = pl.BlockSpec((1, 1, d), lambda b, i: (b, 0, 0))
    tok = pl.BlockSpec((1, tm, d), lambda b, i: (b, i, 0))
    return pl.pallas_call(
        functools.partial(_route_kernel, n_experts=n_experts),
        grid=(bsz, seq // tm),
        in_specs=[tok, tok, mod,
                  pl.BlockSpec((d, d), lambda b, i: (0, 0)),
                  pl.BlockSpec((1, d), lambda b, i: (0, 0)), mod, mod,
                  pl.BlockSpec((d, LANES), lambda b, i: (0, 0)),
                  pl.BlockSpec((1, LANES), lambda b, i: (0, 0))],
        out_specs=[tok,
                   pl.BlockSpec((1, tm, LANES), lambda b, i: (b, i, 0)),
                   pl.BlockSpec((8, LANES), lambda b, i: (0, 0))],
        out_shape=[jax.ShapeDtypeStruct((bsz, seq, d), F32),
                   jax.ShapeDtypeStruct((bsz, seq, LANES), F32),
                   jax.ShapeDtypeStruct((8, LANES), F32)],
        scratch_shapes=[pltpu.VMEM((1, LANES), F32)],
        compiler_params=_params("arbitrary", "arbitrary"),
        name="residual_router",
    )(merged, x, gate1, w_out, norm2_g.reshape(1, d), scale2, shift2, wr, br)


def _dispatch_kernel(slot_ref, x_ref, n2_ref, sc_ref, sh_ref, xs_in_ref, xs_ref, hbuf, sem):
    del xs_in_ref
    i = pl.program_id(0)
    tm = hbuf.shape[0]
    hbuf[...] = _rms_mod(x_ref[...], n2_ref[...], sc_ref[...], sh_ref[...])

    def row_copy(r, s):
        return pltpu.make_async_copy(hbuf.at[pl.ds(r, 1)], xs_ref.at[pl.ds(s, 1)], sem)

    def issue(r, carry):
        for kk in range(TOP_K):
            row_copy(r, slot_ref[(i * tm + r) * TOP_K + kk]).start()
        return carry

    lax.fori_loop(0, tm, issue, 0)

    def drain(r, carry):
        for kk in range(TOP_K):
            row_copy(0, 0).wait()
        return carry

    lax.fori_loop(0, tm, drain, 0)


def _dispatch(x1_b, slots_b, norm2_g, scale2_b, shift2_b, xs):
    seq, d = x1_b.shape
    tm = _tile(seq, 256)
    grid_spec = pltpu.PrefetchScalarGridSpec(
        num_scalar_prefetch=1,
        grid=(seq // tm,),
        in_specs=[pl.BlockSpec((tm, d), lambda i, s: (i, 0)),
                  pl.BlockSpec((1, d), lambda i, s: (0, 0)),
                  pl.BlockSpec((1, d), lambda i, s: (0, 0)),
                  pl.BlockSpec((1, d), lambda i, s: (0, 0)),
                  pl.BlockSpec(memory_space=pl.ANY)],
        out_specs=pl.BlockSpec(memory_space=pl.ANY),
        scratch_shapes=[pltpu.VMEM((tm, d), F32), pltpu.SemaphoreType.DMA(())],
    )
    return pl.pallas_call(
        _dispatch_kernel,
        grid_spec=grid_spec,
        out_shape=jax.ShapeDtypeStruct(xs.shape, xs.dtype),
        input_output_aliases={5: 0},
        compiler_params=_params("arbitrary"),
        name="moe_dispatch",
    )(slots_b, x1_b, norm2_g.reshape(1, d), scale2_b, shift2_b, xs)


def _expert_kernel(be_ref, na_ref, x_ref, wg_ref, wl_ref, bg_ref, bl_ref, wd_ref, bd_ref,
                   o_ref, acc, *, n_f):
    b = pl.program_id(0)
    f = pl.program_id(1)

    @pl.when(b < na_ref[0])
    def _():
        x = x_ref[...].astype(BF16)
        gate = jnp.dot(x, wg_ref[0], preferred_element_type=F32) + bg_ref[0]
        lin = jnp.dot(x, wl_ref[0], preferred_element_type=F32) + bl_ref[0]
        gate = jnp.minimum(gate, SWIGLU_LIMIT)
        lin = jnp.clip(lin, -SWIGLU_LIMIT, SWIGLU_LIMIT)
        act = gate * _sigmoid(SWIGLU_ALPHA * gate) * (lin + 1.0)
        contrib = jnp.dot(act.astype(BF16), wd_ref[0], preferred_element_type=F32)

        @pl.when(f == 0)
        def _():
            acc[...] = contrib

        @pl.when(f > 0)
        def _():
            acc[...] += contrib

        @pl.when(f == n_f - 1)
        def _():
            o_ref[...] = acc[...] + bd_ref[0]

    @pl.when(jnp.logical_and(b >= na_ref[0], f == n_f - 1))
    def _():
        o_ref[...] = jnp.zeros_like(o_ref)


def _experts(xs, blk_e, n_active, w_gu, b_gu, w_dn, b_dn):
    n_slots, d = xs.shape
    n_e, _, two_ff = w_gu.shape
    d_ff = two_ff // 2
    bm = EXPERT_ROWS
    n_blocks = n_slots // bm
    tf = _tile(d_ff, 512, LANES)
    n_f = d_ff // tf

    def blk(b, na):
        return jnp.minimum(b, na[0] - 1)

    def ff(b, f, na):
        return jnp.where(b < na[0], f, n_f - 1)

    grid_spec = pltpu.PrefetchScalarGridSpec(
        num_scalar_prefetch=2,
        grid=(n_blocks, n_f),
        in_specs=[pl.BlockSpec((bm, d), lambda b, f, be, na: (blk(b, na), 0)),
                  pl.BlockSpec((1, d, tf), lambda b, f, be, na: (be[blk(b, na)], 0, ff(b, f, na))),
                  pl.BlockSpec((1, d, tf), lambda b, f, be, na: (be[blk(b, na)], 0, n_f + ff(b, f, na))),
                  pl.BlockSpec((1, 1, tf), lambda b, f, be, na: (be[blk(b, na)], 0, ff(b, f, na))),
                  pl.BlockSpec((1, 1, tf), lambda b, f, be, na: (be[blk(b, na)], 0, n_f + ff(b, f, na))),
                  pl.BlockSpec((1, tf, d), lambda b, f, be, na: (be[blk(b, na)], ff(b, f, na), 0)),
                  pl.BlockSpec((1, 1, d), lambda b, f, be, na: (be[blk(b, na)], 0, 0))],
        out_specs=pl.BlockSpec((bm, d), lambda b, f, be, na: (b, 0)),
        scratch_shapes=[pltpu.VMEM((bm, d), F32)],
    )
    return pl.pallas_call(
        functools.partial(_expert_kernel, n_f=n_f),
        grid_spec=grid_spec,
        out_shape=jax.ShapeDtypeStruct((n_slots, d), F32),
        compiler_params=_params("arbitrary", "arbitrary"),
        name="moe_experts",
    )(blk_e, n_active, xs, w_gu, w_gu, b_gu.reshape(n_e, 1, two_ff), b_gu.reshape(n_e, 1, two_ff),
      w_dn, b_dn.reshape(n_e, 1, d))


def _combine_kernel(slot_ref, x_ref, route_ref, g2_ref, gf_ref, ys_ref, o_ref, buf, sem):
    i = pl.program_id(0)
    tm = x_ref.shape[0]

    def row_copy(kk, r, s):
        return pltpu.make_async_copy(ys_ref.at[pl.ds(s, 1)], buf.at[kk, pl.ds(r, 1)], sem)

    def issue(r, carry):
        for kk in range(TOP_K):
            row_copy(kk, r, slot_ref[(i * tm + r) * TOP_K + kk]).start()
        return carry

    lax.fori_loop(0, tm, issue, 0)

    def drain(r, carry):
        for kk in range(TOP_K):
            row_copy(0, 0, 0).wait()
        return carry

    lax.fori_loop(0, tm, drain, 0)

    route = route_ref[...]
    y = jnp.zeros(x_ref.shape, F32)
    for kk in range(TOP_K):
        y = y + buf[kk] * route[:, TOP_K + kk:TOP_K + kk + 1]
    x2 = x_ref[...] + g2_ref[...] * y
    ms = jnp.mean(x2 * x2, axis=-1, keepdims=True)
    o_ref[...] = x2 * lax.rsqrt(ms + NORM_EPS) * gf_ref[...]


def _combine(x1_b, route_b, slots_b, gate2_b, norm_final_g, ys):
    seq, d = x1_b.shape
    tm = _tile(seq, 256)
    grid_spec = pltpu.PrefetchScalarGridSpec(
        num_scalar_prefetch=1,
        grid=(seq // tm,),
        in_specs=[pl.BlockSpec((tm, d), lambda i, s: (i, 0)),
                  pl.BlockSpec((tm, LANES), lambda i, s: (i, 0)),
                  pl.BlockSpec((1, d), lambda i, s: (0, 0)),
                  pl.BlockSpec((1, d), lambda i, s: (0, 0)),
                  pl.BlockSpec(memory_space=pl.ANY)],
        out_specs=pl.BlockSpec((tm, d), lambda i, s: (i, 0)),
        scratch_shapes=[pltpu.VMEM((TOP_K, tm, d), F32), pltpu.SemaphoreType.DMA(())],
    )
    return pl.pallas_call(
        _combine_kernel,
        grid_spec=grid_spec,
        out_shape=jax.ShapeDtypeStruct((seq, d), F32),
        compiler_params=_params("arbitrary"),
        name="moe_combine",
    )(slots_b, x1_b, route_b, gate2_b, norm_final_g.reshape(1, d), ys)


def _pad_cols(w, n):
    return jnp.pad(w, ((0, 0), (0, n - w.shape[1])))


def kernel(x, c, w_ada, b_ada, norm1_g, w_in, rwkv_mu, w_decay_up, decay_w0, w_iclr_up, iclr_a0, w_gate_up_rwkv, rwkv_k_k, rwkv_k_a, rwkv_r_k, rwkv_gn_g, rwkv_gn_b, w_out_a, fox_b_f, fox_q_norm, fox_k_norm, w_out_b, w_out, norm2_g, w_router, b_router, w_expert_gu, b_expert_gu, w_expert_down, b_expert_down, norm_final_g):
    bsz, seq, d = x.shape
    rw_heads, rw_hd = rwkv_r_k.shape
    rw = rw_heads * rw_hd
    dr, ir, gr = w_decay_up.shape[0], w_iclr_up.shape[0], w_gate_up_rwkv.shape[0]
    fh = fox_b_f.shape[0]
    fw = w_out_b.shape[0]
    fhd = fw // fh
    assert fhd == LANES and dr <= LANES and ir <= LANES and gr % LANES == 0
    n_experts = w_router.shape[1]

    mod = _ada(c, w_ada, b_ada)
    shift1, scale1, gate1, shift2, scale2, gate2 = (m[:, None, :] for m in jnp.split(mod, 6, axis=-1))

    o_r = 0
    o_f = 3 * rw + dr + ir + gr
    o_g = o_f + 3 * fw + fh + fw
    col = lambda a, n: w_in[:, a:a + n]
    w_rwkv = jnp.concatenate([col(0, 3 * rw), _pad_cols(col(3 * rw, dr), LANES),
                              _pad_cols(col(3 * rw + dr, ir), LANES), col(3 * rw + dr + ir, gr)],
                             axis=1).astype(BF16)
    mu = jnp.concatenate([rwkv_mu[:3 * rw], jnp.pad(rwkv_mu[3 * rw:3 * rw + dr], (0, LANES - dr)),
                          jnp.pad(rwkv_mu[3 * rw + dr:3 * rw + dr + ir], (0, LANES - ir)),
                          rwkv_mu[3 * rw + dr + ir:]])
    w_qk = col(o_f, 2 * fw).astype(BF16)
    w_v = col(o_f + 2 * fw, fw).astype(BF16)
    w_ogf = jnp.concatenate([col(o_f + 3 * fw + fh, fw), _pad_cols(col(o_f + 3 * fw, fh), LANES)],
                            axis=1).astype(BF16)
    w_gates = col(o_g, 2 * d).astype(BF16)
    qk_gain = jnp.concatenate([jnp.tile(fox_q_norm * (fhd ** -0.5), fh), jnp.tile(fox_k_norm, fh)])

    def proj(w, extra, dtype, ep, name):
        if extra is None:
            extra = jnp.zeros((w.shape[1],), F32)
        return _inproj(x, norm1_g, scale1, shift1, w, extra.reshape(1, -1), dtype, ep, name)

    p_rwkv = proj(w_rwkv, None, F32, _ep_identity, "inproj_rwkv")
    qk = proj(w_qk, qk_gain, BF16, _ep_headnorm, "inproj_fox_qk")
    v_fox = proj(w_v, None, BF16, _ep_identity, "inproj_fox_v")
    ogf = proj(w_ogf, None, F32, _ep_identity, "inproj_fox_gate_forget")
    gates = proj(w_gates, None, BF16, _ep_sigmoid, "inproj_merge_gates")

    wd = jnp.pad(w_decay_up, ((0, LANES - dr), (0, 0)))
    wi = jnp.pad(w_iclr_up, ((0, LANES - ir), (0, 0)))
    r, k2, v, lw, kk, aic, g = _rwkv_prep(p_rwkv, mu, wd, decay_w0, wi, iclr_a0,
                                          w_gate_up_rwkv.astype(BF16), rwkv_k_k, rwkv_k_a, rw, gr)
    y_a = _rwkv_scan(r, k2, v, lw, kk, aic, g, rwkv_r_k, rwkv_gn_g, rwkv_gn_b, rw_hd)

    f_t = jnp.transpose(ogf[:, :, fw:fw + fh], (0, 2, 1))
    cum = _logf_cumsum(f_t, fox_b_f)
    y_b = _fox_attention(qk, v_fox, cum, ogf, fh)

    merged = _merge(y_a, y_b, gates, w_out_a.astype(BF16), w_out_b.astype(BF16))
    x1, route, counts = _route(merged, x, gate1, w_out.astype(BF16), norm2_g, scale2, shift2,
                               w_router, b_router)

    n_tok = bsz * seq
    n_assign = n_tok * TOP_K
    bm = EXPERT_ROWS
    n_blocks = -(-n_assign // bm) + n_experts
    cnt = counts[0, :n_experts].astype(jnp.int32)
    padded = (cnt + bm - 1) // bm * bm
    pad_end = jnp.cumsum(padded)
    pad_start = pad_end - padded
    top_i = route[:, :, 0:TOP_K].astype(jnp.int32)
    rank = route[:, :, 2 * TOP_K:3 * TOP_K].astype(jnp.int32)
    slots = (pad_start[top_i] + rank).reshape(bsz, seq * TOP_K)
    blk_e = jnp.minimum(jnp.searchsorted(pad_end, jnp.arange(n_blocks, dtype=jnp.int32) * bm,
                                         side='right'), n_experts - 1).astype(jnp.int32)
    n_active = (pad_end[-1:] // bm).astype(jnp.int32)

    xs = jnp.zeros((n_blocks * bm, d), F32)
    for b in range(bsz):
        xs = _dispatch(x1[b], slots[b], norm2_g, scale2[b], shift2[b], xs)
    ys = _experts(xs, blk_e, n_active, w_expert_gu.astype(BF16), b_expert_gu,
                  w_expert_down.astype(BF16), b_expert_down)
    outs = [_combine(x1[b], route[b], slots[b], gate2[b], norm_final_g, ys) for b in range(bsz)]
    return jnp.stack(outs, axis=0)
```

```python
import functools

import jax
import jax.numpy as jnp
from jax import lax
from jax.experimental import pallas as pl
from jax.experimental.pallas import tpu as pltpu

F32 = jnp.float32
BF16 = jnp.bfloat16
HIGHEST = lax.Precision.HIGHEST

TOP_K = 4
NORM_EPS = 1e-6
GN_EPS = 64e-5
SWIGLU_LIMIT = 7.0
SWIGLU_ALPHA = 1.702
LANES = 128
RWKV_CHUNK = 64
VMEM_LIMIT_BYTES = 56 * 1024 * 1024
EXPERT_ROWS = 512
NEG_BIG = -1e30
LOG2E = 1.4426950408889634
FOX_ROWS = 256
FOX_TQ = 1024
FOX_TK = 512


def _tile(n, pref, mult=8):
    t = min(pref, n)
    t -= t % mult
    while t >= mult:
        if n % t == 0:
            return t
        t -= mult
    return n


def _params(*sem):
    return pltpu.CompilerParams(dimension_semantics=sem, vmem_limit_bytes=VMEM_LIMIT_BYTES)


def _sigmoid(x):
    return 1.0 / (1.0 + jnp.exp(-x))


def _rms_mod(x, g, scale, shift):
    ms = jnp.mean(x * x, axis=-1, keepdims=True)
    return x * lax.rsqrt(ms + NORM_EPS) * g * (1.0 + scale) + shift


def _ada_kernel(c_ref, w_ref, b_ref, o_ref):
    c = c_ref[...]
    s = c * _sigmoid(c)
    o_ref[...] = jnp.dot(s, w_ref[...], preferred_element_type=F32, precision=HIGHEST) + b_ref[...]


def _ada(c, w_ada, b_ada):
    bsz, d = c.shape
    n = w_ada.shape[1]
    rows = 8
    c_pad = jnp.zeros((rows, d), F32).at[:bsz].set(c)
    tn = _tile(n, 1024, LANES)
    out = pl.pallas_call(
        _ada_kernel,
        grid=(n // tn,),
        in_specs=[pl.BlockSpec((rows, d), lambda j: (0, 0)),
                  pl.BlockSpec((d, tn), lambda j: (0, j)),
                  pl.BlockSpec((1, tn), lambda j: (0, j))],
        out_specs=pl.BlockSpec((rows, tn), lambda j: (0, j)),
        out_shape=jax.ShapeDtypeStruct((rows, n), F32),
        compiler_params=_params("parallel"),
        name="adaln",
    )(c_pad, w_ada, b_ada.reshape(1, n))
    return out[:bsz]


def _inproj_kernel(x_ref, g_ref, sc_ref, sh_ref, w_ref, e_ref, o_ref, h_scr, *, epilogue):
    @pl.when(pl.program_id(2) == 0)
    def _():
        h = _rms_mod(x_ref[0], g_ref[...], sc_ref[0], sh_ref[0])
        h_scr[...] = h.astype(BF16)

    acc = jnp.dot(h_scr[...], w_ref[...], preferred_element_type=F32)
    o_ref[0] = epilogue(acc, e_ref[...]).astype(o_ref.dtype)


def _ep_identity(acc, extra):
    return acc


def _ep_sigmoid(acc, extra):
    return _sigmoid(acc)


def _ep_headnorm(acc, extra):
    outs = []
    for h in range(acc.shape[1] // LANES):
        a = acc[:, h * LANES:(h + 1) * LANES]
        ms = jnp.mean(a * a, axis=-1, keepdims=True)
        outs.append(a * lax.rsqrt(ms + NORM_EPS))
    return jnp.concatenate(outs, axis=1) * extra


def _inproj(x, g, scale, shift, w, extra, out_dtype, epilogue, name):
    bsz, seq, d = x.shape
    n = w.shape[1]
    tm = _tile(seq, 1024)
    tn = _tile(n, 512, LANES)
    return pl.pallas_call(
        functools.partial(_inproj_kernel, epilogue=epilogue),
        grid=(bsz, seq // tm, n // tn),
        in_specs=[pl.BlockSpec((1, tm, d), lambda b, i, j: (b, i, 0)),
                  pl.BlockSpec((1, d), lambda b, i, j: (0, 0)),
                  pl.BlockSpec((1, 1, d), lambda b, i, j: (b, 0, 0)),
                  pl.BlockSpec((1, 1, d), lambda b, i, j: (b, 0, 0)),
                  pl.BlockSpec((d, tn), lambda b, i, j: (0, j)),
                  pl.BlockSpec((1, tn), lambda b, i, j: (0, j))],
        out_specs=pl.BlockSpec((1, tm, tn), lambda b, i, j: (b, i, j)),
        out_shape=jax.ShapeDtypeStruct((bsz, seq, n), out_dtype),
        scratch_shapes=[pltpu.VMEM((tm, d), BF16)],
        compiler_params=_params("parallel", "parallel", "arbitrary"),
        name=name,
    )(x, g.reshape(1, d), scale, shift, w, extra)


def _rwkv_prep_kernel(p_ref, mu_ref, wd_ref, w0_ref, wi_ref, a0_ref, wg_ref, kk_ref, ka_ref,
                      r_o, k_o, v_o, lw_o, kk_o, a_o, g_o, carry, *, width, gate_rank):
    @pl.when(pl.program_id(1) == 0)
    def _():
        carry[...] = jnp.zeros_like(carry)

    p = p_ref[0]
    tt = p.shape[0]
    row = lax.broadcasted_iota(jnp.int32, p.shape, 0)
    prev = jnp.where(row == 0, carry[...], pltpu.roll(p, 1, axis=0))
    carry[...] = p[tt - 1:tt, :]
    pm = p + mu_ref[...] * (prev - p)
    w = width
    r = pm[:, 0:w]
    k = pm[:, w:2 * w]
    v = pm[:, 2 * w:3 * w]
    d_lo = pm[:, 3 * w:3 * w + LANES]
    a_lo = pm[:, 3 * w + LANES:3 * w + 2 * LANES]
    g_lo = pm[:, 3 * w + 2 * LANES:3 * w + 2 * LANES + gate_rank]
    w_pre = w0_ref[...] + jnp.dot(jnp.tanh(d_lo), wd_ref[...], preferred_element_type=F32,
                                  precision=HIGHEST)
    lw_o[0] = -jnp.exp(-0.5) * _sigmoid(w_pre)
    a = _sigmoid(a0_ref[...] + jnp.dot(a_lo, wi_ref[...], preferred_element_type=F32,
                                       precision=HIGHEST))
    g_o[0] = jnp.dot(_sigmoid(g_lo).astype(BF16), wg_ref[...], preferred_element_type=F32)
    r_o[0] = r
    v_o[0] = v
    a_o[0] = a
    kk_o[0] = k * kk_ref[...]
    k_o[0] = k * (1.0 + (a - 1.0) * ka_ref[...])


def _rwkv_prep(p, mu, wd, w0, wi, a0, wg, k_k, k_a, width, gate_rank):
    bsz, seq, n = p.shape
    tt = _tile(seq, 256)
    row = lambda arr: arr.reshape(1, -1)
    full = lambda shape: pl.BlockSpec(shape, lambda b, i: (0,) * len(shape))
    out_sds = jax.ShapeDtypeStruct((bsz, seq, width), F32)
    out_spec = pl.BlockSpec((1, tt, width), lambda b, i: (b, i, 0))
    return pl.pallas_call(
        functools.partial(_rwkv_prep_kernel, width=width, gate_rank=gate_rank),
        grid=(bsz, seq // tt),
        in_specs=[pl.BlockSpec((1, tt, n), lambda b, i: (b, i, 0)),
                  full((1, n)), full(wd.shape), full((1, width)), full(wi.shape), full((1, width)),
                  full(wg.shape), full((1, width)), full((1, width))],
        out_specs=[out_spec] * 7,
        out_shape=[out_sds] * 7,
        scratch_shapes=[pltpu.VMEM((1, n), F32)],
        compiler_params=_params("parallel", "arbitrary"),
        name="rwkv_prep",
    )(p, row(mu), wd, row(w0), wi, row(a0), wg, row(k_k), row(k_a))


def _rwkv_scan_kernel(r_ref, k_ref, v_ref, lw_ref, kk_ref, a_ref, g_ref, rk_ref, gg_ref, gb_ref,
                      o_ref, s_scr, *, n_chunks, head_dim):
    L = RWKV_CHUNK
    L2 = 2 * L

    @pl.when(pl.program_id(2) == 0)
    def _():
        s_scr[...] = jnp.zeros_like(s_scr)

    lane = lax.broadcasted_iota(jnp.int32, (1, LANES), 1)
    m0 = (lane < head_dim).astype(F32)
    m1 = 1.0 - m0
    r2 = lax.broadcasted_iota(jnp.int32, (L2, L2), 0)
    c2 = lax.broadcasted_iota(jnp.int32, (L2, L2), 1)
    same = (r2 < L) == (c2 < L)
    strict = jnp.logical_and(same, c2 < r2)
    incl = jnp.logical_and(same, c2 <= r2)
    eye = (r2 == c2).astype(F32)
    rowi = lax.broadcasted_iota(jnp.int32, (L, LANES), 0)
    rk = rk_ref[...]
    gn_g = gg_ref[...]
    gn_b = gb_ref[...]

    def stack_f32(x):
        return jnp.concatenate([x * m0, x * m1], axis=0)

    def stack(x):
        return stack_f32(x).astype(BF16)

    def head_sum(x):
        s0 = jnp.sum(x * m0, axis=-1, keepdims=True)
        s1 = jnp.sum(x * m1, axis=-1, keepdims=True)
        return s0 * m0 + s1 * m1

    def nt(a, b):
        return lax.dot_general(a, b, (((1,), (1,)), ((), ())), preferred_element_type=F32)

    def mm(a, b):
        return jnp.dot(a.astype(BF16), b.astype(BF16), preferred_element_type=F32)

    blk_masks = []
    size = 8
    while size <= L:
        blk_masks.append(jnp.bitwise_xor(r2, c2) < size)
        size *= 2
    off_masks = [jnp.logical_and(hi, jnp.logical_not(lo))
                 for lo, hi in zip(blk_masks[:-1], blk_masks[1:])]
    chunks = range(n_chunks)

    ctx = []
    for c in chunks:
        sl = pl.ds(c * L, L)
        r = r_ref[0, sl, :]
        k = k_ref[0, sl, :]
        v = v_ref[0, sl, :]
        lw = lw_ref[0, sl, :]
        kk = kk_ref[0, sl, :]
        aic = a_ref[0, sl, :]
        cum = lw
        sh = 1
        while sh < L:
            cum = cum + jnp.where(rowi >= sh, pltpu.roll(cum, sh, axis=0), 0.0)
            sh *= 2
        tot = cum[L - 1:L, :]
        w_rem = jnp.exp(tot - cum)
        w_inv = jnp.exp(-cum)
        kk = kk / jnp.maximum(jnp.sqrt(head_sum(kk * kk)), 1e-12)
        b_vec = kk * aic
        ar = jnp.concatenate([stack(-kk * jnp.exp(cum - lw)), stack(r * jnp.exp(cum))], axis=0)
        bk = jnp.concatenate([stack(b_vec * w_inv), stack(k * w_inv)], axis=0)
        v_st = stack(v)
        b_rem_t = stack_f32(b_vec * w_rem).T.astype(BF16)
        k_rem_t = stack_f32(k * w_rem).T.astype(BF16)
        decay = jnp.broadcast_to(jnp.exp(tot), (LANES, LANES)).T
        ctx.append(dict(sl=sl, ar=ar, bk=bk, v_st=v_st, decay=decay, b_rem_t=b_rem_t,
                        k_rem_t=k_rem_t, bonus=head_sum(r * k * rk) * v))

    for x in ctx:
        big = nt(x["ar"], x["bk"])
        x["n_ab"] = jnp.where(strict, big[0:L2, 0:L2], 0.0)
        m_ak = jnp.where(strict, big[0:L2, L2:2 * L2], 0.0)
        m_rk = jnp.where(incl, big[L2:2 * L2, L2:2 * L2], 0.0)
        x["m_rb"] = jnp.where(incl, big[L2:2 * L2, 0:L2], 0.0).astype(BF16)
        x["m_akrk"] = jnp.concatenate([m_ak, m_rk], axis=0).astype(BF16)
    for x in ctx:
        x["m_v"] = mm(x["m_akrk"], x["v_st"])
        x["kv"] = mm(x["k_rem_t"], x["v_st"])

    for x in ctx:
        x["d8"] = jnp.where(blk_masks[0], x["n_ab"], 0.0)
        x["t"] = eye + x["d8"]
    for x in ctx:
        x["pw"] = mm(x["d8"], x["d8"])
    for x in ctx:
        x["t"] = x["t"] + mm(x["t"], x["pw"])
    for x in ctx:
        x["pw"] = mm(x["pw"], x["pw"])
    for x in ctx:
        x["t"] = x["t"] + mm(x["t"], x["pw"])
    for off in off_masks:
        for x in ctx:
            x["nt"] = mm(jnp.where(off, x["n_ab"], 0.0), x["t"])
        for x in ctx:
            x["t"] = x["t"] + mm(x["t"], x["nt"])

    s = s_scr[...]
    for x in ctx:
        a_s = mm(x["ar"], s)
        u = mm(x["t"], a_s[0:L2] + x["m_v"][0:L2])
        x["y_st"] = a_s[L2:2 * L2] + x["m_v"][L2:2 * L2] + mm(x["m_rb"], u)
        s = x["decay"] * s + mm(x["b_rem_t"], u) + x["kv"]
    s_scr[...] = s

    inv_n = 1.0 / head_dim
    for x in ctx:
        y_st = x["y_st"]
        y = y_st[0:L] + y_st[L:L2]
        mean = head_sum(y) * inv_n
        yc = y - mean
        var = head_sum(yc * yc) * inv_n
        y_gn = yc * lax.rsqrt(var + GN_EPS) * gn_g + gn_b
        o_ref[0, x["sl"], :] = ((y_gn + x["bonus"]) * g_ref[0, x["sl"], :]).astype(o_ref.dtype)


def _rwkv_scan(r, k, v, lw, kk, a, g, r_k, gn_g, gn_b, head_dim):
    bsz, seq, width = r.shape
    assert 2 * head_dim == LANES and width % LANES == 0
    tc = _tile(seq, 8 * RWKV_CHUNK, RWKV_CHUNK)
    tok = pl.BlockSpec((1, tc, LANES), lambda b, h, i: (b, i, h))
    par = pl.BlockSpec((1, LANES), lambda b, h, i: (0, h))
    row = lambda arr: arr.reshape(1, width)
    return pl.pallas_call(
        functools.partial(_rwkv_scan_kernel, n_chunks=tc // RWKV_CHUNK, head_dim=head_dim),
        grid=(bsz, width // LANES, seq // tc),
        in_specs=[tok] * 7 + [par] * 3,
        out_specs=tok,
        out_shape=jax.ShapeDtypeStruct((bsz, seq, width), BF16),
        scratch_shapes=[pltpu.VMEM((LANES, LANES), F32)],
        compiler_params=_params("parallel", "parallel", "arbitrary"),
        name="rwkv_scan",
    )(r, k, v, lw, kk, a, g, row(r_k), row(gn_g), row(gn_b))


def _logf_cumsum_kernel(f_ref, b_ref, o_ref, carry):
    @pl.when(pl.program_id(1) == 0)
    def _():
        carry[...] = jnp.zeros_like(carry)

    z = f_ref[0] + b_ref[...]
    lf = jnp.minimum(z, 0.0) - jnp.log(1.0 + jnp.exp(-jnp.abs(z)))
    ts = z.shape[1]
    ri = lax.broadcasted_iota(jnp.int32, (LANES, LANES), 0)
    ci = lax.broadcasted_iota(jnp.int32, (LANES, LANES), 1)
    upper = (ri <= ci).astype(F32)
    run = carry[...]
    for j in range(ts // LANES):
        blk = jnp.dot(lf[:, j * LANES:(j + 1) * LANES], upper, preferred_element_type=F32,
                      precision=HIGHEST) + run
        o_ref[0, :, j * LANES:(j + 1) * LANES] = blk
        run = blk[:, LANES - 1:LANES]
    carry[...] = run


def _logf_cumsum(f_t, b_f):
    bsz, nh, seq = f_t.shape
    ts = _tile(seq, 2048, LANES)
    return pl.pallas_call(
        _logf_cumsum_kernel,
        grid=(bsz, seq // ts),
        in_specs=[pl.BlockSpec((1, nh, ts), lambda b, i: (b, 0, i)),
                  pl.BlockSpec((nh, 1), lambda b, i: (0, 0))],
        out_specs=pl.BlockSpec((1, nh, ts), lambda b, i: (b, 0, i)),
        out_shape=jax.ShapeDtypeStruct((bsz, nh, seq), F32),
        scratch_shapes=[pltpu.VMEM((nh, 1), F32)],
        compiler_params=_params("parallel", "arbitrary"),
        name="fox_logf_cumsum",
    )(f_t, b_f.reshape(nh, 1))


def _fox_kernel(qi_ref, ki_ref, q_ref, k_ref, v_ref, ck_ref, og_ref, o_ref,
                m_scr, l_scr, acc_scr, *, rows):
    p = pl.program_id(2)
    qi = qi_ref[p]
    ki = ki_ref[p]
    tq = q_ref.shape[1]
    tk = k_ref.shape[1]

    @pl.when(ki == 0)
    def _():
        m_scr[...] = jnp.full_like(m_scr, NEG_BIG)
        l_scr[...] = jnp.zeros_like(l_scr)
        acc_scr[...] = jnp.zeros_like(acc_scr)

    n_rc = tq // rows
    n_lt = tk // LANES

    n_kq = tq // tk

    def step(diag):
        k = k_ref[0]
        v = v_ref[0]
        ck = ck_ref[0, 0] * LOG2E
        col0 = 0 if diag is None else diag * tk
        live = [rc for rc in range(n_rc) if diag is None or col0 < (rc + 1) * rows]

        def logits(rc):
            s = lax.dot_general(q_ref[0, pl.ds(rc * rows, rows), :], k, (((1,), (1,)), ((), ())),
                                preferred_element_type=F32) - ck
            if diag is not None and col0 + tk - 1 > rc * rows:
                ri = lax.broadcasted_iota(jnp.int32, s.shape, 0) + rc * rows
                ci = lax.broadcasted_iota(jnp.int32, s.shape, 1) + col0
                s = jnp.where(ci <= ri, s, NEG_BIG)
            return s

        old = {rc: (m_scr[pl.ds(rc * rows, rows), :], l_scr[pl.ds(rc * rows, rows), :],
                    acc_scr[pl.ds(rc * rows, rows), :]) for rc in live}
        new = {}
        s_next = logits(live[0])
        for i, rc in enumerate(live):
            s = s_next
            if i + 1 < len(live):
                s_next = logits(live[i + 1])
            m_prev, l_prev, acc_prev = old[rc]
            tiles = [s[:, j * LANES:(j + 1) * LANES] for j in range(n_lt)]
            m_new = jnp.maximum(m_prev, jnp.max(functools.reduce(jnp.maximum, tiles),
                                                axis=-1, keepdims=True))
            alpha = jnp.exp2(m_prev - m_new)
            p_tiles = [jnp.exp2(t - m_new) for t in tiles]
            l_new = alpha * l_prev + functools.reduce(jnp.add, p_tiles)
            pr = jnp.concatenate([t.astype(BF16) for t in p_tiles], axis=1)
            acc_new = alpha * acc_prev + jnp.dot(pr, v, preferred_element_type=F32)
            new[rc] = (m_new, l_new, acc_new)
        for rc in live:
            rs = pl.ds(rc * rows, rows)
            m_scr[rs, :], l_scr[rs, :], acc_scr[rs, :] = new[rc]

    @pl.when(ki < qi * n_kq)
    def _():
        step(None)

    for diag in range(n_kq):
        @pl.when(ki == qi * n_kq + diag)
        def _(diag=diag):
            step(diag)

    @pl.when(ki == qi * n_kq + n_kq - 1)
    def _():
        o = acc_scr[...] / jnp.sum(l_scr[...], axis=-1, keepdims=True)
        o_ref[0] = (o * _sigmoid(og_ref[0])).astype(o_ref.dtype)


def _fox_attention(qk, v, cum, ogf, n_heads):
    bsz, seq, _ = v.shape
    tk = _tile(seq, FOX_TK, LANES)
    tq = _tile(seq, FOX_TQ, tk)
    qi_list, ki_list = [], []
    for qi in range(seq // tq):
        for ki in range((qi + 1) * (tq // tk)):
            qi_list.append(qi)
            ki_list.append(ki)
    qi_arr = jnp.asarray(qi_list, jnp.int32)
    ki_arr = jnp.asarray(ki_list, jnp.int32)
    cum_row = cum[:, :, None, :]
    nh = n_heads
    grid_spec = pltpu.PrefetchScalarGridSpec(
        num_scalar_prefetch=2,
        grid=(bsz, nh, len(qi_list)),
        in_specs=[pl.BlockSpec((1, tq, LANES), lambda b, h, p, qi, ki: (b, qi[p], h)),
                  pl.BlockSpec((1, tk, LANES), lambda b, h, p, qi, ki: (b, ki[p], nh + h)),
                  pl.BlockSpec((1, tk, LANES), lambda b, h, p, qi, ki: (b, ki[p], h)),
                  pl.BlockSpec((1, 1, 1, tk), lambda b, h, p, qi, ki: (b, h, 0, ki[p])),
                  pl.BlockSpec((1, tq, LANES), lambda b, h, p, qi, ki: (b, qi[p], h))],
        out_specs=pl.BlockSpec((1, tq, LANES), lambda b, h, p, qi, ki: (b, qi[p], h)),
        scratch_shapes=[pltpu.VMEM((tq, LANES), F32), pltpu.VMEM((tq, LANES), F32),
                        pltpu.VMEM((tq, LANES), F32)],
    )
    return pl.pallas_call(
        functools.partial(_fox_kernel, rows=_tile(tq, FOX_ROWS)),
        grid_spec=grid_spec,
        out_shape=jax.ShapeDtypeStruct((bsz, seq, nh * LANES), BF16),
        compiler_params=_params("parallel", "parallel", "arbitrary"),
        name="fox_attention",
    )(qi_arr, ki_arr, qk, qk, v, cum_row, ogf)


def _merge_kernel(ya_ref, yb_ref, ga_ref, gb_ref, wa_ref, wb_ref, o_ref):
    pa = jnp.dot(ya_ref[0], wa_ref[...], preferred_element_type=F32)
    pb = jnp.dot(yb_ref[0], wb_ref[...], preferred_element_type=F32)
    o_ref[0] = (ga_ref[0].astype(F32) * pa + gb_ref[0].astype(F32) * pb).astype(o_ref.dtype)


def _merge(y_a, y_b, gates, w_a, w_b):
    bsz, seq, wa = y_a.shape
    wb = y_b.shape[2]
    d = w_a.shape[1]
    tm = _tile(seq, 512)
    return pl.pallas_call(
        _merge_kernel,
        grid=(bsz, seq // tm),
        in_specs=[pl.BlockSpec((1, tm, wa), lambda b, i: (b, i, 0)),
                  pl.BlockSpec((1, tm, wb), lambda b, i: (b, i, 0)),
                  pl.BlockSpec((1, tm, d), lambda b, i: (b, i, 0)),
                  pl.BlockSpec((1, tm, d), lambda b, i: (b, i, 1)),
                  pl.BlockSpec((wa, d), lambda b, i: (0, 0)),
                  pl.BlockSpec((wb, d), lambda b, i: (0, 0))],
        out_specs=pl.BlockSpec((1, tm, d), lambda b, i: (b, i, 0)),
        out_shape=jax.ShapeDtypeStruct((bsz, seq, d), BF16),
        compiler_params=_params("parallel", "parallel"),
        name="branch_merge",
    )(y_a, y_b, gates, gates, w_a, w_b)


def _route_kernel(m_ref, x_ref, g1_ref, w_ref, n2_ref, sc_ref, sh_ref, wr_ref, br_ref,
                  x1_ref, route_ref, cnt_ref, base, *, n_experts):
    @pl.when(jnp.logical_and(pl.program_id(0) == 0, pl.program_id(1) == 0))
    def _():
        base[...] = jnp.zeros_like(base)

    x1 = x_ref[0] + g1_ref[0] * jnp.dot(m_ref[0], w_ref[...], preferred_element_type=F32)
    x1_ref[0] = x1
    h = _rms_mod(x1, n2_ref[...], sc_ref[0], sh_ref[0])
    logits = jnp.dot(h, wr_ref[...], preferred_element_type=F32, precision=HIGHEST) + br_ref[...]
    tm = logits.shape[0]
    lane_i = lax.broadcasted_iota(jnp.int32, (tm, LANES), 1)
    lane = lane_i.astype(F32)
    vals = jnp.where(lane_i < n_experts, logits, -jnp.inf)
    top_v, top_i, hot = [], [], []
    for _ in range(TOP_K):
        mx = jnp.max(vals, axis=-1, keepdims=True)
        ix = jnp.min(jnp.where(vals == mx, lane, float(LANES)), axis=-1, keepdims=True)
        sel = lane == ix
        vals = jnp.where(sel, -jnp.inf, vals)
        top_v.append(mx)
        top_i.append(ix)
        hot.append(sel.astype(F32))
    ex = [jnp.exp(tv - top_v[0]) for tv in top_v]
    den = ex[0] + ex[1] + ex[2] + ex[3]
    cnt = hot[0] + hot[1] + hot[2] + hot[3]
    ri = lax.broadcasted_iota(jnp.int32, (tm, tm), 0)
    ci = lax.broadcasted_iota(jnp.int32, (tm, tm), 1)
    before = jnp.dot((ci < ri).astype(BF16), cnt.astype(BF16), preferred_element_type=F32)
    before = before + base[...]
    out = jnp.zeros((tm, LANES), F32)
    for kk in range(TOP_K):
        rank = jnp.sum(hot[kk] * before, axis=-1, keepdims=True)
        out = jnp.where(lane_i == kk, top_i[kk], out)
        out = jnp.where(lane_i == TOP_K + kk, ex[kk] / den, out)
        out = jnp.where(lane_i == 2 * TOP_K + kk, rank, out)
    route_ref[0] = out
    new_base = base[...] + jnp.sum(cnt, axis=0, keepdims=True)
    base[...] = new_base
    cnt_ref[...] = jnp.broadcast_to(new_base, cnt_ref.shape)


def _route(merged, x, gate1, w_out, norm2_g, scale2, shift2, w_router, b_router):
    bsz, seq, d = x.shape
    n_experts = w_router.shape[1]
    tm = _tile(seq, 512)
    wr = jnp.zeros((d, LANES), F32).at[:, :n_experts].set(w_router)
    br = jnp.zeros((1, LANES), F32).at[0, :n_experts].set(b_router)
    mod = pl.BlockSpec((1, 1, d), lambda b, i: (b, 0, 0))
    tok = pl.BlockSpec((1, tm, d), lambda b, i: (b, i, 0))
    return pl.pallas_call(
        functools.partial(_route_kernel, n_experts=n_experts),
        grid=(bsz, seq // tm),
        in_specs=[tok, tok, mod,
                  pl.BlockSpec((d, d), lambda b, i: (0, 0)),
                  pl.BlockSpec((1, d), lambda b, i: (0, 0)), mod, mod,
                  pl.BlockSpec((d, LANES), lambda b, i: (0, 0)),
                  pl.BlockSpec((1, LANES), lambda b, i: (0, 0))],
        out_specs=[tok,
                   pl.BlockSpec((1, tm, LANES), lambda b, i: (b, i, 0)),
                   pl.BlockSpec((8, LANES), lambda b, i: (0, 0))],
        out_shape=[jax.ShapeDtypeStruct((bsz, seq, d), F32),
                   jax.ShapeDtypeStruct((bsz, seq, LANES), F32),
                   jax.ShapeDtypeStruct((8, LANES), F32)],
        scratch_shapes=[pltpu.VMEM((1, LANES), F32)],
        compiler_params=_params("arbitrary", "arbitrary"),
        name="residual_router",
    )(merged, x, gate1, w_out, norm2_g.reshape(1, d), scale2, shift2, wr, br)


def _dispatch_kernel(slot_ref, x_ref, n2_ref, sc_ref, sh_ref, xs_in_ref, xs_ref, hbuf, sem):
    del xs_in_ref
    i = pl.program_id(0)
    tm = hbuf.shape[0]
    hbuf[...] = _rms_mod(x_ref[...], n2_ref[...], sc_ref[...], sh_ref[...])

    def row_copy(r, s):
        return pltpu.make_async_copy(hbuf.at[pl.ds(r, 1)], xs_ref.at[pl.ds(s, 1)], sem)

    def issue(r, carry):
        for kk in range(TOP_K):
            row_copy(r, slot_ref[(i * tm + r) * TOP_K + kk]).start()
        return carry

    lax.fori_loop(0, tm, issue, 0)

    def drain(r, carry):
        for kk in range(TOP_K):
            row_copy(0, 0).wait()
        return carry

    lax.fori_loop(0, tm, drain, 0)


def _dispatch(x1_b, slots_b, norm2_g, scale2_b, shift2_b, xs):
    seq, d = x1_b.shape
    tm = _tile(seq, 256)
    grid_spec = pltpu.PrefetchScalarGridSpec(
        num_scalar_prefetch=1,
        grid=(seq // tm,),
        in_specs=[pl.BlockSpec((tm, d), lambda i, s: (i, 0)),
                  pl.BlockSpec((1, d), lambda i, s: (0, 0)),
                  pl.BlockSpec((1, d), lambda i, s: (0, 0)),
                  pl.BlockSpec((1, d), lambda i, s: (0, 0)),
                  pl.BlockSpec(memory_space=pl.ANY)],
        out_specs=pl.BlockSpec(memory_space=pl.ANY),
        scratch_shapes=[pltpu.VMEM((tm, d), F32), pltpu.SemaphoreType.DMA(())],
    )
    return pl.pallas_call(
        _dispatch_kernel,
        grid_spec=grid_spec,
        out_shape=jax.ShapeDtypeStruct(xs.shape, xs.dtype),
        input_output_aliases={5: 0},
        compiler_params=_params("arbitrary"),
        name="moe_dispatch",
    )(slots_b, x1_b, norm2_g.reshape(1, d), scale2_b, shift2_b, xs)


def _expert_kernel(be_ref, na_ref, x_ref, wg_ref, wl_ref, bg_ref, bl_ref, wd_ref, bd_ref,
                   o_ref, acc, *, n_f):
    b = pl.program_id(0)
    f = pl.program_id(1)

    @pl.when(b < na_ref[0])
    def _():
        x = x_ref[...].astype(BF16)
        gate = jnp.dot(x, wg_ref[0], preferred_element_type=F32) + bg_ref[0]
        lin = jnp.dot(x, wl_ref[0], preferred_element_type=F32) + bl_ref[0]
        gate = jnp.minimum(gate, SWIGLU_LIMIT)
        lin = jnp.clip(lin, -SWIGLU_LIMIT, SWIGLU_LIMIT)
        act = gate * _sigmoid(SWIGLU_ALPHA * gate) * (lin + 1.0)
        contrib = jnp.dot(act.astype(BF16), wd_ref[0], preferred_element_type=F32)

        @pl.when(f == 0)
        def _():
            acc[...] = contrib

        @pl.when(f > 0)
        def _():
            acc[...] += contrib

        @pl.when(f == n_f - 1)
        def _():
            o_ref[...] = acc[...] + bd_ref[0]

    @pl.when(jnp.logical_and(b >= na_ref[0], f == n_f - 1))
    def _():
        o_ref[...] = jnp.zeros_like(o_ref)


def _experts(xs, blk_e, n_active, w_gu, b_gu, w_dn, b_dn):
    n_slots, d = xs.shape
    n_e, _, two_ff = w_gu.shape
    d_ff = two_ff // 2
    bm = EXPERT_ROWS
    n_blocks = n_slots // bm
    tf = _tile(d_ff, 512, LANES)
    n_f = d_ff // tf

    def blk(b, na):
        return jnp.minimum(b, na[0] - 1)

    def ff(b, f, na):
        return jnp.where(b < na[0], f, n_f - 1)

    grid_spec = pltpu.PrefetchScalarGridSpec(
        num_scalar_prefetch=2,
        grid=(n_blocks, n_f),
        in_specs=[pl.BlockSpec((bm, d), lambda b, f, be, na: (blk(b, na), 0)),
                  pl.BlockSpec((1, d, tf), lambda b, f, be, na: (be[blk(b, na)], 0, ff(b, f, na))),
                  pl.BlockSpec((1, d, tf), lambda b, f, be, na: (be[blk(b, na)], 0, n_f + ff(b, f, na))),
                  pl.BlockSpec((1, 1, tf), lambda b, f, be, na: (be[blk(b, na)], 0, ff(b, f, na))),
                  pl.BlockSpec((1, 1, tf), lambda b, f, be, na: (be[blk(b, na)], 0, n_f + ff(b, f, na))),
                  pl.BlockSpec((1, tf, d), lambda b, f, be, na: (be[blk(b, na)], ff(b, f, na), 0)),
                  pl.BlockSpec((1, 1, d), lambda b, f, be, na: (be[blk(b, na)], 0, 0))],
        out_specs=pl.BlockSpec((bm, d), lambda b, f, be, na: (b, 0)),
        scratch_shapes=[pltpu.VMEM((bm, d), F32)],
    )
    return pl.pallas_call(
        functools.partial(_expert_kernel, n_f=n_f),
        grid_spec=grid_spec,
        out_shape=jax.ShapeDtypeStruct((n_slots, d), F32),
        compiler_params=_params("arbitrary", "arbitrary"),
        name="moe_experts",
    )(blk_e, n_active, xs, w_gu, w_gu, b_gu.reshape(n_e, 1, two_ff), b_gu.reshape(n_e, 1, two_ff),
      w_dn, b_dn.reshape(n_e, 1, d))


def _combine_kernel(slot_ref, x_ref, route_ref, g2_ref, gf_ref, ys_ref, o_ref, buf, sem):
    i = pl.program_id(0)
    tm = x_ref.shape[0]

    def row_copy(kk, r, s):
        return pltpu.make_async_copy(ys_ref.at[pl.ds(s, 1)], buf.at[kk, pl.ds(r, 1)], sem)

    def issue(r, carry):
        for kk in range(TOP_K):
            row_copy(kk, r, slot_ref[(i * tm + r) * TOP_K + kk]).start()
        return carry

    lax.fori_loop(0, tm, issue, 0)

    def drain(r, carry):
        for kk in range(TOP_K):
            row_copy(0, 0, 0).wait()
        return carry

    lax.fori_loop(0, tm, drain, 0)

    route = route_ref[...]
    y = jnp.zeros(x_ref.shape, F32)
    for kk in range(TOP_K):
        y = y + buf[kk] * route[:, TOP_K + kk:TOP_K + kk + 1]
    x2 = x_ref[...] + g2_ref[...] * y
    ms = jnp.mean(x2 * x2, axis=-1, keepdims=True)
    o_ref[...] = x2 * lax.rsqrt(ms + NORM_EPS) * gf_ref[...]


def _combine(x1_b, route_b, slots_b, gate2_b, norm_final_g, ys):
    seq, d = x1_b.shape
    tm = _tile(seq, 256)
    grid_spec = pltpu.PrefetchScalarGridSpec(
        num_scalar_prefetch=1,
        grid=(seq // tm,),
        in_specs=[pl.BlockSpec((tm, d), lambda i, s: (i, 0)),
                  pl.BlockSpec((tm, LANES), lambda i, s: (i, 0)),
                  pl.BlockSpec((1, d), lambda i, s: (0, 0)),
                  pl.BlockSpec((1, d), lambda i, s: (0, 0)),
                  pl.BlockSpec(memory_space=pl.ANY)],
        out_specs=pl.BlockSpec((tm, d), lambda i, s: (i, 0)),
        scratch_shapes=[pltpu.VMEM((TOP_K, tm, d), F32), pltpu.SemaphoreType.DMA(())],
    )
    return pl.pallas_call(
        _combine_kernel,
        grid_spec=grid_spec,
        out_shape=jax.ShapeDtypeStruct((seq, d), F32),
        compiler_params=_params("arbitrary"),
        name="moe_combine",
    )(slots_b, x1_b, route_b, gate2_b, norm_final_g.reshape(1, d), ys)


def _pad_cols(w, n):
    return jnp.pad(w, ((0, 0), (0, n - w.shape[1])))


def kernel(x, c, w_ada, b_ada, norm1_g, w_in, rwkv_mu, w_decay_up, decay_w0, w_iclr_up, iclr_a0, w_gate_up_rwkv, rwkv_k_k, rwkv_k_a, rwkv_r_k, rwkv_gn_g, rwkv_gn_b, w_out_a, fox_b_f, fox_q_norm, fox_k_norm, w_out_b, w_out, norm2_g, w_router, b_router, w_expert_gu, b_expert_gu, w_expert_down, b_expert_down, norm_final_g):
    bsz, seq, d = x.shape
    rw_heads, rw_hd = rwkv_r_k.shape
    rw = rw_heads * rw_hd
    dr, ir, gr = w_decay_up.shape[0], w_iclr_up.shape[0], w_gate_up_rwkv.shape[0]
    fh = fox_b_f.shape[0]
    fw = w_out_b.shape[0]
    fhd = fw // fh
    assert fhd == LANES and dr <= LANES and ir <= LANES and gr % LANES == 0
    n_experts = w_router.shape[1]

    mod = _ada(c, w_ada, b_ada)
    shift1, scale1, gate1, shift2, scale2, gate2 = (m[:, None, :] for m in jnp.split(mod, 6, axis=-1))

    o_r = 0
    o_f = 3 * rw + dr + ir + gr
    o_g = o_f + 3 * fw + fh + fw
    col = lambda a, n: w_in[:, a:a + n]
    w_rwkv = jnp.concatenate([col(0, 3 * rw), _pad_cols(col(3 * rw, dr), LANES),
                              _pad_cols(col(3 * rw + dr, ir), LANES), col(3 * rw + dr + ir, gr)],
                             axis=1).astype(BF16)
    mu = jnp.concatenate([rwkv_mu[:3 * rw], jnp.pad(rwkv_mu[3 * rw:3 * rw + dr], (0, LANES - dr)),
                          jnp.pad(rwkv_mu[3 * rw + dr:3 * rw + dr + ir], (0, LANES - ir)),
                          rwkv_mu[3 * rw + dr + ir:]])
    w_qk = col(o_f, 2 * fw).astype(BF16)
    w_v = col(o_f + 2 * fw, fw).astype(BF16)
    w_ogf = jnp.concatenate([col(o_f + 3 * fw + fh, fw), _pad_cols(col(o_f + 3 * fw, fh), LANES)],
                            axis=1).astype(BF16)
    w_gates = col(o_g, 2 * d).astype(BF16)
    qk_gain = jnp.concatenate([jnp.tile(fox_q_norm * (fhd ** -0.5 * LOG2E), fh), jnp.tile(fox_k_norm, fh)])

    def proj(w, extra, dtype, ep, name):
        if extra is None:
            extra = jnp.zeros((w.shape[1],), F32)
        return _inproj(x, norm1_g, scale1, shift1, w, extra.reshape(1, -1), dtype, ep, name)

    p_rwkv = proj(w_rwkv, None, F32, _ep_identity, "inproj_rwkv")
    qk = proj(w_qk, qk_gain, BF16, _ep_headnorm, "inproj_fox_qk")
    v_fox = proj(w_v, None, BF16, _ep_identity, "inproj_fox_v")
    ogf = proj(w_ogf, None, F32, _ep_identity, "inproj_fox_gate_forget")
    gates = proj(w_gates, None, BF16, _ep_sigmoid, "inproj_merge_gates")

    wd = jnp.pad(w_decay_up, ((0, LANES - dr), (0, 0)))
    wi = jnp.pad(w_iclr_up, ((0, LANES - ir), (0, 0)))
    r, k2, v, lw, kk, aic, g = _rwkv_prep(p_rwkv, mu, wd, decay_w0, wi, iclr_a0,
                                          w_gate_up_rwkv.astype(BF16), rwkv_k_k, rwkv_k_a, rw, gr)
    y_a = _rwkv_scan(r, k2, v, lw, kk, aic, g, rwkv_r_k, rwkv_gn_g, rwkv_gn_b, rw_hd)

    f_t = jnp.transpose(ogf[:, :, fw:fw + fh], (0, 2, 1))
    cum = _logf_cumsum(f_t, fox_b_f)
    y_b = _fox_attention(qk, v_fox, cum, ogf, fh)

    merged = _merge(y_a, y_b, gates, w_out_a.astype(BF16), w_out_b.astype(BF16))
    x1, route, counts = _route(merged, x, gate1, w_out.astype(BF16), norm2_g, scale2, shift2,
                               w_router, b_router)

    n_tok = bsz * seq
    n_assign = n_tok * TOP_K
    bm = EXPERT_ROWS
    n_blocks = -(-n_assign // bm) + n_experts
    cnt = counts[0, :n_experts].astype(jnp.int32)
    padded = (cnt + bm - 1) // bm * bm
    pad_end = jnp.cumsum(padded)
    pad_start = pad_end - padded
    top_i = route[:, :, 0:TOP_K].astype(jnp.int32)
    rank = route[:, :, 2 * TOP_K:3 * TOP_K].astype(jnp.int32)
    e_ids = jnp.arange(n_experts, dtype=jnp.int32)
    start_of = jnp.sum(jnp.where(top_i[..., None] == e_ids, pad_start, 0), axis=-1)
    slots = (start_of + rank).reshape(bsz, seq * TOP_K)
    blk_first = jnp.arange(n_blocks, dtype=jnp.int32) * bm
    blk_e = jnp.minimum(jnp.sum((pad_end[None, :] <= blk_first[:, None]).astype(jnp.int32), axis=1),
                        n_experts - 1)
    n_active = (pad_end[-1:] // bm).astype(jnp.int32)

    xs = jnp.zeros((n_blocks * bm, d), F32)
    for b in range(bsz):
        xs = _dispatch(x1[b], slots[b], norm2_g, scale2[b], shift2[b], xs)
    ys = _experts(xs, blk_e, n_active, w_expert_gu.astype(BF16), b_expert_gu,
                  w_expert_down.astype(BF16), b_expert_down)
    outs = [_combine(x1[b], route[b], slots[b], gate2[b], norm_final_g, ys) for b in range(bsz)]
    return jnp.stack(outs, axis=0)
```

```python
import functools

import jax
import jax.numpy as jnp
from jax import lax
from jax.experimental import pallas as pl
from jax.experimental.pallas import tpu as pltpu

F32 = jnp.float32
BF16 = jnp.bfloat16
HIGHEST = lax.Precision.HIGHEST

TOP_K = 4
NORM_EPS = 1e-6
GN_EPS = 64e-5
SWIGLU_LIMIT = 7.0
SWIGLU_ALPHA = 1.702
LANES = 128
RWKV_CHUNK = 64
VMEM_LIMIT_BYTES = 56 * 1024 * 1024
EXPERT_ROWS = 512
EXPERT_FF_TILE = 1024
NEG_BIG = -1e30
LOG2E = 1.4426950408889634
FOX_ROWS = 256
FOX_TQ = 1024
FOX_TK = 1024


def _tile(n, pref, mult=8):
    t = min(pref, n)
    t -= t % mult
    while t >= mult:
        if n % t == 0:
            return t
        t -= mult
    return n


def _params(*sem):
    return pltpu.CompilerParams(dimension_semantics=sem, vmem_limit_bytes=VMEM_LIMIT_BYTES)


def _sigmoid(x):
    return 1.0 / (1.0 + jnp.exp(-x))


def _rms_mod(x, g, scale, shift):
    ms = jnp.mean(x * x, axis=-1, keepdims=True)
    return x * lax.rsqrt(ms + NORM_EPS) * g * (1.0 + scale) + shift


def _ada_kernel(c_ref, w_ref, b_ref, o_ref):
    c = c_ref[...]
    s = c * _sigmoid(c)
    o_ref[...] = jnp.dot(s, w_ref[...], preferred_element_type=F32, precision=HIGHEST) + b_ref[...]


def _ada(c, w_ada, b_ada):
    bsz, d = c.shape
    n = w_ada.shape[1]
    rows = 8
    c_pad = jnp.zeros((rows, d), F32).at[:bsz].set(c)
    tn = _tile(n, 1024, LANES)
    out = pl.pallas_call(
        _ada_kernel,
        grid=(n // tn,),
        in_specs=[pl.BlockSpec((rows, d), lambda j: (0, 0)),
                  pl.BlockSpec((d, tn), lambda j: (0, j)),
                  pl.BlockSpec((1, tn), lambda j: (0, j))],
        out_specs=pl.BlockSpec((rows, tn), lambda j: (0, j)),
        out_shape=jax.ShapeDtypeStruct((rows, n), F32),
        compiler_params=_params("parallel"),
        name="adaln",
    )(c_pad, w_ada, b_ada.reshape(1, n))
    return out[:bsz]


def _inproj_kernel(x_ref, g_ref, sc_ref, sh_ref, w_ref, e_ref, o_ref, h_ref, *, epilogue):
    @pl.when(pl.program_id(2) == 0)
    def _():
        h = _rms_mod(x_ref[0], g_ref[...], sc_ref[0], sh_ref[0])
        h_ref[0] = h.astype(h_ref.dtype)

    acc = jnp.dot(h_ref[0], w_ref[...], preferred_element_type=F32)
    o_ref[0] = epilogue(acc, e_ref[...]).astype(o_ref.dtype)


def _proj_kernel(h_ref, w_ref, e_ref, o_ref, *, epilogue):
    acc = jnp.dot(h_ref[0], w_ref[...], preferred_element_type=F32)
    o_ref[0] = epilogue(acc, e_ref[...]).astype(o_ref.dtype)


def _ep_identity(acc, extra):
    return acc


def _ep_sigmoid(acc, extra):
    return _sigmoid(acc)


def _ep_headnorm(acc, extra):
    outs = []
    for h in range(acc.shape[1] // LANES):
        a = acc[:, h * LANES:(h + 1) * LANES]
        ms = jnp.mean(a * a, axis=-1, keepdims=True)
        outs.append(a * lax.rsqrt(ms + NORM_EPS))
    return jnp.concatenate(outs, axis=1) * extra


def _inproj(x, g, scale, shift, w, extra, out_dtype, epilogue, name):
    bsz, seq, d = x.shape
    n = w.shape[1]
    tm = _tile(seq, 1024)
    tn = _tile(n, 512, LANES)
    return pl.pallas_call(
        functools.partial(_inproj_kernel, epilogue=epilogue),
        grid=(bsz, seq // tm, n // tn),
        in_specs=[pl.BlockSpec((1, tm, d), lambda b, i, j: (b, i, 0)),
                  pl.BlockSpec((1, d), lambda b, i, j: (0, 0)),
                  pl.BlockSpec((1, 1, d), lambda b, i, j: (b, 0, 0)),
                  pl.BlockSpec((1, 1, d), lambda b, i, j: (b, 0, 0)),
                  pl.BlockSpec((d, tn), lambda b, i, j: (0, j)),
                  pl.BlockSpec((1, tn), lambda b, i, j: (0, j))],
        out_specs=[pl.BlockSpec((1, tm, tn), lambda b, i, j: (b, i, j)),
                   pl.BlockSpec((1, tm, d), lambda b, i, j: (b, i, 0))],
        out_shape=[jax.ShapeDtypeStruct((bsz, seq, n), out_dtype),
                   jax.ShapeDtypeStruct((bsz, seq, d), BF16)],
        compiler_params=_params("parallel", "parallel", "arbitrary"),
        name=name,
    )(x, g.reshape(1, d), scale, shift, w, extra)


def _proj(h, w, extra, out_dtype, epilogue, name):
    bsz, seq, d = h.shape
    n = w.shape[1]
    tm = _tile(seq, 1024)
    tn = _tile(n, 512, LANES)
    return pl.pallas_call(
        functools.partial(_proj_kernel, epilogue=epilogue),
        grid=(bsz, seq // tm, n // tn),
        in_specs=[pl.BlockSpec((1, tm, d), lambda b, i, j: (b, i, 0)),
                  pl.BlockSpec((d, tn), lambda b, i, j: (0, j)),
                  pl.BlockSpec((1, tn), lambda b, i, j: (0, j))],
        out_specs=pl.BlockSpec((1, tm, tn), lambda b, i, j: (b, i, j)),
        out_shape=jax.ShapeDtypeStruct((bsz, seq, n), out_dtype),
        compiler_params=_params("parallel", "parallel", "parallel"),
        name=name,
    )(h, w, extra)


def _rwkv_prep_kernel(p_ref, mu_ref, wd_ref, w0_ref, wi_ref, a0_ref, wg_ref, kk_ref, ka_ref,
                      r_o, k_o, v_o, lw_o, kk_o, a_o, g_o, carry, *, width, gate_rank):
    @pl.when(pl.program_id(1) == 0)
    def _():
        carry[...] = jnp.zeros_like(carry)

    p = p_ref[0]
    tt = p.shape[0]
    row = lax.broadcasted_iota(jnp.int32, p.shape, 0)
    prev = jnp.where(row == 0, carry[...], pltpu.roll(p, 1, axis=0))
    carry[...] = p[tt - 1:tt, :]
    pm = p + mu_ref[...] * (prev - p)
    w = width
    r = pm[:, 0:w]
    k = pm[:, w:2 * w]
    v = pm[:, 2 * w:3 * w]
    d_lo = pm[:, 3 * w:3 * w + LANES]
    a_lo = pm[:, 3 * w + LANES:3 * w + 2 * LANES]
    g_lo = pm[:, 3 * w + 2 * LANES:3 * w + 2 * LANES + gate_rank]
    w_pre = w0_ref[...] + jnp.dot(jnp.tanh(d_lo), wd_ref[...], preferred_element_type=F32,
                                  precision=HIGHEST)
    lw_o[0] = -jnp.exp(-0.5) * _sigmoid(w_pre)
    a = _sigmoid(a0_ref[...] + jnp.dot(a_lo, wi_ref[...], preferred_element_type=F32,
                                       precision=HIGHEST))
    g_o[0] = jnp.dot(_sigmoid(g_lo).astype(BF16), wg_ref[...], preferred_element_type=F32)
    r_o[0] = r
    v_o[0] = v
    a_o[0] = a
    kk_o[0] = k * kk_ref[...]
    k_o[0] = k * (1.0 + (a - 1.0) * ka_ref[...])


def _rwkv_prep(p, mu, wd, w0, wi, a0, wg, k_k, k_a, width, gate_rank):
    bsz, seq, n = p.shape
    tt = _tile(seq, 256)
    row = lambda arr: arr.reshape(1, -1)
    full = lambda shape: pl.BlockSpec(shape, lambda b, i: (0,) * len(shape))
    out_sds = jax.ShapeDtypeStruct((bsz, seq, width), F32)
    out_spec = pl.BlockSpec((1, tt, width), lambda b, i: (b, i, 0))
    return pl.pallas_call(
        functools.partial(_rwkv_prep_kernel, width=width, gate_rank=gate_rank),
        grid=(bsz, seq // tt),
        in_specs=[pl.BlockSpec((1, tt, n), lambda b, i: (b, i, 0)),
                  full((1, n)), full(wd.shape), full((1, width)), full(wi.shape), full((1, width)),
                  full(wg.shape), full((1, width)), full((1, width))],
        out_specs=[out_spec] * 7,
        out_shape=[out_sds] * 7,
        scratch_shapes=[pltpu.VMEM((1, n), F32)],
        compiler_params=_params("parallel", "arbitrary"),
        name="rwkv_prep",
    )(p, row(mu), wd, row(w0), wi, row(a0), wg, row(k_k), row(k_a))


def _rwkv_scan_kernel(r_ref, k_ref, v_ref, lw_ref, kk_ref, a_ref, g_ref, rk_ref, gg_ref, gb_ref,
                      o_ref, s_scr, *, n_chunks, head_dim):
    L = RWKV_CHUNK
    L2 = 2 * L

    @pl.when(pl.program_id(2) == 0)
    def _():
        s_scr[...] = jnp.zeros_like(s_scr)

    lane = lax.broadcasted_iota(jnp.int32, (1, LANES), 1)
    m0 = (lane < head_dim).astype(F32)
    m1 = 1.0 - m0
    r2 = lax.broadcasted_iota(jnp.int32, (L2, L2), 0)
    c2 = lax.broadcasted_iota(jnp.int32, (L2, L2), 1)
    same = (r2 < L) == (c2 < L)
    strict = jnp.logical_and(same, c2 < r2)
    incl = jnp.logical_and(same, c2 <= r2)
    eye = (r2 == c2).astype(F32)
    rowi = lax.broadcasted_iota(jnp.int32, (L, LANES), 0)
    rk = rk_ref[...]
    gn_g = gg_ref[...]
    gn_b = gb_ref[...]

    def stack_f32(x):
        return jnp.concatenate([x * m0, x * m1], axis=0)

    def stack(x):
        return stack_f32(x).astype(BF16)

    def head_sum(x):
        s0 = jnp.sum(x * m0, axis=-1, keepdims=True)
        s1 = jnp.sum(x * m1, axis=-1, keepdims=True)
        return s0 * m0 + s1 * m1

    def nt(a, b):
        return lax.dot_general(a, b, (((1,), (1,)), ((), ())), preferred_element_type=F32)

    def mm(a, b):
        return jnp.dot(a.astype(BF16), b.astype(BF16), preferred_element_type=F32)

    blk_masks = []
    size = 8
    while size <= L:
        blk_masks.append(jnp.bitwise_xor(r2, c2) < size)
        size *= 2
    off_masks = [jnp.logical_and(hi, jnp.logical_not(lo))
                 for lo, hi in zip(blk_masks[:-1], blk_masks[1:])]
    chunks = range(n_chunks)

    ctx = []
    for c in chunks:
        sl = pl.ds(c * L, L)
        r = r_ref[0, sl, :]
        k = k_ref[0, sl, :]
        v = v_ref[0, sl, :]
        lw = lw_ref[0, sl, :]
        kk = kk_ref[0, sl, :]
        aic = a_ref[0, sl, :]
        cum = lw
        sh = 1
        while sh < L:
            cum = cum + jnp.where(rowi >= sh, pltpu.roll(cum, sh, axis=0), 0.0)
            sh *= 2
        tot = cum[L - 1:L, :]
        w_rem = jnp.exp(tot - cum)
        w_inv = jnp.exp(-cum)
        kk = kk / jnp.maximum(jnp.sqrt(head_sum(kk * kk)), 1e-12)
        b_vec = kk * aic
        ar = jnp.concatenate([stack(-kk * jnp.exp(cum - lw)), stack(r * jnp.exp(cum))], axis=0)
        bk = jnp.concatenate([stack(b_vec * w_inv), stack(k * w_inv)], axis=0)
        v_st = stack(v)
        b_rem_t = stack_f32(b_vec * w_rem).T.astype(BF16)
        k_rem_t = stack_f32(k * w_rem).T.astype(BF16)
        decay = jnp.broadcast_to(jnp.exp(tot), (LANES, LANES)).T
        ctx.append(dict(sl=sl, ar=ar, bk=bk, v_st=v_st, decay=decay, b_rem_t=b_rem_t,
                        k_rem_t=k_rem_t, bonus=head_sum(r * k * rk) * v))

    for x in ctx:
        big = nt(x["ar"], x["bk"])
        x["n_ab"] = jnp.where(strict, big[0:L2, 0:L2], 0.0)
        m_ak = jnp.where(strict, big[0:L2, L2:2 * L2], 0.0)
        m_rk = jnp.where(incl, big[L2:2 * L2, L2:2 * L2], 0.0)
        x["m_rb"] = jnp.where(incl, big[L2:2 * L2, 0:L2], 0.0).astype(BF16)
        x["m_akrk"] = jnp.concatenate([m_ak, m_rk], axis=0).astype(BF16)
    for x in ctx:
        x["m_v"] = mm(x["m_akrk"], x["v_st"])
        x["kv"] = mm(x["k_rem_t"], x["v_st"])

    for x in ctx:
        x["d8"] = jnp.where(blk_masks[0], x["n_ab"], 0.0)
        x["t"] = eye + x["d8"]
    for x in ctx:
        x["pw"] = mm(x["d8"], x["d8"])
    for x in ctx:
        x["t"] = x["t"] + mm(x["t"], x["pw"])
    for x in ctx:
        x["pw"] = mm(x["pw"], x["pw"])
    for x in ctx:
        x["t"] = x["t"] + mm(x["t"], x["pw"])
    for off in off_masks:
        for x in ctx:
            x["nt"] = mm(jnp.where(off, x["n_ab"], 0.0), x["t"])
        for x in ctx:
            x["t"] = x["t"] + mm(x["t"], x["nt"])

    for x in ctx:
        rhs = jnp.concatenate([x["ar"][0:L2], x["m_v"][0:L2].astype(BF16)], axis=1)
        x["ta_uv"] = mm(x["t"], rhs).astype(BF16)
    for x in ctx:
        w_q = mm(x["b_rem_t"], x["ta_uv"])
        g_y = mm(x["m_rb"], x["ta_uv"])
        x["q"] = w_q[:, LANES:] + x["kv"]
        x["y_loc"] = g_y[:, LANES:] + x["m_v"][L2:2 * L2]
        g = g_y[:, :LANES] + x["ar"][L2:2 * L2].astype(F32)
        x["wg"] = jnp.concatenate([w_q[:, :LANES], g], axis=0).astype(BF16)

    s = s_scr[...]
    for x in ctx:
        ws = mm(x["wg"], s)
        x["y_st"] = ws[L2:2 * L2] + x["y_loc"]
        s = x["decay"] * s + ws[0:L2] + x["q"]
    s_scr[...] = s

    inv_n = 1.0 / head_dim
    for x in ctx:
        y_st = x["y_st"]
        y = y_st[0:L] + y_st[L:L2]
        mean = head_sum(y) * inv_n
        yc = y - mean
        var = head_sum(yc * yc) * inv_n
        y_gn = yc * lax.rsqrt(var + GN_EPS) * gn_g + gn_b
        o_ref[0, x["sl"], :] = ((y_gn + x["bonus"]) * g_ref[0, x["sl"], :]).astype(o_ref.dtype)


def _rwkv_scan(r, k, v, lw, kk, a, g, r_k, gn_g, gn_b, head_dim):
    bsz, seq, width = r.shape
    assert 2 * head_dim == LANES and width % LANES == 0
    tc = _tile(seq, 8 * RWKV_CHUNK, RWKV_CHUNK)
    tok = pl.BlockSpec((1, tc, LANES), lambda b, h, i: (b, i, h))
    par = pl.BlockSpec((1, LANES), lambda b, h, i: (0, h))
    row = lambda arr: arr.reshape(1, width)
    return pl.pallas_call(
        functools.partial(_rwkv_scan_kernel, n_chunks=tc // RWKV_CHUNK, head_dim=head_dim),
        grid=(bsz, width // LANES, seq // tc),
        in_specs=[tok] * 7 + [par] * 3,
        out_specs=tok,
        out_shape=jax.ShapeDtypeStruct((bsz, seq, width), BF16),
        scratch_shapes=[pltpu.VMEM((LANES, LANES), F32)],
        compiler_params=_params("parallel", "parallel", "arbitrary"),
        name="rwkv_scan",
    )(r, k, v, lw, kk, a, g, row(r_k), row(gn_g), row(gn_b))


def _logf_cumsum_kernel(f_ref, b_ref, o_ref, carry):
    @pl.when(pl.program_id(1) == 0)
    def _():
        carry[...] = jnp.zeros_like(carry)

    z = f_ref[0] + b_ref[...]
    lf = jnp.minimum(z, 0.0) - jnp.log(1.0 + jnp.exp(-jnp.abs(z)))
    ts = z.shape[1]
    ri = lax.broadcasted_iota(jnp.int32, (LANES, LANES), 0)
    ci = lax.broadcasted_iota(jnp.int32, (LANES, LANES), 1)
    upper = (ri <= ci).astype(F32)
    run = carry[...]
    for j in range(ts // LANES):
        blk = jnp.dot(lf[:, j * LANES:(j + 1) * LANES], upper, preferred_element_type=F32,
                      precision=HIGHEST) + run
        o_ref[0, :, j * LANES:(j + 1) * LANES] = blk
        run = blk[:, LANES - 1:LANES]
    carry[...] = run


def _logf_cumsum(f_t, b_f):
    bsz, nh, seq = f_t.shape
    ts = _tile(seq, 2048, LANES)
    return pl.pallas_call(
        _logf_cumsum_kernel,
        grid=(bsz, seq // ts),
        in_specs=[pl.BlockSpec((1, nh, ts), lambda b, i: (b, 0, i)),
                  pl.BlockSpec((nh, 1), lambda b, i: (0, 0))],
        out_specs=pl.BlockSpec((1, nh, ts), lambda b, i: (b, 0, i)),
        out_shape=jax.ShapeDtypeStruct((bsz, nh, seq), F32),
        scratch_shapes=[pltpu.VMEM((nh, 1), F32)],
        compiler_params=_params("parallel", "arbitrary"),
        name="fox_logf_cumsum",
    )(f_t, b_f.reshape(nh, 1))


def _fox_kernel(qi_ref, ki_ref, q_ref, k_ref, v_ref, ck_ref, og_ref, o_ref,
                m_scr, l_scr, acc_scr, *, rows):
    p = pl.program_id(2)
    qi = qi_ref[p]
    ki = ki_ref[p]
    tq = q_ref.shape[1]
    tk = k_ref.shape[1]

    @pl.when(ki == 0)
    def _():
        m_scr[...] = jnp.full_like(m_scr, NEG_BIG)
        l_scr[...] = jnp.zeros_like(l_scr)
        acc_scr[...] = jnp.zeros_like(acc_scr)

    n_rc = tq // rows

    n_kq = tq // tk

    def step(diag):
        k = k_ref[0]
        v = v_ref[0]
        ck = ck_ref[0, 0] * LOG2E
        col0 = 0 if diag is None else diag * tk
        live = [rc for rc in range(n_rc) if diag is None or col0 < (rc + 1) * rows]

        def n_cols(rc):
            return tk if diag is None else min(tk, (rc + 1) * rows - col0)

        def logits(rc):
            kc = n_cols(rc)
            s = lax.dot_general(q_ref[0, pl.ds(rc * rows, rows), :], k[:kc], (((1,), (1,)), ((), ())),
                                preferred_element_type=F32) - ck[:, :kc]
            if diag is not None and col0 + kc - 1 > rc * rows:
                ri = lax.broadcasted_iota(jnp.int32, s.shape, 0) + rc * rows
                ci = lax.broadcasted_iota(jnp.int32, s.shape, 1) + col0
                s = jnp.where(ci <= ri, s, NEG_BIG)
            return s

        old = {rc: (m_scr[pl.ds(rc * rows, rows), :], l_scr[pl.ds(rc * rows, rows), :],
                    acc_scr[pl.ds(rc * rows, rows), :]) for rc in live}
        new = {}
        s_next = logits(live[0])
        for i, rc in enumerate(live):
            s = s_next
            if i + 1 < len(live):
                s_next = logits(live[i + 1])
            m_prev, l_prev, acc_prev = old[rc]
            tiles = [s[:, j * LANES:(j + 1) * LANES] for j in range(n_cols(rc) // LANES)]
            m_new = jnp.maximum(m_prev, jnp.max(functools.reduce(jnp.maximum, tiles),
                                                axis=-1, keepdims=True))
            alpha = jnp.exp2(m_prev - m_new)
            p_tiles = [jnp.exp2(t - m_new) for t in tiles]
            l_new = alpha * l_prev + functools.reduce(jnp.add, p_tiles)
            pr = jnp.concatenate([t.astype(BF16) for t in p_tiles], axis=1)
            acc_new = alpha * acc_prev + jnp.dot(pr, v[:n_cols(rc)], preferred_element_type=F32)
            new[rc] = (m_new, l_new, acc_new)
        for rc in live:
            rs = pl.ds(rc * rows, rows)
            m_scr[rs, :], l_scr[rs, :], acc_scr[rs, :] = new[rc]

    @pl.when(ki < qi * n_kq)
    def _():
        step(None)

    for diag in range(n_kq):
        @pl.when(ki == qi * n_kq + diag)
        def _(diag=diag):
            step(diag)

    @pl.when(ki == qi * n_kq + n_kq - 1)
    def _():
        o = acc_scr[...] / jnp.sum(l_scr[...], axis=-1, keepdims=True)
        o_ref[0] = (o * _sigmoid(og_ref[0])).astype(o_ref.dtype)


def _fox_attention(qk, v, cum, ogf, n_heads):
    bsz, seq, _ = v.shape
    tk = _tile(seq, FOX_TK, LANES)
    tq = _tile(seq, FOX_TQ, tk)
    qi_list, ki_list = [], []
    for qi in range(seq // tq):
        for ki in range((qi + 1) * (tq // tk)):
            qi_list.append(qi)
            ki_list.append(ki)
    qi_arr = jnp.asarray(qi_list, jnp.int32)
    ki_arr = jnp.asarray(ki_list, jnp.int32)
    cum_row = cum[:, :, None, :]
    nh = n_heads
    grid_spec = pltpu.PrefetchScalarGridSpec(
        num_scalar_prefetch=2,
        grid=(bsz, nh, len(qi_list)),
        in_specs=[pl.BlockSpec((1, tq, LANES), lambda b, h, p, qi, ki: (b, qi[p], h)),
                  pl.BlockSpec((1, tk, LANES), lambda b, h, p, qi, ki: (b, ki[p], nh + h)),
                  pl.BlockSpec((1, tk, LANES), lambda b, h, p, qi, ki: (b, ki[p], h)),
                  pl.BlockSpec((1, 1, 1, tk), lambda b, h, p, qi, ki: (b, h, 0, ki[p])),
                  pl.BlockSpec((1, tq, LANES), lambda b, h, p, qi, ki: (b, qi[p], h))],
        out_specs=pl.BlockSpec((1, tq, LANES), lambda b, h, p, qi, ki: (b, qi[p], h)),
        scratch_shapes=[pltpu.VMEM((tq, LANES), F32), pltpu.VMEM((tq, LANES), F32),
                        pltpu.VMEM((tq, LANES), F32)],
    )
    return pl.pallas_call(
        functools.partial(_fox_kernel, rows=_tile(tq, FOX_ROWS, LANES)),
        grid_spec=grid_spec,
        out_shape=jax.ShapeDtypeStruct((bsz, seq, nh * LANES), BF16),
        compiler_params=_params("parallel", "parallel", "arbitrary"),
        name="fox_attention",
    )(qi_arr, ki_arr, qk, qk, v, cum_row, ogf)


def _merge_kernel(ya_ref, yb_ref, ga_ref, gb_ref, wa_ref, wb_ref, o_ref):
    pa = jnp.dot(ya_ref[0], wa_ref[...], preferred_element_type=F32)
    pb = jnp.dot(yb_ref[0], wb_ref[...], preferred_element_type=F32)
    o_ref[0] = (ga_ref[0].astype(F32) * pa + gb_ref[0].astype(F32) * pb).astype(o_ref.dtype)


def _merge(y_a, y_b, gates, w_a, w_b):
    bsz, seq, wa = y_a.shape
    wb = y_b.shape[2]
    d = w_a.shape[1]
    tm = _tile(seq, 512)
    return pl.pallas_call(
        _merge_kernel,
        grid=(bsz, seq // tm),
        in_specs=[pl.BlockSpec((1, tm, wa), lambda b, i: (b, i, 0)),
                  pl.BlockSpec((1, tm, wb), lambda b, i: (b, i, 0)),
                  pl.BlockSpec((1, tm, d), lambda b, i: (b, i, 0)),
                  pl.BlockSpec((1, tm, d), lambda b, i: (b, i, 1)),
                  pl.BlockSpec((wa, d), lambda b, i: (0, 0)),
                  pl.BlockSpec((wb, d), lambda b, i: (0, 0))],
        out_specs=pl.BlockSpec((1, tm, d), lambda b, i: (b, i, 0)),
        out_shape=jax.ShapeDtypeStruct((bsz, seq, d), BF16),
        compiler_params=_params("parallel", "parallel"),
        name="branch_merge",
    )(y_a, y_b, gates, gates, w_a, w_b)


def _route_kernel(m_ref, x_ref, g1_ref, w_ref, n2_ref, sc_ref, sh_ref, wr_ref, br_ref,
                  x1_ref, route_ref, cnt_ref, base, *, n_experts):
    @pl.when(jnp.logical_and(pl.program_id(0) == 0, pl.program_id(1) == 0))
    def _():
        base[...] = jnp.zeros_like(base)

    x1 = x_ref[0] + g1_ref[0] * jnp.dot(m_ref[0], w_ref[...], preferred_element_type=F32)
    x1_ref[0] = x1
    h = _rms_mod(x1, n2_ref[...], sc_ref[0], sh_ref[0])
    logits = jnp.dot(h, wr_ref[...], preferred_element_type=F32, precision=HIGHEST) + br_ref[...]
    tm = logits.shape[0]
    lane_i = lax.broadcasted_iota(jnp.int32, (tm, LANES), 1)
    lane = lane_i.astype(F32)
    vals = jnp.where(lane_i < n_experts, logits, -jnp.inf)
    top_v, top_i, hot = [], [], []
    for _ in range(TOP_K):
        mx = jnp.max(vals, axis=-1, keepdims=True)
        ix = jnp.min(jnp.where(vals == mx, lane, float(LANES)), axis=-1, keepdims=True)
        sel = lane == ix
        vals = jnp.where(sel, -jnp.inf, vals)
        top_v.append(mx)
        top_i.append(ix)
        hot.append(sel.astype(F32))
    ex = [jnp.exp(tv - top_v[0]) for tv in top_v]
    den = ex[0] + ex[1] + ex[2] + ex[3]
    cnt = hot[0] + hot[1] + hot[2] + hot[3]
    ri = lax.broadcasted_iota(jnp.int32, (tm, tm), 0)
    ci = lax.broadcasted_iota(jnp.int32, (tm, tm), 1)
    before = jnp.dot((ci < ri).astype(BF16), cnt.astype(BF16), preferred_element_type=F32)
    before = before + base[...]
    out = jnp.zeros((tm, LANES), F32)
    for kk in range(TOP_K):
        rank = jnp.sum(hot[kk] * before, axis=-1, keepdims=True)
        out = jnp.where(lane_i == kk, top_i[kk], out)
        out = jnp.where(lane_i == TOP_K + kk, ex[kk] / den, out)
        out = jnp.where(lane_i == 2 * TOP_K + kk, rank, out)
    route_ref[0] = out
    new_base = base[...] + jnp.sum(cnt, axis=0, keepdims=True)
    base[...] = new_base
    cnt_ref[...] = jnp.broadcast_to(new_base, cnt_ref.shape)


def _route(merged, x, gate1, w_out, norm2_g, scale2, shift2, w_router, b_router):
    bsz, seq, d = x.shape
    n_experts = w_router.shape[1]
    tm = _tile(seq, 512)
    wr = jnp.zeros((d, LANES), F32).at[:, :n_experts].set(w_router)
    br = jnp.zeros((1, LANES), F32).at[0, :n_experts].set(b_router)
    mod = pl.BlockSpec((1, 1, d), lambda b, i: (b, 0, 0))
    tok = pl.BlockSpec((1, tm, d), lambda b, i: (b, i, 0))
    return pl.pallas_call(
        functools.partial(_route_kernel, n_experts=n_experts),
        grid=(bsz, seq // tm),
        in_specs=[tok, tok, mod,
                  pl.BlockSpec((d, d), lambda b, i: (0, 0)),
                  pl.BlockSpec((1, d), lambda b, i: (0, 0)), mod, mod,
                  pl.BlockSpec((d, LANES), lambda b, i: (0, 0)),
                  pl.BlockSpec((1, LANES), lambda b, i: (0, 0))],
        out_specs=[tok,
                   pl.BlockSpec((1, tm, LANES), lambda b, i: (b, i, 0)),
                   pl.BlockSpec((8, LANES), lambda b, i: (0, 0))],
        out_shape=[jax.ShapeDtypeStruct((bsz, seq, d), F32),
                   jax.ShapeDtypeStruct((bsz, seq, LANES), F32),
                   jax.ShapeDtypeStruct((8, LANES), F32)],
        scratch_shapes=[pltpu.VMEM((1, LANES), F32)],
        compiler_params=_params("arbitrary", "arbitrary"),
        name="residual_router",
    )(merged, x, gate1, w_out, norm2_g.reshape(1, d), scale2, shift2, wr, br)


def _dispatch_kernel(slot_ref, x_ref, n2_ref, sc_ref, sh_ref, xs_in_ref, xs_ref, hbuf, sem):
    del xs_in_ref
    i = pl.program_id(0)
    tm = hbuf.shape[0]
    hbuf[...] = _rms_mod(x_ref[...], n2_ref[...], sc_ref[...], sh_ref[...])

    def row_copy(r, s):
        return pltpu.make_async_copy(hbuf.at[pl.ds(r, 1)], xs_ref.at[pl.ds(s, 1)], sem)

    def issue(r, carry):
        for kk in range(TOP_K):
            row_copy(r, slot_ref[(i * tm + r) * TOP_K + kk]).start()
        return carry

    lax.fori_loop(0, tm, issue, 0)

    def drain(r, carry):
        for kk in range(TOP_K):
            row_copy(0, 0).wait()
        return carry

    lax.fori_loop(0, tm, drain, 0)


def _dispatch(x1_b, slots_b, norm2_g, scale2_b, shift2_b, xs):
    seq, d = x1_b.shape
    tm = _tile(seq, 256)
    grid_spec = pltpu.PrefetchScalarGridSpec(
        num_scalar_prefetch=1,
        grid=(seq // tm,),
        in_specs=[pl.BlockSpec((tm, d), lambda i, s: (i, 0)),
                  pl.BlockSpec((1, d), lambda i, s: (0, 0)),
                  pl.BlockSpec((1, d), lambda i, s: (0, 0)),
                  pl.BlockSpec((1, d), lambda i, s: (0, 0)),
                  pl.BlockSpec(memory_space=pl.ANY)],
        out_specs=pl.BlockSpec(memory_space=pl.ANY),
        scratch_shapes=[pltpu.VMEM((tm, d), F32), pltpu.SemaphoreType.DMA(())],
    )
    return pl.pallas_call(
        _dispatch_kernel,
        grid_spec=grid_spec,
        out_shape=jax.ShapeDtypeStruct(xs.shape, xs.dtype),
        input_output_aliases={5: 0},
        compiler_params=_params("arbitrary"),
        name="moe_dispatch",
    )(slots_b, x1_b, norm2_g.reshape(1, d), scale2_b, shift2_b, xs)


def _expert_kernel(be_ref, na_ref, x_ref, wg_ref, wl_ref, bg_ref, bl_ref, wd_ref, bd_ref,
                   o_ref, acc, *, n_f):
    b = pl.program_id(0)
    f = pl.program_id(1)

    @pl.when(b < na_ref[0])
    def _():
        x = x_ref[...].astype(BF16)
        gate = jnp.dot(x, wg_ref[0], preferred_element_type=F32) + bg_ref[0]
        lin = jnp.dot(x, wl_ref[0], preferred_element_type=F32) + bl_ref[0]
        gate = jnp.minimum(gate, SWIGLU_LIMIT)
        lin = jnp.clip(lin, -SWIGLU_LIMIT, SWIGLU_LIMIT)
        act = gate * _sigmoid(SWIGLU_ALPHA * gate) * (lin + 1.0)
        contrib = jnp.dot(act.astype(BF16), wd_ref[0], preferred_element_type=F32)

        @pl.when(f == 0)
        def _():
            acc[...] = contrib

        @pl.when(f > 0)
        def _():
            acc[...] += contrib

        @pl.when(f == n_f - 1)
        def _():
            o_ref[...] = acc[...] + bd_ref[0]

    @pl.when(jnp.logical_and(b >= na_ref[0], f == n_f - 1))
    def _():
        o_ref[...] = jnp.zeros_like(o_ref)


def _experts(xs, blk_e, n_active, w_gu, b_gu, w_dn, b_dn):
    n_slots, d = xs.shape
    n_e, _, two_ff = w_gu.shape
    d_ff = two_ff // 2
    bm = EXPERT_ROWS
    n_blocks = n_slots // bm
    tf = _tile(d_ff, EXPERT_FF_TILE, LANES)
    n_f = d_ff // tf

    def blk(b, na):
        return jnp.minimum(b, na[0] - 1)

    def ff(b, f, na):
        return jnp.where(b < na[0], f, n_f - 1)

    grid_spec = pltpu.PrefetchScalarGridSpec(
        num_scalar_prefetch=2,
        grid=(n_blocks, n_f),
        in_specs=[pl.BlockSpec((bm, d), lambda b, f, be, na: (blk(b, na), 0)),
                  pl.BlockSpec((1, d, tf), lambda b, f, be, na: (be[blk(b, na)], 0, ff(b, f, na))),
                  pl.BlockSpec((1, d, tf), lambda b, f, be, na: (be[blk(b, na)], 0, n_f + ff(b, f, na))),
                  pl.BlockSpec((1, 1, tf), lambda b, f, be, na: (be[blk(b, na)], 0, ff(b, f, na))),
                  pl.BlockSpec((1, 1, tf), lambda b, f, be, na: (be[blk(b, na)], 0, n_f + ff(b, f, na))),
                  pl.BlockSpec((1, tf, d), lambda b, f, be, na: (be[blk(b, na)], ff(b, f, na), 0)),
                  pl.BlockSpec((1, 1, d), lambda b, f, be, na: (be[blk(b, na)], 0, 0))],
        out_specs=pl.BlockSpec((bm, d), lambda b, f, be, na: (b, 0)),
        scratch_shapes=[pltpu.VMEM((bm, d), F32)],
    )
    return pl.pallas_call(
        functools.partial(_expert_kernel, n_f=n_f),
        grid_spec=grid_spec,
        out_shape=jax.ShapeDtypeStruct((n_slots, d), F32),
        compiler_params=_params("arbitrary", "arbitrary"),
        name="moe_experts",
    )(blk_e, n_active, xs, w_gu, w_gu, b_gu.reshape(n_e, 1, two_ff), b_gu.reshape(n_e, 1, two_ff),
      w_dn, b_dn.reshape(n_e, 1, d))


def _combine_kernel(slot_ref, x_ref, route_ref, g2_ref, gf_ref, ys_ref, o_ref, buf, sem):
    i = pl.program_id(0)
    tm = x_ref.shape[0]

    def row_copy(kk, r, s):
        return pltpu.make_async_copy(ys_ref.at[pl.ds(s, 1)], buf.at[kk, pl.ds(r, 1)], sem)

    def issue(r, carry):
        for kk in range(TOP_K):
            row_copy(kk, r, slot_ref[(i * tm + r) * TOP_K + kk]).start()
        return carry

    lax.fori_loop(0, tm, issue, 0)

    def drain(r, carry):
        for kk in range(TOP_K):
            row_copy(0, 0, 0).wait()
        return carry

    lax.fori_loop(0, tm, drain, 0)

    route = route_ref[...]
    y = jnp.zeros(x_ref.shape, F32)
    for kk in range(TOP_K):
        y = y + buf[kk] * route[:, TOP_K + kk:TOP_K + kk + 1]
    x2 = x_ref[...] + g2_ref[...] * y
    ms = jnp.mean(x2 * x2, axis=-1, keepdims=True)
    o_ref[...] = x2 * lax.rsqrt(ms + NORM_EPS) * gf_ref[...]


def _combine(x1_b, route_b, slots_b, gate2_b, norm_final_g, ys):
    seq, d = x1_b.shape
    tm = _tile(seq, 256)
    grid_spec = pltpu.PrefetchScalarGridSpec(
        num_scalar_prefetch=1,
        grid=(seq // tm,),
        in_specs=[pl.BlockSpec((tm, d), lambda i, s: (i, 0)),
                  pl.BlockSpec((tm, LANES), lambda i, s: (i, 0)),
                  pl.BlockSpec((1, d), lambda i, s: (0, 0)),
                  pl.BlockSpec((1, d), lambda i, s: (0, 0)),
                  pl.BlockSpec(memory_space=pl.ANY)],
        out_specs=pl.BlockSpec((tm, d), lambda i, s: (i, 0)),
        scratch_shapes=[pltpu.VMEM((TOP_K, tm, d), F32), pltpu.SemaphoreType.DMA(())],
    )
    return pl.pallas_call(
        _combine_kernel,
        grid_spec=grid_spec,
        out_shape=jax.ShapeDtypeStruct((seq, d), F32),
        compiler_params=_params("arbitrary"),
        name="moe_combine",
    )(slots_b, x1_b, route_b, gate2_b, norm_final_g.reshape(1, d), ys)


def _pad_cols(w, n):
    return jnp.pad(w, ((0, 0), (0, n - w.shape[1])))


def kernel(x, c, w_ada, b_ada, norm1_g, w_in, rwkv_mu, w_decay_up, decay_w0, w_iclr_up, iclr_a0, w_gate_up_rwkv, rwkv_k_k, rwkv_k_a, rwkv_r_k, rwkv_gn_g, rwkv_gn_b, w_out_a, fox_b_f, fox_q_norm, fox_k_norm, w_out_b, w_out, norm2_g, w_router, b_router, w_expert_gu, b_expert_gu, w_expert_down, b_expert_down, norm_final_g):
    bsz, seq, d = x.shape
    rw_heads, rw_hd = rwkv_r_k.shape
    rw = rw_heads * rw_hd
    dr, ir, gr = w_decay_up.shape[0], w_iclr_up.shape[0], w_gate_up_rwkv.shape[0]
    fh = fox_b_f.shape[0]
    fw = w_out_b.shape[0]
    fhd = fw // fh
    assert fhd == LANES and dr <= LANES and ir <= LANES and gr % LANES == 0
    n_experts = w_router.shape[1]

    mod = _ada(c, w_ada, b_ada)
    shift1, scale1, gate1, shift2, scale2, gate2 = (m[:, None, :] for m in jnp.split(mod, 6, axis=-1))

    o_r = 0
    o_f = 3 * rw + dr + ir + gr
    o_g = o_f + 3 * fw + fh + fw
    col = lambda a, n: w_in[:, a:a + n]
    w_rwkv = jnp.concatenate([col(0, 3 * rw), _pad_cols(col(3 * rw, dr), LANES),
                              _pad_cols(col(3 * rw + dr, ir), LANES), col(3 * rw + dr + ir, gr)],
                             axis=1).astype(BF16)
    mu = jnp.concatenate([rwkv_mu[:3 * rw], jnp.pad(rwkv_mu[3 * rw:3 * rw + dr], (0, LANES - dr)),
                          jnp.pad(rwkv_mu[3 * rw + dr:3 * rw + dr + ir], (0, LANES - ir)),
                          rwkv_mu[3 * rw + dr + ir:]])
    w_qk = col(o_f, 2 * fw).astype(BF16)
    w_v = col(o_f + 2 * fw, fw).astype(BF16)
    w_ogf = jnp.concatenate([col(o_f + 3 * fw + fh, fw), _pad_cols(col(o_f + 3 * fw, fh), LANES)],
                            axis=1).astype(BF16)
    w_gates = col(o_g, 2 * d).astype(BF16)
    qk_gain = jnp.concatenate([jnp.tile(fox_q_norm * (fhd ** -0.5 * LOG2E), fh), jnp.tile(fox_k_norm, fh)])

    def row(extra, n):
        return (jnp.zeros((n,), F32) if extra is None else extra).reshape(1, n)

    gates, h1 = _inproj(x, norm1_g, scale1, shift1, w_gates, row(None, w_gates.shape[1]), BF16,
                        _ep_sigmoid, "inproj_merge_gates")

    def proj(w, extra, dtype, ep, name):
        return _proj(h1, w, row(extra, w.shape[1]), dtype, ep, name)

    p_rwkv = proj(w_rwkv, None, F32, _ep_identity, "inproj_rwkv")
    qk = proj(w_qk, qk_gain, BF16, _ep_headnorm, "inproj_fox_qk")
    v_fox = proj(w_v, None, BF16, _ep_identity, "inproj_fox_v")
    ogf = proj(w_ogf, None, F32, _ep_identity, "inproj_fox_gate_forget")

    wd = jnp.pad(w_decay_up, ((0, LANES - dr), (0, 0)))
    wi = jnp.pad(w_iclr_up, ((0, LANES - ir), (0, 0)))
    r, k2, v, lw, kk, aic, g = _rwkv_prep(p_rwkv, mu, wd, decay_w0, wi, iclr_a0,
                                          w_gate_up_rwkv.astype(BF16), rwkv_k_k, rwkv_k_a, rw, gr)
    y_a = _rwkv_scan(r, k2, v, lw, kk, aic, g, rwkv_r_k, rwkv_gn_g, rwkv_gn_b, rw_hd)

    f_t = jnp.transpose(ogf[:, :, fw:fw + fh], (0, 2, 1))
    cum = _logf_cumsum(f_t, fox_b_f)
    y_b = _fox_attention(qk, v_fox, cum, ogf, fh)

    merged = _merge(y_a, y_b, gates, w_out_a.astype(BF16), w_out_b.astype(BF16))
    x1, route, counts = _route(merged, x, gate1, w_out.astype(BF16), norm2_g, scale2, shift2,
                               w_router, b_router)

    n_tok = bsz * seq
    n_assign = n_tok * TOP_K
    bm = EXPERT_ROWS
    n_blocks = -(-n_assign // bm) + n_experts
    cnt = counts[0, :n_experts].astype(jnp.int32)
    padded = (cnt + bm - 1) // bm * bm
    pad_end = jnp.cumsum(padded)
    pad_start = pad_end - padded
    top_i = route[:, :, 0:TOP_K].astype(jnp.int32)
    rank = route[:, :, 2 * TOP_K:3 * TOP_K].astype(jnp.int32)
    e_ids = jnp.arange(n_experts, dtype=jnp.int32)
    start_of = jnp.sum(jnp.where(top_i[..., None] == e_ids, pad_start, 0), axis=-1)
    slots = (start_of + rank).reshape(bsz, seq * TOP_K)
    blk_first = jnp.arange(n_blocks, dtype=jnp.int32) * bm
    blk_e = jnp.minimum(jnp.sum((pad_end[None, :] <= blk_first[:, None]).astype(jnp.int32), axis=1),
                        n_experts - 1)
    n_active = (pad_end[-1:] // bm).astype(jnp.int32)

    xs = jnp.zeros((n_blocks * bm, d), F32)
    for b in range(bsz):
        xs = _dispatch(x1[b], slots[b], norm2_g, scale2[b], shift2[b], xs)
    ys = _experts(xs, blk_e, n_active, w_expert_gu.astype(BF16), b_expert_gu,
                  w_expert_down.astype(BF16), b_expert_down)
    outs = [_combine(x1[b], route[b], slots[b], gate2[b], norm_final_g, ys) for b in range(bsz)]
    return jnp.stack(outs, axis=0)
```

```python
import functools

import jax
import jax.numpy as jnp
from jax import lax
from jax.experimental import pallas as pl
from jax.experimental.pallas import tpu as pltpu

F32 = jnp.float32
BF16 = jnp.bfloat16
HIGHEST = lax.Precision.HIGHEST

TOP_K = 4
NORM_EPS = 1e-6
GN_EPS = 64e-5
SWIGLU_LIMIT = 7.0
SWIGLU_ALPHA = 1.702
LANES = 128
RWKV_CHUNK = 64
RWKV_PAIRS = 2
VMEM_LIMIT_BYTES = 56 * 1024 * 1024
EXPERT_ROWS = 512
EXPERT_FF_TILE = 1024
NEG_BIG = -1e30
LOG2E = 1.4426950408889634
FOX_ROWS = 256
FOX_TQ = 2048
FOX_TK = 1024


def _tile(n, pref, mult=8):
    t = min(pref, n)
    t -= t % mult
    while t >= mult:
        if n % t == 0:
            return t
        t -= mult
    return n


def _params(*sem):
    return pltpu.CompilerParams(dimension_semantics=sem, vmem_limit_bytes=VMEM_LIMIT_BYTES)


def _sigmoid(x):
    return 1.0 / (1.0 + jnp.exp(-x))


def _rms_mod(x, g, scale, shift):
    ms = jnp.mean(x * x, axis=-1, keepdims=True)
    return x * lax.rsqrt(ms + NORM_EPS) * g * (1.0 + scale) + shift


def _ada_kernel(c_ref, w_ref, b_ref, o_ref):
    c = c_ref[...]
    s = c * _sigmoid(c)
    o_ref[...] = jnp.dot(s, w_ref[...], preferred_element_type=F32, precision=HIGHEST) + b_ref[...]


def _ada(c, w_ada, b_ada):
    bsz, d = c.shape
    n = w_ada.shape[1]
    rows = 8
    c_pad = jnp.zeros((rows, d), F32).at[:bsz].set(c)
    tn = _tile(n, 1024, LANES)
    out = pl.pallas_call(
        _ada_kernel,
        grid=(n // tn,),
        in_specs=[pl.BlockSpec((rows, d), lambda j: (0, 0)),
                  pl.BlockSpec((d, tn), lambda j: (0, j)),
                  pl.BlockSpec((1, tn), lambda j: (0, j))],
        out_specs=pl.BlockSpec((rows, tn), lambda j: (0, j)),
        out_shape=jax.ShapeDtypeStruct((rows, n), F32),
        compiler_params=_params("parallel"),
        name="adaln",
    )(c_pad, w_ada, b_ada.reshape(1, n))
    return out[:bsz]


def _inproj_kernel(x_ref, g_ref, sc_ref, sh_ref, w_ref, e_ref, o_ref, h_ref, *, epilogue):
    @pl.when(pl.program_id(2) == 0)
    def _():
        h = _rms_mod(x_ref[0], g_ref[...], sc_ref[0], sh_ref[0])
        h_ref[0] = h.astype(h_ref.dtype)

    acc = jnp.dot(h_ref[0], w_ref[...], preferred_element_type=F32)
    o_ref[0] = epilogue(acc, e_ref[...]).astype(o_ref.dtype)


def _proj_kernel(h_ref, w_ref, e_ref, o_ref, *, epilogue):
    acc = jnp.dot(h_ref[0], w_ref[...], preferred_element_type=F32)
    o_ref[0] = epilogue(acc, e_ref[...]).astype(o_ref.dtype)


def _ep_identity(acc, extra):
    return acc


def _ep_sigmoid(acc, extra):
    return _sigmoid(acc)


def _ep_headnorm(acc, extra):
    outs = []
    for h in range(acc.shape[1] // LANES):
        a = acc[:, h * LANES:(h + 1) * LANES]
        ms = jnp.mean(a * a, axis=-1, keepdims=True)
        outs.append(a * lax.rsqrt(ms + NORM_EPS))
    return jnp.concatenate(outs, axis=1) * extra


def _inproj(x, g, scale, shift, w, extra, out_dtype, epilogue, name):
    bsz, seq, d = x.shape
    n = w.shape[1]
    tm = _tile(seq, 1024)
    tn = _tile(n, 512, LANES)
    return pl.pallas_call(
        functools.partial(_inproj_kernel, epilogue=epilogue),
        grid=(bsz, seq // tm, n // tn),
        in_specs=[pl.BlockSpec((1, tm, d), lambda b, i, j: (b, i, 0)),
                  pl.BlockSpec((1, d), lambda b, i, j: (0, 0)),
                  pl.BlockSpec((1, 1, d), lambda b, i, j: (b, 0, 0)),
                  pl.BlockSpec((1, 1, d), lambda b, i, j: (b, 0, 0)),
                  pl.BlockSpec((d, tn), lambda b, i, j: (0, j)),
                  pl.BlockSpec((1, tn), lambda b, i, j: (0, j))],
        out_specs=[pl.BlockSpec((1, tm, tn), lambda b, i, j: (b, i, j)),
                   pl.BlockSpec((1, tm, d), lambda b, i, j: (b, i, 0))],
        out_shape=[jax.ShapeDtypeStruct((bsz, seq, n), out_dtype),
                   jax.ShapeDtypeStruct((bsz, seq, d), BF16)],
        compiler_params=_params("parallel", "parallel", "arbitrary"),
        name=name,
    )(x, g.reshape(1, d), scale, shift, w, extra)


def _proj(h, w, extra, out_dtype, epilogue, name):
    bsz, seq, d = h.shape
    n = w.shape[1]
    tm = _tile(seq, 1024)
    tn = _tile(n, 512, LANES)
    return pl.pallas_call(
        functools.partial(_proj_kernel, epilogue=epilogue),
        grid=(bsz, seq // tm, n // tn),
        in_specs=[pl.BlockSpec((1, tm, d), lambda b, i, j: (b, i, 0)),
                  pl.BlockSpec((d, tn), lambda b, i, j: (0, j)),
                  pl.BlockSpec((1, tn), lambda b, i, j: (0, j))],
        out_specs=pl.BlockSpec((1, tm, tn), lambda b, i, j: (b, i, j)),
        out_shape=jax.ShapeDtypeStruct((bsz, seq, n), out_dtype),
        compiler_params=_params("parallel", "parallel", "parallel"),
        name=name,
    )(h, w, extra)


def _rwkv_prep_kernel(p_ref, mu_ref, wd_ref, w0_ref, wi_ref, a0_ref, wg_ref, kk_ref, ka_ref,
                      r_o, k_o, v_o, lw_o, kk_o, a_o, g_o, carry, *, width, gate_rank):
    @pl.when(pl.program_id(1) == 0)
    def _():
        carry[...] = jnp.zeros_like(carry)

    p = p_ref[0]
    tt = p.shape[0]
    row = lax.broadcasted_iota(jnp.int32, p.shape, 0)
    prev = jnp.where(row == 0, carry[...], pltpu.roll(p, 1, axis=0))
    carry[...] = p[tt - 1:tt, :]
    pm = p + mu_ref[...] * (prev - p)
    w = width
    r = pm[:, 0:w]
    k = pm[:, w:2 * w]
    v = pm[:, 2 * w:3 * w]
    d_lo = pm[:, 3 * w:3 * w + LANES]
    a_lo = pm[:, 3 * w + LANES:3 * w + 2 * LANES]
    g_lo = pm[:, 3 * w + 2 * LANES:3 * w + 2 * LANES + gate_rank]
    w_pre = w0_ref[...] + jnp.dot(jnp.tanh(d_lo), wd_ref[...], preferred_element_type=F32,
                                  precision=HIGHEST)
    lw_o[0] = -jnp.exp(-0.5) * _sigmoid(w_pre)
    a = _sigmoid(a0_ref[...] + jnp.dot(a_lo, wi_ref[...], preferred_element_type=F32,
                                       precision=HIGHEST))
    g_o[0] = jnp.dot(_sigmoid(g_lo).astype(BF16), wg_ref[...], preferred_element_type=F32)
    r_o[0] = r
    v_o[0] = v
    a_o[0] = a
    kk_o[0] = k * kk_ref[...]
    k_o[0] = k * (1.0 + (a - 1.0) * ka_ref[...])


def _rwkv_prep(p, mu, wd, w0, wi, a0, wg, k_k, k_a, width, gate_rank):
    bsz, seq, n = p.shape
    tt = _tile(seq, 256)
    row = lambda arr: arr.reshape(1, -1)
    full = lambda shape: pl.BlockSpec(shape, lambda b, i: (0,) * len(shape))
    out_sds = jax.ShapeDtypeStruct((bsz, seq, width), F32)
    out_spec = pl.BlockSpec((1, tt, width), lambda b, i: (b, i, 0))
    return pl.pallas_call(
        functools.partial(_rwkv_prep_kernel, width=width, gate_rank=gate_rank),
        grid=(bsz, seq // tt),
        in_specs=[pl.BlockSpec((1, tt, n), lambda b, i: (b, i, 0)),
                  full((1, n)), full(wd.shape), full((1, width)), full(wi.shape), full((1, width)),
                  full(wg.shape), full((1, width)), full((1, width))],
        out_specs=[out_spec] * 7,
        out_shape=[out_sds] * 7,
        scratch_shapes=[pltpu.VMEM((1, n), F32)],
        compiler_params=_params("parallel", "arbitrary"),
        name="rwkv_prep",
    )(p, row(mu), wd, row(w0), wi, row(a0), wg, row(k_k), row(k_a))


def _rwkv_scan_kernel(r_ref, k_ref, v_ref, lw_ref, kk_ref, a_ref, g_ref, rk_ref, gg_ref, gb_ref,
                      o_ref, s_scr, *, n_chunks, head_dim):
    L = RWKV_CHUNK
    L2 = 2 * L

    @pl.when(pl.program_id(2) == 0)
    def _():
        s_scr[...] = jnp.zeros_like(s_scr)

    lane = lax.broadcasted_iota(jnp.int32, (1, LANES), 1)
    m0 = (lane < head_dim).astype(F32)
    m1 = 1.0 - m0
    r2 = lax.broadcasted_iota(jnp.int32, (L2, L2), 0)
    c2 = lax.broadcasted_iota(jnp.int32, (L2, L2), 1)
    same = (r2 < L) == (c2 < L)
    strict = jnp.logical_and(same, c2 < r2)
    incl = jnp.logical_and(same, c2 <= r2)
    eye = (r2 == c2).astype(F32)
    rowi = lax.broadcasted_iota(jnp.int32, (L, LANES), 0)
    n_pairs = r_ref.shape[2] // LANES

    def stack_f32(x):
        return jnp.concatenate([x * m0, x * m1], axis=0)

    def stack(x):
        return stack_f32(x).astype(BF16)

    def head_sum(x):
        s0 = jnp.sum(x * m0, axis=-1, keepdims=True)
        s1 = jnp.sum(x * m1, axis=-1, keepdims=True)
        return s0 * m0 + s1 * m1

    def nt(a, b):
        return lax.dot_general(a, b, (((1,), (1,)), ((), ())), preferred_element_type=F32)

    def mm(a, b):
        return jnp.dot(a.astype(BF16), b.astype(BF16), preferred_element_type=F32)

    blk_masks = []
    size = 8
    while size <= L:
        blk_masks.append(jnp.bitwise_xor(r2, c2) < size)
        size *= 2
    off_masks = [jnp.logical_and(hi, jnp.logical_not(lo))
                 for lo, hi in zip(blk_masks[:-1], blk_masks[1:])]
    chunks = range(n_chunks)

    ctx = []
    for c, p in [(c, p) for c in chunks for p in range(n_pairs)]:
        sl = pl.ds(c * L, L)
        ln = pl.ds(p * LANES, LANES)
        r = r_ref[0, sl, ln]
        k = k_ref[0, sl, ln]
        v = v_ref[0, sl, ln]
        lw = lw_ref[0, sl, ln]
        kk = kk_ref[0, sl, ln]
        aic = a_ref[0, sl, ln]
        rk = rk_ref[:, ln]
        cum = lw
        sh = 1
        while sh < L:
            cum = cum + jnp.where(rowi >= sh, pltpu.roll(cum, sh, axis=0), 0.0)
            sh *= 2
        tot = cum[L - 1:L, :]
        w_rem = jnp.exp(tot - cum)
        w_inv = jnp.exp(-cum)
        kk = kk / jnp.maximum(jnp.sqrt(head_sum(kk * kk)), 1e-12)
        b_vec = kk * aic
        ar = jnp.concatenate([stack(-kk * jnp.exp(cum - lw)), stack(r * jnp.exp(cum))], axis=0)
        bk = jnp.concatenate([stack(b_vec * w_inv), stack(k * w_inv)], axis=0)
        v_st = stack(v)
        b_rem_t = stack_f32(b_vec * w_rem).T.astype(BF16)
        k_rem_t = stack_f32(k * w_rem).T.astype(BF16)
        decay = jnp.broadcast_to(jnp.exp(tot), (LANES, LANES)).T
        ctx.append(dict(sl=sl, ln=ln, p=p, ar=ar, bk=bk, v_st=v_st, decay=decay, b_rem_t=b_rem_t,
                        k_rem_t=k_rem_t, bonus=head_sum(r * k * rk) * v))

    for x in ctx:
        big = nt(x["ar"], x["bk"])
        x["n_ab"] = jnp.where(strict, big[0:L2, 0:L2], 0.0)
        m_ak = jnp.where(strict, big[0:L2, L2:2 * L2], 0.0)
        m_rk = jnp.where(incl, big[L2:2 * L2, L2:2 * L2], 0.0)
        x["m_rb"] = jnp.where(incl, big[L2:2 * L2, 0:L2], 0.0).astype(BF16)
        x["m_akrk"] = jnp.concatenate([m_ak, m_rk], axis=0).astype(BF16)
    for x in ctx:
        x["m_v"] = mm(x["m_akrk"], x["v_st"])
        x["kv"] = mm(x["k_rem_t"], x["v_st"])

    for x in ctx:
        x["d8"] = jnp.where(blk_masks[0], x["n_ab"], 0.0)
        x["t"] = eye + x["d8"]
    for x in ctx:
        x["pw"] = mm(x["d8"], x["d8"])
    for x in ctx:
        x["t"] = x["t"] + mm(x["t"], x["pw"])
    for x in ctx:
        x["pw"] = mm(x["pw"], x["pw"])
    for x in ctx:
        x["t"] = x["t"] + mm(x["t"], x["pw"])
    for off in off_masks:
        for x in ctx:
            x["nt"] = mm(jnp.where(off, x["n_ab"], 0.0), x["t"])
        for x in ctx:
            x["t"] = x["t"] + mm(x["t"], x["nt"])

    for x in ctx:
        rhs = jnp.concatenate([x["ar"][0:L2], x["m_v"][0:L2].astype(BF16)], axis=1)
        x["ta_uv"] = mm(x["t"], rhs).astype(BF16)
    for x in ctx:
        w_q = mm(x["b_rem_t"], x["ta_uv"])
        g_y = mm(x["m_rb"], x["ta_uv"])
        x["q"] = w_q[:, LANES:] + x["kv"]
        x["y_loc"] = g_y[:, LANES:] + x["m_v"][L2:2 * L2]
        g = g_y[:, :LANES] + x["ar"][L2:2 * L2].astype(F32)
        x["wg"] = jnp.concatenate([w_q[:, :LANES], g], axis=0).astype(BF16)

    s = [s_scr[p] for p in range(n_pairs)]
    for x in ctx:
        p = x["p"]
        ws = mm(x["wg"], s[p])
        x["y_st"] = ws[L2:2 * L2] + x["y_loc"]
        s[p] = x["decay"] * s[p] + ws[0:L2] + x["q"]
    for p in range(n_pairs):
        s_scr[p] = s[p]

    inv_n = 1.0 / head_dim
    for x in ctx:
        y_st = x["y_st"]
        y = y_st[0:L] + y_st[L:L2]
        mean = head_sum(y) * inv_n
        yc = y - mean
        var = head_sum(yc * yc) * inv_n
        y_gn = yc * lax.rsqrt(var + GN_EPS) * gg_ref[:, x["ln"]] + gb_ref[:, x["ln"]]
        o_ref[0, x["sl"], x["ln"]] = ((y_gn + x["bonus"]) * g_ref[0, x["sl"], x["ln"]]).astype(o_ref.dtype)


def _rwkv_scan(r, k, v, lw, kk, a, g, r_k, gn_g, gn_b, head_dim):
    bsz, seq, width = r.shape
    assert 2 * head_dim == LANES and width % LANES == 0
    tc = _tile(seq, 8 * RWKV_CHUNK, RWKV_CHUNK)
    pw = _tile(width, RWKV_PAIRS * LANES, LANES)
    tok = pl.BlockSpec((1, tc, pw), lambda b, h, i: (b, i, h))
    par = pl.BlockSpec((1, pw), lambda b, h, i: (0, h))
    row = lambda arr: arr.reshape(1, width)
    return pl.pallas_call(
        functools.partial(_rwkv_scan_kernel, n_chunks=tc // RWKV_CHUNK, head_dim=head_dim),
        grid=(bsz, width // pw, seq // tc),
        in_specs=[tok] * 7 + [par] * 3,
        out_specs=tok,
        out_shape=jax.ShapeDtypeStruct((bsz, seq, width), BF16),
        scratch_shapes=[pltpu.VMEM((pw // LANES, LANES, LANES), F32)],
        compiler_params=_params("parallel", "parallel", "arbitrary"),
        name="rwkv_scan",
    )(r, k, v, lw, kk, a, g, row(r_k), row(gn_g), row(gn_b))


def _logf_cumsum_kernel(f_ref, b_ref, o_ref, carry):
    @pl.when(pl.program_id(1) == 0)
    def _():
        carry[...] = jnp.zeros_like(carry)

    z = f_ref[0] + b_ref[...]
    lf = jnp.minimum(z, 0.0) - jnp.log(1.0 + jnp.exp(-jnp.abs(z)))
    ts = z.shape[1]
    ri = lax.broadcasted_iota(jnp.int32, (LANES, LANES), 0)
    ci = lax.broadcasted_iota(jnp.int32, (LANES, LANES), 1)
    upper = (ri <= ci).astype(F32)
    run = carry[...]
    for j in range(ts // LANES):
        blk = jnp.dot(lf[:, j * LANES:(j + 1) * LANES], upper, preferred_element_type=F32,
                      precision=HIGHEST) + run
        o_ref[0, :, j * LANES:(j + 1) * LANES] = blk
        run = blk[:, LANES - 1:LANES]
    carry[...] = run


def _logf_cumsum(f_t, b_f):
    bsz, nh, seq = f_t.shape
    ts = _tile(seq, 2048, LANES)
    return pl.pallas_call(
        _logf_cumsum_kernel,
        grid=(bsz, seq // ts),
        in_specs=[pl.BlockSpec((1, nh, ts), lambda b, i: (b, 0, i)),
                  pl.BlockSpec((nh, 1), lambda b, i: (0, 0))],
        out_specs=pl.BlockSpec((1, nh, ts), lambda b, i: (b, 0, i)),
        out_shape=jax.ShapeDtypeStruct((bsz, nh, seq), F32),
        scratch_shapes=[pltpu.VMEM((nh, 1), F32)],
        compiler_params=_params("parallel", "arbitrary"),
        name="fox_logf_cumsum",
    )(f_t, b_f.reshape(nh, 1))


def _fox_kernel(qi_ref, ki_ref, q_ref, k_ref, v_ref, ck_ref, og_ref, o_ref,
                m_scr, l_scr, acc_scr, *, rows):
    p = pl.program_id(2)
    qi = qi_ref[p]
    ki = ki_ref[p]
    tq = q_ref.shape[1]
    tk = k_ref.shape[1]

    @pl.when(ki == 0)
    def _():
        m_scr[...] = jnp.full_like(m_scr, NEG_BIG)
        l_scr[...] = jnp.zeros_like(l_scr)
        acc_scr[...] = jnp.zeros_like(acc_scr)

    n_rc = tq // rows

    n_kq = tq // tk

    def step(diag):
        k = k_ref[0]
        v = v_ref[0]
        ck = ck_ref[0, 0] * LOG2E
        col0 = 0 if diag is None else diag * tk
        live = [rc for rc in range(n_rc) if diag is None or col0 < (rc + 1) * rows]

        def n_cols(rc):
            return tk if diag is None else min(tk, (rc + 1) * rows - col0)

        def logits(rc):
            kc = n_cols(rc)
            s = lax.dot_general(q_ref[0, pl.ds(rc * rows, rows), :], k[:kc], (((1,), (1,)), ((), ())),
                                preferred_element_type=F32) - ck[:, :kc]
            if diag is not None and col0 + kc - 1 > rc * rows:
                ri = lax.broadcasted_iota(jnp.int32, s.shape, 0) + rc * rows
                ci = lax.broadcasted_iota(jnp.int32, s.shape, 1) + col0
                s = jnp.where(ci <= ri, s, NEG_BIG)
            return s

        old = {rc: (m_scr[pl.ds(rc * rows, rows), :], l_scr[pl.ds(rc * rows, rows), :],
                    acc_scr[pl.ds(rc * rows, rows), :]) for rc in live}
        new = {}
        s_next = logits(live[0])
        for i, rc in enumerate(live):
            s = s_next
            if i + 1 < len(live):
                s_next = logits(live[i + 1])
            m_prev, l_prev, acc_prev = old[rc]
            tiles = [s[:, j * LANES:(j + 1) * LANES] for j in range(n_cols(rc) // LANES)]
            m_new = jnp.maximum(m_prev, jnp.max(functools.reduce(jnp.maximum, tiles),
                                                axis=-1, keepdims=True))
            alpha = jnp.exp2(m_prev - m_new)
            p_tiles = [jnp.exp2(t - m_new) for t in tiles]
            l_new = alpha * l_prev + functools.reduce(jnp.add, p_tiles)
            pr = jnp.concatenate([t.astype(BF16) for t in p_tiles], axis=1)
            acc_new = alpha * acc_prev + jnp.dot(pr, v[:n_cols(rc)], preferred_element_type=F32)
            new[rc] = (m_new, l_new, acc_new)
        for rc in live:
            rs = pl.ds(rc * rows, rows)
            m_scr[rs, :], l_scr[rs, :], acc_scr[rs, :] = new[rc]

    @pl.when(ki < qi * n_kq)
    def _():
        step(None)

    for diag in range(n_kq):
        @pl.when(ki == qi * n_kq + diag)
        def _(diag=diag):
            step(diag)

    @pl.when(ki == qi * n_kq + n_kq - 1)
    def _():
        o = acc_scr[...] / jnp.sum(l_scr[...], axis=-1, keepdims=True)
        o_ref[0] = (o * _sigmoid(og_ref[0])).astype(o_ref.dtype)


def _fox_attention(qk, v, cum, ogf, n_heads):
    bsz, seq, _ = v.shape
    tk = _tile(seq, FOX_TK, LANES)
    tq = _tile(seq, FOX_TQ, tk)
    qi_list, ki_list = [], []
    for qi in range(seq // tq):
        for ki in range((qi + 1) * (tq // tk)):
            qi_list.append(qi)
            ki_list.append(ki)
    qi_arr = jnp.asarray(qi_list, jnp.int32)
    ki_arr = jnp.asarray(ki_list, jnp.int32)
    cum_row = cum[:, :, None, :]
    nh = n_heads
    grid_spec = pltpu.PrefetchScalarGridSpec(
        num_scalar_prefetch=2,
        grid=(bsz, nh, len(qi_list)),
        in_specs=[pl.BlockSpec((1, tq, LANES), lambda b, h, p, qi, ki: (b, qi[p], h)),
                  pl.BlockSpec((1, tk, LANES), lambda b, h, p, qi, ki: (b, ki[p], nh + h)),
                  pl.BlockSpec((1, tk, LANES), lambda b, h, p, qi, ki: (b, ki[p], h)),
                  pl.BlockSpec((1, 1, 1, tk), lambda b, h, p, qi, ki: (b, h, 0, ki[p])),
                  pl.BlockSpec((1, tq, LANES), lambda b, h, p, qi, ki: (b, qi[p], h))],
        out_specs=pl.BlockSpec((1, tq, LANES), lambda b, h, p, qi, ki: (b, qi[p], h)),
        scratch_shapes=[pltpu.VMEM((tq, LANES), F32), pltpu.VMEM((tq, LANES), F32),
                        pltpu.VMEM((tq, LANES), F32)],
    )
    return pl.pallas_call(
        functools.partial(_fox_kernel, rows=_tile(tq, FOX_ROWS, LANES)),
        grid_spec=grid_spec,
        out_shape=jax.ShapeDtypeStruct((bsz, seq, nh * LANES), BF16),
        compiler_params=_params("parallel", "parallel", "arbitrary"),
        name="fox_attention",
    )(qi_arr, ki_arr, qk, qk, v, cum_row, ogf)


def _merge_kernel(ya_ref, yb_ref, ga_ref, gb_ref, wa_ref, wb_ref, o_ref):
    pa = jnp.dot(ya_ref[0], wa_ref[...], preferred_element_type=F32)
    pb = jnp.dot(yb_ref[0], wb_ref[...], preferred_element_type=F32)
    o_ref[0] = (ga_ref[0].astype(F32) * pa + gb_ref[0].astype(F32) * pb).astype(o_ref.dtype)


def _merge(y_a, y_b, gates, w_a, w_b):
    bsz, seq, wa = y_a.shape
    wb = y_b.shape[2]
    d = w_a.shape[1]
    tm = _tile(seq, 512)
    return pl.pallas_call(
        _merge_kernel,
        grid=(bsz, seq // tm),
        in_specs=[pl.BlockSpec((1, tm, wa), lambda b, i: (b, i, 0)),
                  pl.BlockSpec((1, tm, wb), lambda b, i: (b, i, 0)),
                  pl.BlockSpec((1, tm, d), lambda b, i: (b, i, 0)),
                  pl.BlockSpec((1, tm, d), lambda b, i: (b, i, 1)),
                  pl.BlockSpec((wa, d), lambda b, i: (0, 0)),
                  pl.BlockSpec((wb, d), lambda b, i: (0, 0))],
        out_specs=pl.BlockSpec((1, tm, d), lambda b, i: (b, i, 0)),
        out_shape=jax.ShapeDtypeStruct((bsz, seq, d), BF16),
        compiler_params=_params("parallel", "parallel"),
        name="branch_merge",
    )(y_a, y_b, gates, gates, w_a, w_b)


def _route_kernel(m_ref, x_ref, g1_ref, w_ref, n2_ref, sc_ref, sh_ref, wr_ref, br_ref,
                  x1_ref, route_ref, cnt_ref, base, *, n_experts):
    @pl.when(jnp.logical_and(pl.program_id(0) == 0, pl.program_id(1) == 0))
    def _():
        base[...] = jnp.zeros_like(base)

    x1 = x_ref[0] + g1_ref[0] * jnp.dot(m_ref[0], w_ref[...], preferred_element_type=F32)
    x1_ref[0] = x1
    h = _rms_mod(x1, n2_ref[...], sc_ref[0], sh_ref[0])
    h_hi = h.astype(BF16)
    h_lo = (h - h_hi.astype(F32)).astype(BF16)
    logits = jnp.dot(jnp.concatenate([h_hi, h_hi, h_lo], axis=1), wr_ref[...],
                     preferred_element_type=F32) + br_ref[...]
    tm = logits.shape[0]
    lane_i = lax.broadcasted_iota(jnp.int32, (tm, LANES), 1)
    lane = lane_i.astype(F32)
    vals = jnp.where(lane_i < n_experts, logits, -jnp.inf)
    top_v, top_i, hot = [], [], []
    for _ in range(TOP_K):
        mx = jnp.max(vals, axis=-1, keepdims=True)
        ix = jnp.min(jnp.where(vals == mx, lane, float(LANES)), axis=-1, keepdims=True)
        sel = lane == ix
        vals = jnp.where(sel, -jnp.inf, vals)
        top_v.append(mx)
        top_i.append(ix)
        hot.append(sel.astype(F32))
    ex = [jnp.exp(tv - top_v[0]) for tv in top_v]
    den = ex[0] + ex[1] + ex[2] + ex[3]
    cnt = hot[0] + hot[1] + hot[2] + hot[3]
    ri = lax.broadcasted_iota(jnp.int32, (tm, tm), 0)
    ci = lax.broadcasted_iota(jnp.int32, (tm, tm), 1)
    before = jnp.dot((ci < ri).astype(BF16), cnt.astype(BF16), preferred_element_type=F32)
    before = before + base[...]
    out = jnp.zeros((tm, LANES), F32)
    for kk in range(TOP_K):
        rank = jnp.sum(hot[kk] * before, axis=-1, keepdims=True)
        out = jnp.where(lane_i == kk, top_i[kk], out)
        out = jnp.where(lane_i == TOP_K + kk, ex[kk] / den, out)
        out = jnp.where(lane_i == 2 * TOP_K + kk, rank, out)
    route_ref[0] = out
    new_base = base[...] + jnp.sum(cnt, axis=0, keepdims=True)
    base[...] = new_base
    cnt_ref[...] = jnp.broadcast_to(new_base, cnt_ref.shape)


def _route(merged, x, gate1, w_out, norm2_g, scale2, shift2, w_router, b_router):
    bsz, seq, d = x.shape
    n_experts = w_router.shape[1]
    tm = _tile(seq, 512)
    wr = jnp.zeros((d, LANES), F32).at[:, :n_experts].set(w_router)
    wr_hi = wr.astype(BF16)
    wr_lo = (wr - wr_hi.astype(F32)).astype(BF16)
    wr = jnp.concatenate([wr_hi, wr_lo, wr_hi], axis=0)
    br = jnp.zeros((1, LANES), F32).at[0, :n_experts].set(b_router)
    mod = pl.BlockSpec((1, 1, d), lambda b, i: (b, 0, 0))
    tok = pl.BlockSpec((1, tm, d), lambda b, i: (b, i, 0))
    return pl.pallas_call(
        functools.partial(_route_kernel, n_experts=n_experts),
        grid=(bsz, seq // tm),
        in_specs=[tok, tok, mod,
                  pl.BlockSpec((d, d), lambda b, i: (0, 0)),
                  pl.BlockSpec((1, d), lambda b, i: (0, 0)), mod, mod,
                  pl.BlockSpec((3 * d, LANES), lambda b, i: (0, 0)),
                  pl.BlockSpec((1, LANES), lambda b, i: (0, 0))],
        out_specs=[tok,
                   pl.BlockSpec((1, tm, LANES), lambda b, i: (b, i, 0)),
                   pl.BlockSpec((8, LANES), lambda b, i: (0, 0))],
        out_shape=[jax.ShapeDtypeStruct((bsz, seq, d), F32),
                   jax.ShapeDtypeStruct((bsz, seq, LANES), F32),
                   jax.ShapeDtypeStruct((8, LANES), F32)],
        scratch_shapes=[pltpu.VMEM((1, LANES), F32)],
        compiler_params=_params("arbitrary", "arbitrary"),
        name="residual_router",
    )(merged, x, gate1, w_out, norm2_g.reshape(1, d), scale2, shift2, wr, br)


def _dispatch_kernel(slot_ref, x_ref, n2_ref, sc_ref, sh_ref, xs_in_ref, xs_ref, hbuf, sem):
    del xs_in_ref
    i = pl.program_id(0)
    tm = hbuf.shape[0]
    hbuf[...] = _rms_mod(x_ref[...], n2_ref[...], sc_ref[...], sh_ref[...])

    def row_copy(r, s):
        return pltpu.make_async_copy(hbuf.at[pl.ds(r, 1)], xs_ref.at[pl.ds(s, 1)], sem)

    def issue(r, carry):
        for kk in range(TOP_K):
            row_copy(r, slot_ref[(i * tm + r) * TOP_K + kk]).start()
        return carry

    lax.fori_loop(0, tm, issue, 0)

    def drain(r, carry):
        for kk in range(TOP_K):
            row_copy(0, 0).wait()
        return carry

    lax.fori_loop(0, tm, drain, 0)


def _dispatch(x1_b, slots_b, norm2_g, scale2_b, shift2_b, xs):
    seq, d = x1_b.shape
    tm = _tile(seq, 256)
    grid_spec = pltpu.PrefetchScalarGridSpec(
        num_scalar_prefetch=1,
        grid=(seq // tm,),
        in_specs=[pl.BlockSpec((tm, d), lambda i, s: (i, 0)),
                  pl.BlockSpec((1, d), lambda i, s: (0, 0)),
                  pl.BlockSpec((1, d), lambda i, s: (0, 0)),
                  pl.BlockSpec((1, d), lambda i, s: (0, 0)),
                  pl.BlockSpec(memory_space=pl.ANY)],
        out_specs=pl.BlockSpec(memory_space=pl.ANY),
        scratch_shapes=[pltpu.VMEM((tm, d), F32), pltpu.SemaphoreType.DMA(())],
    )
    return pl.pallas_call(
        _dispatch_kernel,
        grid_spec=grid_spec,
        out_shape=jax.ShapeDtypeStruct(xs.shape, xs.dtype),
        input_output_aliases={5: 0},
        compiler_params=_params("arbitrary"),
        name="moe_dispatch",
    )(slots_b, x1_b, norm2_g.reshape(1, d), scale2_b, shift2_b, xs)


def _expert_kernel(be_ref, na_ref, x_ref, wg_ref, wl_ref, bg_ref, bl_ref, wd_ref, bd_ref,
                   o_ref, acc, *, n_f):
    b = pl.program_id(0)
    f = pl.program_id(1)

    @pl.when(b < na_ref[0])
    def _():
        x = x_ref[...].astype(BF16)
        gate = jnp.dot(x, wg_ref[0], preferred_element_type=F32) + bg_ref[0]
        lin = jnp.dot(x, wl_ref[0], preferred_element_type=F32) + bl_ref[0]
        gate = jnp.minimum(gate, SWIGLU_LIMIT)
        lin = jnp.clip(lin, -SWIGLU_LIMIT, SWIGLU_LIMIT)
        act = gate * _sigmoid(SWIGLU_ALPHA * gate) * (lin + 1.0)
        contrib = jnp.dot(act.astype(BF16), wd_ref[0], preferred_element_type=F32)

        @pl.when(f == 0)
        def _():
            acc[...] = contrib

        @pl.when(f > 0)
        def _():
            acc[...] += contrib

        @pl.when(f == n_f - 1)
        def _():
            o_ref[...] = acc[...] + bd_ref[0]

    @pl.when(jnp.logical_and(b >= na_ref[0], f == n_f - 1))
    def _():
        o_ref[...] = jnp.zeros_like(o_ref)


def _experts(xs, blk_e, n_active, w_gu, b_gu, w_dn, b_dn):
    n_slots, d = xs.shape
    n_e, _, two_ff = w_gu.shape
    d_ff = two_ff // 2
    bm = EXPERT_ROWS
    n_blocks = n_slots // bm
    tf = _tile(d_ff, EXPERT_FF_TILE, LANES)
    n_f = d_ff // tf

    def blk(b, na):
        return jnp.minimum(b, na[0] - 1)

    def ff(b, f, na):
        return jnp.where(b < na[0], f, n_f - 1)

    grid_spec = pltpu.PrefetchScalarGridSpec(
        num_scalar_prefetch=2,
        grid=(n_blocks, n_f),
        in_specs=[pl.BlockSpec((bm, d), lambda b, f, be, na: (blk(b, na), 0)),
                  pl.BlockSpec((1, d, tf), lambda b, f, be, na: (be[blk(b, na)], 0, ff(b, f, na))),
                  pl.BlockSpec((1, d, tf), lambda b, f, be, na: (be[blk(b, na)], 0, n_f + ff(b, f, na))),
                  pl.BlockSpec((1, 1, tf), lambda b, f, be, na: (be[blk(b, na)], 0, ff(b, f, na))),
                  pl.BlockSpec((1, 1, tf), lambda b, f, be, na: (be[blk(b, na)], 0, n_f + ff(b, f, na))),
                  pl.BlockSpec((1, tf, d), lambda b, f, be, na: (be[blk(b, na)], ff(b, f, na), 0)),
                  pl.BlockSpec((1, 1, d), lambda b, f, be, na: (be[blk(b, na)], 0, 0))],
        out_specs=pl.BlockSpec((bm, d), lambda b, f, be, na: (b, 0)),
        scratch_shapes=[pltpu.VMEM((bm, d), F32)],
    )
    return pl.pallas_call(
        functools.partial(_expert_kernel, n_f=n_f),
        grid_spec=grid_spec,
        out_shape=jax.ShapeDtypeStruct((n_slots, d), F32),
        compiler_params=_params("arbitrary", "arbitrary"),
        name="moe_experts",
    )(blk_e, n_active, xs, w_gu, w_gu, b_gu.reshape(n_e, 1, two_ff), b_gu.reshape(n_e, 1, two_ff),
      w_dn, b_dn.reshape(n_e, 1, d))


def _combine_kernel(slot_ref, x_ref, route_ref, g2_ref, gf_ref, ys_ref, o_ref, buf, sem):
    i = pl.program_id(0)
    tm = x_ref.shape[0]

    def row_copy(kk, r, s):
        return pltpu.make_async_copy(ys_ref.at[pl.ds(s, 1)], buf.at[kk, pl.ds(r, 1)], sem)

    def issue(r, carry):
        for kk in range(TOP_K):
            row_copy(kk, r, slot_ref[(i * tm + r) * TOP_K + kk]).start()
        return carry

    lax.fori_loop(0, tm, issue, 0)

    def drain(r, carry):
        for kk in range(TOP_K):
            row_copy(0, 0, 0).wait()
        return carry

    lax.fori_loop(0, tm, drain, 0)

    route = route_ref[...]
    y = jnp.zeros(x_ref.shape, F32)
    for kk in range(TOP_K):
        y = y + buf[kk] * route[:, TOP_K + kk:TOP_K + kk + 1]
    x2 = x_ref[...] + g2_ref[...] * y
    ms = jnp.mean(x2 * x2, axis=-1, keepdims=True)
    o_ref[...] = x2 * lax.rsqrt(ms + NORM_EPS) * gf_ref[...]


def _combine(x1_b, route_b, slots_b, gate2_b, norm_final_g, ys):
    seq, d = x1_b.shape
    tm = _tile(seq, 256)
    grid_spec = pltpu.PrefetchScalarGridSpec(
        num_scalar_prefetch=1,
        grid=(seq // tm,),
        in_specs=[pl.BlockSpec((tm, d), lambda i, s: (i, 0)),
                  pl.BlockSpec((tm, LANES), lambda i, s: (i, 0)),
                  pl.BlockSpec((1, d), lambda i, s: (0, 0)),
                  pl.BlockSpec((1, d), lambda i, s: (0, 0)),
                  pl.BlockSpec(memory_space=pl.ANY)],
        out_specs=pl.BlockSpec((tm, d), lambda i, s: (i, 0)),
        scratch_shapes=[pltpu.VMEM((TOP_K, tm, d), F32), pltpu.SemaphoreType.DMA(())],
    )
    return pl.pallas_call(
        _combine_kernel,
        grid_spec=grid_spec,
        out_shape=jax.ShapeDtypeStruct((seq, d), F32),
        compiler_params=_params("arbitrary"),
        name="moe_combine",
    )(slots_b, x1_b, route_b, gate2_b, norm_final_g.reshape(1, d), ys)


def _pad_cols(w, n):
    return jnp.pad(w, ((0, 0), (0, n - w.shape[1])))


def kernel(x, c, w_ada, b_ada, norm1_g, w_in, rwkv_mu, w_decay_up, decay_w0, w_iclr_up, iclr_a0, w_gate_up_rwkv, rwkv_k_k, rwkv_k_a, rwkv_r_k, rwkv_gn_g, rwkv_gn_b, w_out_a, fox_b_f, fox_q_norm, fox_k_norm, w_out_b, w_out, norm2_g, w_router, b_router, w_expert_gu, b_expert_gu, w_expert_down, b_expert_down, norm_final_g):
    bsz, seq, d = x.shape
    rw_heads, rw_hd = rwkv_r_k.shape
    rw = rw_heads * rw_hd
    dr, ir, gr = w_decay_up.shape[0], w_iclr_up.shape[0], w_gate_up_rwkv.shape[0]
    fh = fox_b_f.shape[0]
    fw = w_out_b.shape[0]
    fhd = fw // fh
    assert fhd == LANES and dr <= LANES and ir <= LANES and gr % LANES == 0
    n_experts = w_router.shape[1]

    mod = _ada(c, w_ada, b_ada)
    shift1, scale1, gate1, shift2, scale2, gate2 = (m[:, None, :] for m in jnp.split(mod, 6, axis=-1))

    o_r = 0
    o_f = 3 * rw + dr + ir + gr
    o_g = o_f + 3 * fw + fh + fw
    col = lambda a, n: w_in[:, a:a + n]
    w_rwkv = jnp.concatenate([col(0, 3 * rw), _pad_cols(col(3 * rw, dr), LANES),
                              _pad_cols(col(3 * rw + dr, ir), LANES), col(3 * rw + dr + ir, gr)],
                             axis=1).astype(BF16)
    mu = jnp.concatenate([rwkv_mu[:3 * rw], jnp.pad(rwkv_mu[3 * rw:3 * rw + dr], (0, LANES - dr)),
                          jnp.pad(rwkv_mu[3 * rw + dr:3 * rw + dr + ir], (0, LANES - ir)),
                          rwkv_mu[3 * rw + dr + ir:]])
    w_qk = col(o_f, 2 * fw).astype(BF16)
    w_v = col(o_f + 2 * fw, fw).astype(BF16)
    w_ogf = jnp.concatenate([col(o_f + 3 * fw + fh, fw), _pad_cols(col(o_f + 3 * fw, fh), LANES)],
                            axis=1).astype(BF16)
    w_gates = col(o_g, 2 * d).astype(BF16)
    qk_gain = jnp.concatenate([jnp.tile(fox_q_norm * (fhd ** -0.5 * LOG2E), fh), jnp.tile(fox_k_norm, fh)])

    def row(extra, n):
        return (jnp.zeros((n,), F32) if extra is None else extra).reshape(1, n)

    gates, h1 = _inproj(x, norm1_g, scale1, shift1, w_gates, row(None, w_gates.shape[1]), BF16,
                        _ep_sigmoid, "inproj_merge_gates")

    def proj(w, extra, dtype, ep, name):
        return _proj(h1, w, row(extra, w.shape[1]), dtype, ep, name)

    p_rwkv = proj(w_rwkv, None, F32, _ep_identity, "inproj_rwkv")
    qk = proj(w_qk, qk_gain, BF16, _ep_headnorm, "inproj_fox_qk")
    v_fox = proj(w_v, None, BF16, _ep_identity, "inproj_fox_v")
    ogf = proj(w_ogf, None, F32, _ep_identity, "inproj_fox_gate_forget")

    wd = jnp.pad(w_decay_up, ((0, LANES - dr), (0, 0)))
    wi = jnp.pad(w_iclr_up, ((0, LANES - ir), (0, 0)))
    r, k2, v, lw, kk, aic, g = _rwkv_prep(p_rwkv, mu, wd, decay_w0, wi, iclr_a0,
                                          w_gate_up_rwkv.astype(BF16), rwkv_k_k, rwkv_k_a, rw, gr)
    y_a = _rwkv_scan(r, k2, v, lw, kk, aic, g, rwkv_r_k, rwkv_gn_g, rwkv_gn_b, rw_hd)

    f_t = jnp.transpose(ogf[:, :, fw:fw + fh], (0, 2, 1))
    cum = _logf_cumsum(f_t, fox_b_f)
    y_b = _fox_attention(qk, v_fox, cum, ogf, fh)

    merged = _merge(y_a, y_b, gates, w_out_a.astype(BF16), w_out_b.astype(BF16))
    x1, route, counts = _route(merged, x, gate1, w_out.astype(BF16), norm2_g, scale2, shift2,
                               w_router, b_router)

    n_tok = bsz * seq
    n_assign = n_tok * TOP_K
    bm = EXPERT_ROWS
    n_blocks = -(-n_assign // bm) + n_experts
    cnt = counts[0, :n_experts].astype(jnp.int32)
    padded = (cnt + bm - 1) // bm * bm
    pad_end = jnp.cumsum(padded)
    pad_start = pad_end - padded
    top_i = route[:, :, 0:TOP_K].astype(jnp.int32)
    rank = route[:, :, 2 * TOP_K:3 * TOP_K].astype(jnp.int32)
    e_ids = jnp.arange(n_experts, dtype=jnp.int32)
    start_of = jnp.sum(jnp.where(top_i[..., None] == e_ids, pad_start, 0), axis=-1)
    slots = (start_of + rank).reshape(bsz, seq * TOP_K)
    blk_first = jnp.arange(n_blocks, dtype=jnp.int32) * bm
    blk_e = jnp.minimum(jnp.sum((pad_end[None, :] <= blk_first[:, None]).astype(jnp.int32), axis=1),
                        n_experts - 1)
    n_active = (pad_end[-1:] // bm).astype(jnp.int32)

    xs = jnp.zeros((n_blocks * bm, d), F32)
    for b in range(bsz):
        xs = _dispatch(x1[b], slots[b], norm2_g, scale2[b], shift2[b], xs)
    ys = _experts(xs, blk_e, n_active, w_expert_gu.astype(BF16), b_expert_gu,
                  w_expert_down.astype(BF16), b_expert_down)
    outs = [_combine(x1[b], route[b], slots[b], gate2[b], norm_final_g, ys) for b in range(bsz)]
    return jnp.stack(outs, axis=0)
```

```python
import functools

import jax
import jax.numpy as jnp
from jax import lax
from jax.experimental import pallas as pl
from jax.experimental.pallas import tpu as pltpu

F32 = jnp.float32
BF16 = jnp.bfloat16
HIGHEST = lax.Precision.HIGHEST

TOP_K = 4
NORM_EPS = 1e-6
GN_EPS = 64e-5
SWIGLU_LIMIT = 7.0
SWIGLU_ALPHA = 1.702
LANES = 128
RWKV_CHUNK = 64
RWKV_PAIRS = 4
VMEM_LIMIT_BYTES = 56 * 1024 * 1024
EXPERT_ROWS = 512
EXPERT_FF_TILE = 2048
NEG_BIG = -1e30
LOG2E = 1.4426950408889634
FOX_ROWS = 256
FOX_TQ = 2048
FOX_TK = 1024


def _tile(n, pref, mult=8):
    t = min(pref, n)
    t -= t % mult
    while t >= mult:
        if n % t == 0:
            return t
        t -= mult
    return n


def _params(*sem):
    return pltpu.CompilerParams(dimension_semantics=sem, vmem_limit_bytes=VMEM_LIMIT_BYTES)


def _sigmoid(x):
    return 1.0 / (1.0 + jnp.exp(-x))


def _rms_mod(x, g, scale, shift):
    ms = jnp.mean(x * x, axis=-1, keepdims=True)
    return x * lax.rsqrt(ms + NORM_EPS) * g * (1.0 + scale) + shift


def _ada_kernel(c_ref, w_ref, b_ref, o_ref):
    c = c_ref[...]
    s = c * _sigmoid(c)
    o_ref[...] = jnp.dot(s, w_ref[...], preferred_element_type=F32, precision=HIGHEST) + b_ref[...]


def _ada(c, w_ada, b_ada):
    bsz, d = c.shape
    n = w_ada.shape[1]
    rows = 8
    c_pad = jnp.zeros((rows, d), F32).at[:bsz].set(c)
    tn = _tile(n, 1024, LANES)
    out = pl.pallas_call(
        _ada_kernel,
        grid=(n // tn,),
        in_specs=[pl.BlockSpec((rows, d), lambda j: (0, 0)),
                  pl.BlockSpec((d, tn), lambda j: (0, j)),
                  pl.BlockSpec((1, tn), lambda j: (0, j))],
        out_specs=pl.BlockSpec((rows, tn), lambda j: (0, j)),
        out_shape=jax.ShapeDtypeStruct((rows, n), F32),
        compiler_params=_params("parallel"),
        name="adaln",
    )(c_pad, w_ada, b_ada.reshape(1, n))
    return out[:bsz]


def _inproj_kernel(x_ref, g_ref, sc_ref, sh_ref, w_ref, e_ref, o_ref, h_ref, *, epilogue):
    @pl.when(pl.program_id(2) == 0)
    def _():
        h = _rms_mod(x_ref[0], g_ref[...], sc_ref[0], sh_ref[0])
        h_ref[0] = h.astype(h_ref.dtype)

    acc = jnp.dot(h_ref[0], w_ref[...], preferred_element_type=F32)
    o_ref[0] = epilogue(acc, e_ref[...]).astype(o_ref.dtype)


def _proj_kernel(h_ref, w_ref, e_ref, o_ref, *, epilogue):
    acc = jnp.dot(h_ref[0], w_ref[...], preferred_element_type=F32)
    o_ref[0] = epilogue(acc, e_ref[...]).astype(o_ref.dtype)


def _ep_identity(acc, extra):
    return acc


def _ep_sigmoid(acc, extra):
    return _sigmoid(acc)


def _ep_headnorm(acc, extra):
    outs = []
    for h in range(acc.shape[1] // LANES):
        a = acc[:, h * LANES:(h + 1) * LANES]
        ms = jnp.mean(a * a, axis=-1, keepdims=True)
        outs.append(a * lax.rsqrt(ms + NORM_EPS))
    return jnp.concatenate(outs, axis=1) * extra


def _inproj(x, g, scale, shift, w, extra, out_dtype, epilogue, name):
    bsz, seq, d = x.shape
    n = w.shape[1]
    tm = _tile(seq, 1024)
    tn = _tile(n, 512, LANES)
    return pl.pallas_call(
        functools.partial(_inproj_kernel, epilogue=epilogue),
        grid=(bsz, seq // tm, n // tn),
        in_specs=[pl.BlockSpec((1, tm, d), lambda b, i, j: (b, i, 0)),
                  pl.BlockSpec((1, d), lambda b, i, j: (0, 0)),
                  pl.BlockSpec((1, 1, d), lambda b, i, j: (b, 0, 0)),
                  pl.BlockSpec((1, 1, d), lambda b, i, j: (b, 0, 0)),
                  pl.BlockSpec((d, tn), lambda b, i, j: (0, j)),
                  pl.BlockSpec((1, tn), lambda b, i, j: (0, j))],
        out_specs=[pl.BlockSpec((1, tm, tn), lambda b, i, j: (b, i, j)),
                   pl.BlockSpec((1, tm, d), lambda b, i, j: (b, i, 0))],
        out_shape=[jax.ShapeDtypeStruct((bsz, seq, n), out_dtype),
                   jax.ShapeDtypeStruct((bsz, seq, d), BF16)],
        compiler_params=_params("parallel", "parallel", "arbitrary"),
        name=name,
    )(x, g.reshape(1, d), scale, shift, w, extra)


def _proj(h, w, extra, out_dtype, epilogue, name):
    bsz, seq, d = h.shape
    n = w.shape[1]
    tm = _tile(seq, 1024)
    tn = _tile(n, 512, LANES)
    return pl.pallas_call(
        functools.partial(_proj_kernel, epilogue=epilogue),
        grid=(bsz, seq // tm, n // tn),
        in_specs=[pl.BlockSpec((1, tm, d), lambda b, i, j: (b, i, 0)),
                  pl.BlockSpec((d, tn), lambda b, i, j: (0, j)),
                  pl.BlockSpec((1, tn), lambda b, i, j: (0, j))],
        out_specs=pl.BlockSpec((1, tm, tn), lambda b, i, j: (b, i, j)),
        out_shape=jax.ShapeDtypeStruct((bsz, seq, n), out_dtype),
        compiler_params=_params("parallel", "parallel", "parallel"),
        name=name,
    )(h, w, extra)


def _rwkv_prep_kernel(p_ref, mu_ref, wd_ref, w0_ref, wi_ref, a0_ref, wg_ref, kk_ref, ka_ref,
                      r_o, k_o, v_o, lw_o, kk_o, a_o, g_o, carry, *, width, gate_rank):
    @pl.when(pl.program_id(1) == 0)
    def _():
        carry[...] = jnp.zeros_like(carry)

    p = p_ref[0]
    tt = p.shape[0]
    row = lax.broadcasted_iota(jnp.int32, p.shape, 0)
    prev = jnp.where(row == 0, carry[...], pltpu.roll(p, 1, axis=0))
    carry[...] = p[tt - 1:tt, :]
    pm = p + mu_ref[...] * (prev - p)
    w = width
    r = pm[:, 0:w]
    k = pm[:, w:2 * w]
    v = pm[:, 2 * w:3 * w]
    d_lo = pm[:, 3 * w:3 * w + LANES]
    a_lo = pm[:, 3 * w + LANES:3 * w + 2 * LANES]
    g_lo = pm[:, 3 * w + 2 * LANES:3 * w + 2 * LANES + gate_rank]
    w_pre = w0_ref[...] + jnp.dot(jnp.tanh(d_lo), wd_ref[...], preferred_element_type=F32,
                                  precision=HIGHEST)
    lw_o[0] = -jnp.exp(-0.5) * _sigmoid(w_pre)
    a = _sigmoid(a0_ref[...] + jnp.dot(a_lo, wi_ref[...], preferred_element_type=F32,
                                       precision=HIGHEST))
    g_o[0] = jnp.dot(_sigmoid(g_lo).astype(BF16), wg_ref[...], preferred_element_type=F32)
    r_o[0] = r
    v_o[0] = v
    a_o[0] = a
    kk_o[0] = k * kk_ref[...]
    k_o[0] = k * (1.0 + (a - 1.0) * ka_ref[...])


def _rwkv_prep(p, mu, wd, w0, wi, a0, wg, k_k, k_a, width, gate_rank):
    bsz, seq, n = p.shape
    tt = _tile(seq, 256)
    row = lambda arr: arr.reshape(1, -1)
    full = lambda shape: pl.BlockSpec(shape, lambda b, i: (0,) * len(shape))
    out_sds = jax.ShapeDtypeStruct((bsz, seq, width), F32)
    out_spec = pl.BlockSpec((1, tt, width), lambda b, i: (b, i, 0))
    return pl.pallas_call(
        functools.partial(_rwkv_prep_kernel, width=width, gate_rank=gate_rank),
        grid=(bsz, seq // tt),
        in_specs=[pl.BlockSpec((1, tt, n), lambda b, i: (b, i, 0)),
                  full((1, n)), full(wd.shape), full((1, width)), full(wi.shape), full((1, width)),
                  full(wg.shape), full((1, width)), full((1, width))],
        out_specs=[out_spec] * 7,
        out_shape=[out_sds] * 7,
        scratch_shapes=[pltpu.VMEM((1, n), F32)],
        compiler_params=_params("parallel", "arbitrary"),
        name="rwkv_prep",
    )(p, row(mu), wd, row(w0), wi, row(a0), wg, row(k_k), row(k_a))


def _rwkv_scan_kernel(r_ref, k_ref, v_ref, lw_ref, kk_ref, a_ref, g_ref, rk_ref, gg_ref, gb_ref,
                      o_ref, s_scr, *, n_chunks, head_dim):
    L = RWKV_CHUNK
    L2 = 2 * L

    @pl.when(pl.program_id(2) == 0)
    def _():
        s_scr[...] = jnp.zeros_like(s_scr)

    lane = lax.broadcasted_iota(jnp.int32, (1, LANES), 1)
    m0 = (lane < head_dim).astype(F32)
    m1 = 1.0 - m0
    r2 = lax.broadcasted_iota(jnp.int32, (L2, L2), 0)
    c2 = lax.broadcasted_iota(jnp.int32, (L2, L2), 1)
    same = (r2 < L) == (c2 < L)
    strict = jnp.logical_and(same, c2 < r2)
    incl = jnp.logical_and(same, c2 <= r2)
    eye = (r2 == c2).astype(F32)
    rowi = lax.broadcasted_iota(jnp.int32, (L, LANES), 0)
    n_pairs = r_ref.shape[2] // LANES

    def stack_f32(x):
        return jnp.concatenate([x * m0, x * m1], axis=0)

    def stack(x):
        return stack_f32(x).astype(BF16)

    def head_sum(x):
        s0 = jnp.sum(x * m0, axis=-1, keepdims=True)
        s1 = jnp.sum(x * m1, axis=-1, keepdims=True)
        return s0 * m0 + s1 * m1

    def nt(a, b):
        return lax.dot_general(a, b, (((1,), (1,)), ((), ())), preferred_element_type=F32)

    def mm(a, b):
        return jnp.dot(a.astype(BF16), b.astype(BF16), preferred_element_type=F32)

    blk_masks = []
    size = 8
    while size <= L:
        blk_masks.append(jnp.bitwise_xor(r2, c2) < size)
        size *= 2
    off_masks = [jnp.logical_and(hi, jnp.logical_not(lo))
                 for lo, hi in zip(blk_masks[:-1], blk_masks[1:])]
    chunks = range(n_chunks)

    ctx = []
    for c, p in [(c, p) for c in chunks for p in range(n_pairs)]:
        sl = pl.ds(c * L, L)
        ln = pl.ds(p * LANES, LANES)
        r = r_ref[0, sl, ln]
        k = k_ref[0, sl, ln]
        v = v_ref[0, sl, ln]
        lw = lw_ref[0, sl, ln]
        kk = kk_ref[0, sl, ln]
        aic = a_ref[0, sl, ln]
        rk = rk_ref[:, ln]
        cum = lw
        sh = 1
        while sh < L:
            cum = cum + jnp.where(rowi >= sh, pltpu.roll(cum, sh, axis=0), 0.0)
            sh *= 2
        tot = cum[L - 1:L, :]
        w_rem = jnp.exp(tot - cum)
        w_inv = jnp.exp(-cum)
        kk = kk / jnp.maximum(jnp.sqrt(head_sum(kk * kk)), 1e-12)
        b_vec = kk * aic
        ar = jnp.concatenate([stack(-kk * jnp.exp(cum - lw)), stack(r * jnp.exp(cum))], axis=0)
        bk = jnp.concatenate([stack(b_vec * w_inv), stack(k * w_inv)], axis=0)
        v_st = stack(v)
        b_rem_t = stack_f32(b_vec * w_rem).T.astype(BF16)
        k_rem_t = stack_f32(k * w_rem).T.astype(BF16)
        decay = jnp.broadcast_to(jnp.exp(tot), (LANES, LANES)).T
        ctx.append(dict(sl=sl, ln=ln, p=p, ar=ar, bk=bk, v_st=v_st, decay=decay, b_rem_t=b_rem_t,
                        k_rem_t=k_rem_t, bonus=head_sum(r * k * rk) * v))

    for x in ctx:
        big = nt(x["ar"], x["bk"])
        x["n_ab"] = jnp.where(strict, big[0:L2, 0:L2], 0.0)
        m_ak = jnp.where(strict, big[0:L2, L2:2 * L2], 0.0)
        m_rk = jnp.where(incl, big[L2:2 * L2, L2:2 * L2], 0.0)
        x["m_rb"] = jnp.where(incl, big[L2:2 * L2, 0:L2], 0.0).astype(BF16)
        x["m_akrk"] = jnp.concatenate([m_ak, m_rk], axis=0).astype(BF16)
    for x in ctx:
        x["m_v"] = mm(x["m_akrk"], x["v_st"])
        x["kv"] = mm(x["k_rem_t"], x["v_st"])

    for x in ctx:
        x["d8"] = jnp.where(blk_masks[0], x["n_ab"], 0.0)
        x["t"] = eye + x["d8"]
    for x in ctx:
        x["pw"] = mm(x["d8"], x["d8"])
    for x in ctx:
        x["t"] = x["t"] + mm(x["t"], x["pw"])
    for x in ctx:
        x["pw"] = mm(x["pw"], x["pw"])
    for x in ctx:
        x["t"] = x["t"] + mm(x["t"], x["pw"])
    for off in off_masks:
        for x in ctx:
            x["nt"] = mm(jnp.where(off, x["n_ab"], 0.0), x["t"])
        for x in ctx:
            x["t"] = x["t"] + mm(x["t"], x["nt"])

    for x in ctx:
        rhs = jnp.concatenate([x["ar"][0:L2], x["m_v"][0:L2].astype(BF16)], axis=1)
        x["ta_uv"] = mm(x["t"], rhs).astype(BF16)
    for x in ctx:
        w_q = mm(x["b_rem_t"], x["ta_uv"])
        g_y = mm(x["m_rb"], x["ta_uv"])
        x["q"] = w_q[:, LANES:] + x["kv"]
        x["y_loc"] = g_y[:, LANES:] + x["m_v"][L2:2 * L2]
        g = g_y[:, :LANES] + x["ar"][L2:2 * L2].astype(F32)
        x["wg"] = jnp.concatenate([w_q[:, :LANES], g], axis=0).astype(BF16)

    s = [s_scr[p] for p in range(n_pairs)]
    for x in ctx:
        p = x["p"]
        ws = mm(x["wg"], s[p])
        x["y_st"] = ws[L2:2 * L2] + x["y_loc"]
        s[p] = x["decay"] * s[p] + ws[0:L2] + x["q"]
    for p in range(n_pairs):
        s_scr[p] = s[p]

    inv_n = 1.0 / head_dim
    for x in ctx:
        y_st = x["y_st"]
        y = y_st[0:L] + y_st[L:L2]
        mean = head_sum(y) * inv_n
        yc = y - mean
        var = head_sum(yc * yc) * inv_n
        y_gn = yc * lax.rsqrt(var + GN_EPS) * gg_ref[:, x["ln"]] + gb_ref[:, x["ln"]]
        o_ref[0, x["sl"], x["ln"]] = ((y_gn + x["bonus"]) * g_ref[0, x["sl"], x["ln"]]).astype(o_ref.dtype)


def _rwkv_scan(r, k, v, lw, kk, a, g, r_k, gn_g, gn_b, head_dim):
    bsz, seq, width = r.shape
    assert 2 * head_dim == LANES and width % LANES == 0
    tc = _tile(seq, 8 * RWKV_CHUNK, RWKV_CHUNK)
    pw = _tile(width, RWKV_PAIRS * LANES, LANES)
    tok = pl.BlockSpec((1, tc, pw), lambda b, h, i: (b, i, h))
    par = pl.BlockSpec((1, pw), lambda b, h, i: (0, h))
    row = lambda arr: arr.reshape(1, width)
    return pl.pallas_call(
        functools.partial(_rwkv_scan_kernel, n_chunks=tc // RWKV_CHUNK, head_dim=head_dim),
        grid=(bsz, width // pw, seq // tc),
        in_specs=[tok] * 7 + [par] * 3,
        out_specs=tok,
        out_shape=jax.ShapeDtypeStruct((bsz, seq, width), BF16),
        scratch_shapes=[pltpu.VMEM((pw // LANES, LANES, LANES), F32)],
        compiler_params=_params("parallel", "parallel", "arbitrary"),
        name="rwkv_scan",
    )(r, k, v, lw, kk, a, g, row(r_k), row(gn_g), row(gn_b))


def _logf_cumsum_kernel(f_ref, b_ref, o_ref, carry):
    @pl.when(pl.program_id(1) == 0)
    def _():
        carry[...] = jnp.zeros_like(carry)

    z = f_ref[0] + b_ref[...]
    lf = jnp.minimum(z, 0.0) - jnp.log(1.0 + jnp.exp(-jnp.abs(z)))
    ts = z.shape[1]
    ri = lax.broadcasted_iota(jnp.int32, (LANES, LANES), 0)
    ci = lax.broadcasted_iota(jnp.int32, (LANES, LANES), 1)
    upper = (ri <= ci).astype(F32)
    run = carry[...]
    for j in range(ts // LANES):
        blk = jnp.dot(lf[:, j * LANES:(j + 1) * LANES], upper, preferred_element_type=F32,
                      precision=HIGHEST) + run
        o_ref[0, :, j * LANES:(j + 1) * LANES] = blk
        run = blk[:, LANES - 1:LANES]
    carry[...] = run


def _logf_cumsum(f_t, b_f):
    bsz, nh, seq = f_t.shape
    ts = _tile(seq, 2048, LANES)
    return pl.pallas_call(
        _logf_cumsum_kernel,
        grid=(bsz, seq // ts),
        in_specs=[pl.BlockSpec((1, nh, ts), lambda b, i: (b, 0, i)),
                  pl.BlockSpec((nh, 1), lambda b, i: (0, 0))],
        out_specs=pl.BlockSpec((1, nh, ts), lambda b, i: (b, 0, i)),
        out_shape=jax.ShapeDtypeStruct((bsz, nh, seq), F32),
        scratch_shapes=[pltpu.VMEM((nh, 1), F32)],
        compiler_params=_params("parallel", "arbitrary"),
        name="fox_logf_cumsum",
    )(f_t, b_f.reshape(nh, 1))


def _fox_kernel(qi_ref, ki_ref, q_ref, k_ref, v_ref, ck_ref, og_ref, o_ref,
                m_scr, l_scr, acc_scr, *, rows):
    p = pl.program_id(2)
    qi = qi_ref[p]
    ki = ki_ref[p]
    tq = q_ref.shape[1]
    tk = k_ref.shape[1]

    @pl.when(ki == 0)
    def _():
        m_scr[...] = jnp.full_like(m_scr, NEG_BIG)
        l_scr[...] = jnp.zeros_like(l_scr)
        acc_scr[...] = jnp.zeros_like(acc_scr)

    n_rc = tq // rows

    n_kq = tq // tk

    def step(diag):
        k = k_ref[0]
        v = v_ref[0]
        ck = ck_ref[0, 0] * LOG2E
        col0 = 0 if diag is None else diag * tk
        live = [rc for rc in range(n_rc) if diag is None or col0 < (rc + 1) * rows]

        def n_cols(rc):
            return tk if diag is None else min(tk, (rc + 1) * rows - col0)

        def logits(rc):
            kc = n_cols(rc)
            s = lax.dot_general(q_ref[0, pl.ds(rc * rows, rows), :], k[:kc], (((1,), (1,)), ((), ())),
                                preferred_element_type=F32) - ck[:, :kc]
            if diag is not None and col0 + kc - 1 > rc * rows:
                ri = lax.broadcasted_iota(jnp.int32, s.shape, 0) + rc * rows
                ci = lax.broadcasted_iota(jnp.int32, s.shape, 1) + col0
                s = jnp.where(ci <= ri, s, NEG_BIG)
            return s

        old = {rc: (m_scr[pl.ds(rc * rows, rows), :], l_scr[pl.ds(rc * rows, rows), :],
                    acc_scr[pl.ds(rc * rows, rows), :]) for rc in live}
        new = {}
        s_next = logits(live[0])
        for i, rc in enumerate(live):
            s = s_next
            if i + 1 < len(live):
                s_next = logits(live[i + 1])
            m_prev, l_prev, acc_prev = old[rc]
            tiles = [s[:, j * LANES:(j + 1) * LANES] for j in range(n_cols(rc) // LANES)]
            m_new = jnp.maximum(m_prev, jnp.max(functools.reduce(jnp.maximum, tiles),
                                                axis=-1, keepdims=True))
            alpha = jnp.exp2(m_prev - m_new)
            p_tiles = [jnp.exp2(t - m_new) for t in tiles]
            l_new = alpha * l_prev + functools.reduce(jnp.add, p_tiles)
            pr = jnp.concatenate([t.astype(BF16) for t in p_tiles], axis=1)
            acc_new = alpha * acc_prev + jnp.dot(pr, v[:n_cols(rc)], preferred_element_type=F32)
            new[rc] = (m_new, l_new, acc_new)
        for rc in live:
            rs = pl.ds(rc * rows, rows)
            m_scr[rs, :], l_scr[rs, :], acc_scr[rs, :] = new[rc]

    @pl.when(ki < qi * n_kq)
    def _():
        step(None)

    for diag in range(n_kq):
        @pl.when(ki == qi * n_kq + diag)
        def _(diag=diag):
            step(diag)

    @pl.when(ki == qi * n_kq + n_kq - 1)
    def _():
        o = acc_scr[...] / jnp.sum(l_scr[...], axis=-1, keepdims=True)
        o_ref[0] = (o * _sigmoid(og_ref[0])).astype(o_ref.dtype)


def _fox_attention(qk, v, cum, ogf, n_heads):
    bsz, seq, _ = v.shape
    tk = _tile(seq, FOX_TK, LANES)
    tq = _tile(seq, FOX_TQ, tk)
    qi_list, ki_list = [], []
    for qi in range(seq // tq):
        for ki in range((qi + 1) * (tq // tk)):
            qi_list.append(qi)
            ki_list.append(ki)
    qi_arr = jnp.asarray(qi_list, jnp.int32)
    ki_arr = jnp.asarray(ki_list, jnp.int32)
    cum_row = cum[:, :, None, :]
    nh = n_heads
    grid_spec = pltpu.PrefetchScalarGridSpec(
        num_scalar_prefetch=2,
        grid=(bsz, nh, len(qi_list)),
        in_specs=[pl.BlockSpec((1, tq, LANES), lambda b, h, p, qi, ki: (b, qi[p], h)),
                  pl.BlockSpec((1, tk, LANES), lambda b, h, p, qi, ki: (b, ki[p], nh + h)),
                  pl.BlockSpec((1, tk, LANES), lambda b, h, p, qi, ki: (b, ki[p], h)),
                  pl.BlockSpec((1, 1, 1, tk), lambda b, h, p, qi, ki: (b, h, 0, ki[p])),
                  pl.BlockSpec((1, tq, LANES), lambda b, h, p, qi, ki: (b, qi[p], h))],
        out_specs=pl.BlockSpec((1, tq, LANES), lambda b, h, p, qi, ki: (b, qi[p], h)),
        scratch_shapes=[pltpu.VMEM((tq, LANES), F32), pltpu.VMEM((tq, LANES), F32),
                        pltpu.VMEM((tq, LANES), F32)],
    )
    return pl.pallas_call(
        functools.partial(_fox_kernel, rows=_tile(tq, FOX_ROWS, LANES)),
        grid_spec=grid_spec,
        out_shape=jax.ShapeDtypeStruct((bsz, seq, nh * LANES), BF16),
        compiler_params=_params("parallel", "parallel", "arbitrary"),
        name="fox_attention",
    )(qi_arr, ki_arr, qk, qk, v, cum_row, ogf)


def _merge_kernel(ya_ref, yb_ref, ga_ref, gb_ref, wa_ref, wb_ref, o_ref):
    pa = jnp.dot(ya_ref[0], wa_ref[...], preferred_element_type=F32)
    pb = jnp.dot(yb_ref[0], wb_ref[...], preferred_element_type=F32)
    o_ref[0] = (ga_ref[0].astype(F32) * pa + gb_ref[0].astype(F32) * pb).astype(o_ref.dtype)


def _merge(y_a, y_b, gates, w_a, w_b):
    bsz, seq, wa = y_a.shape
    wb = y_b.shape[2]
    d = w_a.shape[1]
    tm = _tile(seq, 512)
    return pl.pallas_call(
        _merge_kernel,
        grid=(bsz, seq // tm),
        in_specs=[pl.BlockSpec((1, tm, wa), lambda b, i: (b, i, 0)),
                  pl.BlockSpec((1, tm, wb), lambda b, i: (b, i, 0)),
                  pl.BlockSpec((1, tm, d), lambda b, i: (b, i, 0)),
                  pl.BlockSpec((1, tm, d), lambda b, i: (b, i, 1)),
                  pl.BlockSpec((wa, d), lambda b, i: (0, 0)),
                  pl.BlockSpec((wb, d), lambda b, i: (0, 0))],
        out_specs=pl.BlockSpec((1, tm, d), lambda b, i: (b, i, 0)),
        out_shape=jax.ShapeDtypeStruct((bsz, seq, d), BF16),
        compiler_params=_params("parallel", "parallel"),
        name="branch_merge",
    )(y_a, y_b, gates, gates, w_a, w_b)


def _route_kernel(m_ref, x_ref, g1_ref, w_ref, n2_ref, sc_ref, sh_ref, wr_ref, br_ref,
                  x1_ref, route_ref, cnt_ref, base, *, n_experts):
    @pl.when(jnp.logical_and(pl.program_id(0) == 0, pl.program_id(1) == 0))
    def _():
        base[...] = jnp.zeros_like(base)

    x1 = x_ref[0] + g1_ref[0] * jnp.dot(m_ref[0], w_ref[...], preferred_element_type=F32)
    x1_ref[0] = x1
    h = _rms_mod(x1, n2_ref[...], sc_ref[0], sh_ref[0])
    h_hi = h.astype(BF16)
    h_lo = (h - h_hi.astype(F32)).astype(BF16)
    logits = jnp.dot(jnp.concatenate([h_hi, h_hi, h_lo], axis=1), wr_ref[...],
                     preferred_element_type=F32) + br_ref[...]
    tm = logits.shape[0]
    lane_i = lax.broadcasted_iota(jnp.int32, (tm, LANES), 1)
    lane = lane_i.astype(F32)
    vals = jnp.where(lane_i < n_experts, logits, -jnp.inf)
    top_v, top_i, hot = [], [], []
    for _ in range(TOP_K):
        mx = jnp.max(vals, axis=-1, keepdims=True)
        ix = jnp.min(jnp.where(vals == mx, lane, float(LANES)), axis=-1, keepdims=True)
        sel = lane == ix
        vals = jnp.where(sel, -jnp.inf, vals)
        top_v.append(mx)
        top_i.append(ix)
        hot.append(sel.astype(F32))
    ex = [jnp.exp(tv - top_v[0]) for tv in top_v]
    den = ex[0] + ex[1] + ex[2] + ex[3]
    cnt = hot[0] + hot[1] + hot[2] + hot[3]
    ri = lax.broadcasted_iota(jnp.int32, (tm, tm), 0)
    ci = lax.broadcasted_iota(jnp.int32, (tm, tm), 1)
    before = jnp.dot((ci < ri).astype(BF16), cnt.astype(BF16), preferred_element_type=F32)
    before = before + base[...]
    out = jnp.zeros((tm, LANES), F32)
    for kk in range(TOP_K):
        rank = jnp.sum(hot[kk] * before, axis=-1, keepdims=True)
        out = jnp.where(lane_i == kk, top_i[kk], out)
        out = jnp.where(lane_i == TOP_K + kk, ex[kk] / den, out)
        out = jnp.where(lane_i == 2 * TOP_K + kk, rank, out)
    route_ref[0] = out
    new_base = base[...] + jnp.sum(cnt, axis=0, keepdims=True)
    base[...] = new_base
    cnt_ref[...] = jnp.broadcast_to(new_base, cnt_ref.shape)


def _route(merged, x, gate1, w_out, norm2_g, scale2, shift2, w_router, b_router):
    bsz, seq, d = x.shape
    n_experts = w_router.shape[1]
    tm = _tile(seq, 512)
    wr = jnp.zeros((d, LANES), F32).at[:, :n_experts].set(w_router)
    wr_hi = wr.astype(BF16)
    wr_lo = (wr - wr_hi.astype(F32)).astype(BF16)
    wr = jnp.concatenate([wr_hi, wr_lo, wr_hi], axis=0)
    br = jnp.zeros((1, LANES), F32).at[0, :n_experts].set(b_router)
    mod = pl.BlockSpec((1, 1, d), lambda b, i: (b, 0, 0))
    tok = pl.BlockSpec((1, tm, d), lambda b, i: (b, i, 0))
    return pl.pallas_call(
        functools.partial(_route_kernel, n_experts=n_experts),
        grid=(bsz, seq // tm),
        in_specs=[tok, tok, mod,
                  pl.BlockSpec((d, d), lambda b, i: (0, 0)),
                  pl.BlockSpec((1, d), lambda b, i: (0, 0)), mod, mod,
                  pl.BlockSpec((3 * d, LANES), lambda b, i: (0, 0)),
                  pl.BlockSpec((1, LANES), lambda b, i: (0, 0))],
        out_specs=[tok,
                   pl.BlockSpec((1, tm, LANES), lambda b, i: (b, i, 0)),
                   pl.BlockSpec((8, LANES), lambda b, i: (0, 0))],
        out_shape=[jax.ShapeDtypeStruct((bsz, seq, d), F32),
                   jax.ShapeDtypeStruct((bsz, seq, LANES), F32),
                   jax.ShapeDtypeStruct((8, LANES), F32)],
        scratch_shapes=[pltpu.VMEM((1, LANES), F32)],
        compiler_params=_params("arbitrary", "arbitrary"),
        name="residual_router",
    )(merged, x, gate1, w_out, norm2_g.reshape(1, d), scale2, shift2, wr, br)


def _dispatch_kernel(slot_ref, x_ref, n2_ref, sc_ref, sh_ref, xs_in_ref, xs_ref, hbuf, sem):
    del xs_in_ref
    i = pl.program_id(0)
    tm = hbuf.shape[0]
    hbuf[...] = _rms_mod(x_ref[...], n2_ref[...], sc_ref[...], sh_ref[...])

    def row_copy(r, s):
        return pltpu.make_async_copy(hbuf.at[pl.ds(r, 1)], xs_ref.at[pl.ds(s, 1)], sem)

    def issue(r, carry):
        for kk in range(TOP_K):
            row_copy(r, slot_ref[(i * tm + r) * TOP_K + kk]).start()
        return carry

    lax.fori_loop(0, tm, issue, 0)

    def drain(r, carry):
        for kk in range(TOP_K):
            row_copy(0, 0).wait()
        return carry

    lax.fori_loop(0, tm, drain, 0)


def _dispatch(x1_b, slots_b, norm2_g, scale2_b, shift2_b, xs):
    seq, d = x1_b.shape
    tm = _tile(seq, 256)
    grid_spec = pltpu.PrefetchScalarGridSpec(
        num_scalar_prefetch=1,
        grid=(seq // tm,),
        in_specs=[pl.BlockSpec((tm, d), lambda i, s: (i, 0)),
                  pl.BlockSpec((1, d), lambda i, s: (0, 0)),
                  pl.BlockSpec((1, d), lambda i, s: (0, 0)),
                  pl.BlockSpec((1, d), lambda i, s: (0, 0)),
                  pl.BlockSpec(memory_space=pl.ANY)],
        out_specs=pl.BlockSpec(memory_space=pl.ANY),
        scratch_shapes=[pltpu.VMEM((tm, d), F32), pltpu.SemaphoreType.DMA(())],
    )
    return pl.pallas_call(
        _dispatch_kernel,
        grid_spec=grid_spec,
        out_shape=jax.ShapeDtypeStruct(xs.shape, xs.dtype),
        input_output_aliases={5: 0},
        compiler_params=_params("arbitrary"),
        name="moe_dispatch",
    )(slots_b, x1_b, norm2_g.reshape(1, d), scale2_b, shift2_b, xs)


def _expert_kernel(be_ref, na_ref, x_ref, wg_ref, wl_ref, bg_ref, bl_ref, wd_ref, bd_ref,
                   o_ref, acc, *, n_f):
    b = pl.program_id(0)
    f = pl.program_id(1)

    @pl.when(b < na_ref[0])
    def _():
        x = x_ref[...].astype(BF16)
        gate = jnp.dot(x, wg_ref[0], preferred_element_type=F32) + bg_ref[0]
        lin = jnp.dot(x, wl_ref[0], preferred_element_type=F32) + bl_ref[0]
        gate = jnp.minimum(gate, SWIGLU_LIMIT)
        lin = jnp.clip(lin, -SWIGLU_LIMIT, SWIGLU_LIMIT)
        act = gate * _sigmoid(SWIGLU_ALPHA * gate) * (lin + 1.0)
        contrib = jnp.dot(act.astype(BF16), wd_ref[0], preferred_element_type=F32)
        if n_f == 1:
            o_ref[...] = contrib + bd_ref[0]
        else:
            @pl.when(f == 0)
            def _():
                acc[...] = contrib

            @pl.when(f > 0)
            def _():
                acc[...] += contrib

            @pl.when(f == n_f - 1)
            def _():
                o_ref[...] = acc[...] + bd_ref[0]

    @pl.when(jnp.logical_and(b >= na_ref[0], f == n_f - 1))
    def _():
        o_ref[...] = jnp.zeros_like(o_ref)


def _experts(xs, blk_e, n_active, w_gu, b_gu, w_dn, b_dn):
    n_slots, d = xs.shape
    n_e, _, two_ff = w_gu.shape
    d_ff = two_ff // 2
    bm = EXPERT_ROWS
    n_blocks = n_slots // bm
    tf = _tile(d_ff, EXPERT_FF_TILE, LANES)
    n_f = d_ff // tf

    def blk(b, na):
        return jnp.minimum(b, na[0] - 1)

    def ff(b, f, na):
        return jnp.where(b < na[0], f, n_f - 1)

    wmode = dict(pipeline_mode=pl.Buffered(1)) if n_f == 1 else {}
    grid_spec = pltpu.PrefetchScalarGridSpec(
        num_scalar_prefetch=2,
        grid=(n_blocks, n_f),
        in_specs=[pl.BlockSpec((bm, d), lambda b, f, be, na: (blk(b, na), 0)),
                  pl.BlockSpec((1, d, tf), lambda b, f, be, na: (be[blk(b, na)], 0, ff(b, f, na)), **wmode),
                  pl.BlockSpec((1, d, tf), lambda b, f, be, na: (be[blk(b, na)], 0, n_f + ff(b, f, na)),
                               **wmode),
                  pl.BlockSpec((1, 1, tf), lambda b, f, be, na: (be[blk(b, na)], 0, ff(b, f, na))),
                  pl.BlockSpec((1, 1, tf), lambda b, f, be, na: (be[blk(b, na)], 0, n_f + ff(b, f, na))),
                  pl.BlockSpec((1, tf, d), lambda b, f, be, na: (be[blk(b, na)], ff(b, f, na), 0), **wmode),
                  pl.BlockSpec((1, 1, d), lambda b, f, be, na: (be[blk(b, na)], 0, 0))],
        out_specs=pl.BlockSpec((bm, d), lambda b, f, be, na: (b, 0)),
        scratch_shapes=[pltpu.VMEM((bm, d) if n_f > 1 else (8, LANES), F32)],
    )
    return pl.pallas_call(
        functools.partial(_expert_kernel, n_f=n_f),
        grid_spec=grid_spec,
        out_shape=jax.ShapeDtypeStruct((n_slots, d), F32),
        compiler_params=_params("arbitrary", "arbitrary"),
        name="moe_experts",
    )(blk_e, n_active, xs, w_gu, w_gu, b_gu.reshape(n_e, 1, two_ff), b_gu.reshape(n_e, 1, two_ff),
      w_dn, b_dn.reshape(n_e, 1, d))


def _combine_kernel(slot_ref, x_ref, route_ref, g2_ref, gf_ref, ys_ref, o_ref, buf, sem):
    i = pl.program_id(0)
    tm = x_ref.shape[0]

    def row_copy(kk, r, s):
        return pltpu.make_async_copy(ys_ref.at[pl.ds(s, 1)], buf.at[kk, pl.ds(r, 1)], sem)

    def issue(r, carry):
        for kk in range(TOP_K):
            row_copy(kk, r, slot_ref[(i * tm + r) * TOP_K + kk]).start()
        return carry

    lax.fori_loop(0, tm, issue, 0)

    def drain(r, carry):
        for kk in range(TOP_K):
            row_copy(0, 0, 0).wait()
        return carry

    lax.fori_loop(0, tm, drain, 0)

    route = route_ref[...]
    y = jnp.zeros(x_ref.shape, F32)
    for kk in range(TOP_K):
        y = y + buf[kk] * route[:, TOP_K + kk:TOP_K + kk + 1]
    x2 = x_ref[...] + g2_ref[...] * y
    ms = jnp.mean(x2 * x2, axis=-1, keepdims=True)
    o_ref[...] = x2 * lax.rsqrt(ms + NORM_EPS) * gf_ref[...]


def _combine(x1_b, route_b, slots_b, gate2_b, norm_final_g, ys):
    seq, d = x1_b.shape
    tm = _tile(seq, 256)
    grid_spec = pltpu.PrefetchScalarGridSpec(
        num_scalar_prefetch=1,
        grid=(seq // tm,),
        in_specs=[pl.BlockSpec((tm, d), lambda i, s: (i, 0)),
                  pl.BlockSpec((tm, LANES), lambda i, s: (i, 0)),
                  pl.BlockSpec((1, d), lambda i, s: (0, 0)),
                  pl.BlockSpec((1, d), lambda i, s: (0, 0)),
                  pl.BlockSpec(memory_space=pl.ANY)],
        out_specs=pl.BlockSpec((tm, d), lambda i, s: (i, 0)),
        scratch_shapes=[pltpu.VMEM((TOP_K, tm, d), F32), pltpu.SemaphoreType.DMA(())],
    )
    return pl.pallas_call(
        _combine_kernel,
        grid_spec=grid_spec,
        out_shape=jax.ShapeDtypeStruct((seq, d), F32),
        compiler_params=_params("arbitrary"),
        name="moe_combine",
    )(slots_b, x1_b, route_b, gate2_b, norm_final_g.reshape(1, d), ys)


def _pad_cols(w, n):
    return jnp.pad(w, ((0, 0), (0, n - w.shape[1])))


def kernel(x, c, w_ada, b_ada, norm1_g, w_in, rwkv_mu, w_decay_up, decay_w0, w_iclr_up, iclr_a0, w_gate_up_rwkv, rwkv_k_k, rwkv_k_a, rwkv_r_k, rwkv_gn_g, rwkv_gn_b, w_out_a, fox_b_f, fox_q_norm, fox_k_norm, w_out_b, w_out, norm2_g, w_router, b_router, w_expert_gu, b_expert_gu, w_expert_down, b_expert_down, norm_final_g):
    bsz, seq, d = x.shape
    rw_heads, rw_hd = rwkv_r_k.shape
    rw = rw_heads * rw_hd
    dr, ir, gr = w_decay_up.shape[0], w_iclr_up.shape[0], w_gate_up_rwkv.shape[0]
    fh = fox_b_f.shape[0]
    fw = w_out_b.shape[0]
    fhd = fw // fh
    assert fhd == LANES and dr <= LANES and ir <= LANES and gr % LANES == 0
    n_experts = w_router.shape[1]

    mod = _ada(c, w_ada, b_ada)
    shift1, scale1, gate1, shift2, scale2, gate2 = (m[:, None, :] for m in jnp.split(mod, 6, axis=-1))

    o_r = 0
    o_f = 3 * rw + dr + ir + gr
    o_g = o_f + 3 * fw + fh + fw
    col = lambda a, n: w_in[:, a:a + n]
    w_rwkv = jnp.concatenate([col(0, 3 * rw), _pad_cols(col(3 * rw, dr), LANES),
                              _pad_cols(col(3 * rw + dr, ir), LANES), col(3 * rw + dr + ir, gr)],
                             axis=1).astype(BF16)
    mu = jnp.concatenate([rwkv_mu[:3 * rw], jnp.pad(rwkv_mu[3 * rw:3 * rw + dr], (0, LANES - dr)),
                          jnp.pad(rwkv_mu[3 * rw + dr:3 * rw + dr + ir], (0, LANES - ir)),
                          rwkv_mu[3 * rw + dr + ir:]])
    w_qk = col(o_f, 2 * fw).astype(BF16)
    w_v = col(o_f + 2 * fw, fw).astype(BF16)
    w_ogf = jnp.concatenate([col(o_f + 3 * fw + fh, fw), _pad_cols(col(o_f + 3 * fw, fh), LANES)],
                            axis=1).astype(BF16)
    w_gates = col(o_g, 2 * d).astype(BF16)
    qk_gain = jnp.concatenate([jnp.tile(fox_q_norm * (fhd ** -0.5 * LOG2E), fh), jnp.tile(fox_k_norm, fh)])

    def row(extra, n):
        return (jnp.zeros((n,), F32) if extra is None else extra).reshape(1, n)

    gates, h1 = _inproj(x, norm1_g, scale1, shift1, w_gates, row(None, w_gates.shape[1]), BF16,
                        _ep_sigmoid, "inproj_merge_gates")

    def proj(w, extra, dtype, ep, name):
        return _proj(h1, w, row(extra, w.shape[1]), dtype, ep, name)

    p_rwkv = proj(w_rwkv, None, F32, _ep_identity, "inproj_rwkv")
    qk = proj(w_qk, qk_gain, BF16, _ep_headnorm, "inproj_fox_qk")
    v_fox = proj(w_v, None, BF16, _ep_identity, "inproj_fox_v")
    ogf = proj(w_ogf, None, F32, _ep_identity, "inproj_fox_gate_forget")

    wd = jnp.pad(w_decay_up, ((0, LANES - dr), (0, 0)))
    wi = jnp.pad(w_iclr_up, ((0, LANES - ir), (0, 0)))
    r, k2, v, lw, kk, aic, g = _rwkv_prep(p_rwkv, mu, wd, decay_w0, wi, iclr_a0,
                                          w_gate_up_rwkv.astype(BF16), rwkv_k_k, rwkv_k_a, rw, gr)
    y_a = _rwkv_scan(r, k2, v, lw, kk, aic, g, rwkv_r_k, rwkv_gn_g, rwkv_gn_b, rw_hd)

    f_t = jnp.transpose(ogf[:, :, fw:fw + fh], (0, 2, 1))
    cum = _logf_cumsum(f_t, fox_b_f)
    y_b = _fox_attention(qk, v_fox, cum, ogf, fh)

    merged = _merge(y_a, y_b, gates, w_out_a.astype(BF16), w_out_b.astype(BF16))
    x1, route, counts = _route(merged, x, gate1, w_out.astype(BF16), norm2_g, scale2, shift2,
                               w_router, b_router)

    n_tok = bsz * seq
    n_assign = n_tok * TOP_K
    bm = EXPERT_ROWS
    n_blocks = -(-n_assign // bm) + n_experts
    cnt = counts[0, :n_experts].astype(jnp.int32)
    padded = (cnt + bm - 1) // bm * bm
    pad_end = jnp.cumsum(padded)
    pad_start = pad_end - padded
    top_i = route[:, :, 0:TOP_K].astype(jnp.int32)
    rank = route[:, :, 2 * TOP_K:3 * TOP_K].astype(jnp.int32)
    e_ids = jnp.arange(n_experts, dtype=jnp.int32)
    start_of = jnp.sum(jnp.where(top_i[..., None] == e_ids, pad_start, 0), axis=-1)
    slots = (start_of + rank).reshape(bsz, seq * TOP_K)
    blk_first = jnp.arange(n_blocks, dtype=jnp.int32) * bm
    blk_e = jnp.minimum(jnp.sum((pad_end[None, :] <= blk_first[:, None]).astype(jnp.int32), axis=1),
                        n_experts - 1)
    n_active = (pad_end[-1:] // bm).astype(jnp.int32)

    xs = jnp.zeros((n_blocks * bm, d), F32)
    for b in range(bsz):
        xs = _dispatch(x1[b], slots[b], norm2_g, scale2[b], shift2[b], xs)
    ys = _experts(xs, blk_e, n_active, w_expert_gu.astype(BF16), b_expert_gu,
                  w_expert_down.astype(BF16), b_expert_down)
    outs = [_combine(x1[b], route[b], slots[b], gate2[b], norm_final_g, ys) for b in range(bsz)]
    return jnp.stack(outs, axis=0)
```

```python
import functools

import jax
import jax.numpy as jnp
from jax import lax
from jax.experimental import pallas as pl
from jax.experimental.pallas import tpu as pltpu

F32 = jnp.float32
BF16 = jnp.bfloat16
HIGHEST = lax.Precision.HIGHEST

TOP_K = 4
NORM_EPS = 1e-6
GN_EPS = 64e-5
SWIGLU_LIMIT = 7.0
SWIGLU_ALPHA = 1.702
LANES = 128
RWKV_CHUNK = 64
RWKV_PAIRS = 4
VMEM_LIMIT_BYTES = 56 * 1024 * 1024
EXPERT_ROWS = 512
EXPERT_FF_TILE = 1024
NEG_BIG = -1e30
LOG2E = 1.4426950408889634
FOX_ROWS = 256
FOX_TQ = 2048
FOX_TK = 1024


def _tile(n, pref, mult=8):
    t = min(pref, n)
    t -= t % mult
    while t >= mult:
        if n % t == 0:
            return t
        t -= mult
    return n


def _params(*sem):
    return pltpu.CompilerParams(dimension_semantics=sem, vmem_limit_bytes=VMEM_LIMIT_BYTES)


def _sigmoid(x):
    return 1.0 / (1.0 + jnp.exp(-x))


def _rms_mod(x, g, scale, shift):
    ms = jnp.mean(x * x, axis=-1, keepdims=True)
    return x * lax.rsqrt(ms + NORM_EPS) * g * (1.0 + scale) + shift


def _ada_kernel(c_ref, w_ref, b_ref, o_ref):
    c = c_ref[...]
    s = c * _sigmoid(c)
    o_ref[...] = jnp.dot(s, w_ref[...], preferred_element_type=F32, precision=HIGHEST) + b_ref[...]


def _ada(c, w_ada, b_ada):
    bsz, d = c.shape
    n = w_ada.shape[1]
    rows = 8
    c_pad = jnp.zeros((rows, d), F32).at[:bsz].set(c)
    tn = _tile(n, 1024, LANES)
    out = pl.pallas_call(
        _ada_kernel,
        grid=(n // tn,),
        in_specs=[pl.BlockSpec((rows, d), lambda j: (0, 0)),
                  pl.BlockSpec((d, tn), lambda j: (0, j)),
                  pl.BlockSpec((1, tn), lambda j: (0, j))],
        out_specs=pl.BlockSpec((rows, tn), lambda j: (0, j)),
        out_shape=jax.ShapeDtypeStruct((rows, n), F32),
        compiler_params=_params("parallel"),
        name="adaln",
    )(c_pad, w_ada, b_ada.reshape(1, n))
    return out[:bsz]


def _inproj_kernel(x_ref, g_ref, sc_ref, sh_ref, w_ref, e_ref, o_ref, h_ref, *, epilogue):
    @pl.when(pl.program_id(2) == 0)
    def _():
        h = _rms_mod(x_ref[0], g_ref[...], sc_ref[0], sh_ref[0])
        h_ref[0] = h.astype(h_ref.dtype)

    acc = jnp.dot(h_ref[0], w_ref[...], preferred_element_type=F32)
    o_ref[0] = epilogue(acc, e_ref[...]).astype(o_ref.dtype)


def _proj_kernel(h_ref, w_ref, e_ref, o_ref, *, epilogue):
    acc = jnp.dot(h_ref[0], w_ref[...], preferred_element_type=F32)
    o_ref[0] = epilogue(acc, e_ref[...]).astype(o_ref.dtype)


def _ep_identity(acc, extra):
    return acc


def _ep_sigmoid(acc, extra):
    return _sigmoid(acc)


def _ep_headnorm(acc, extra):
    outs = []
    for h in range(acc.shape[1] // LANES):
        a = acc[:, h * LANES:(h + 1) * LANES]
        ms = jnp.mean(a * a, axis=-1, keepdims=True)
        outs.append(a * lax.rsqrt(ms + NORM_EPS))
    return jnp.concatenate(outs, axis=1) * extra


def _inproj(x, g, scale, shift, w, extra, out_dtype, epilogue, name):
    bsz, seq, d = x.shape
    n = w.shape[1]
    tm = _tile(seq, 1024)
    tn = _tile(n, 512, LANES)
    return pl.pallas_call(
        functools.partial(_inproj_kernel, epilogue=epilogue),
        grid=(bsz, seq // tm, n // tn),
        in_specs=[pl.BlockSpec((1, tm, d), lambda b, i, j: (b, i, 0)),
                  pl.BlockSpec((1, d), lambda b, i, j: (0, 0)),
                  pl.BlockSpec((1, 1, d), lambda b, i, j: (b, 0, 0)),
                  pl.BlockSpec((1, 1, d), lambda b, i, j: (b, 0, 0)),
                  pl.BlockSpec((d, tn), lambda b, i, j: (0, j)),
                  pl.BlockSpec((1, tn), lambda b, i, j: (0, j))],
        out_specs=[pl.BlockSpec((1, tm, tn), lambda b, i, j: (b, i, j)),
                   pl.BlockSpec((1, tm, d), lambda b, i, j: (b, i, 0))],
        out_shape=[jax.ShapeDtypeStruct((bsz, seq, n), out_dtype),
                   jax.ShapeDtypeStruct((bsz, seq, d), BF16)],
        compiler_params=_params("parallel", "parallel", "arbitrary"),
        name=name,
    )(x, g.reshape(1, d), scale, shift, w, extra)


def _proj(h, w, extra, out_dtype, epilogue, name):
    bsz, seq, d = h.shape
    n = w.shape[1]
    tm = _tile(seq, 1024)
    tn = _tile(n, 512, LANES)
    return pl.pallas_call(
        functools.partial(_proj_kernel, epilogue=epilogue),
        grid=(bsz, seq // tm, n // tn),
        in_specs=[pl.BlockSpec((1, tm, d), lambda b, i, j: (b, i, 0)),
                  pl.BlockSpec((d, tn), lambda b, i, j: (0, j)),
                  pl.BlockSpec((1, tn), lambda b, i, j: (0, j))],
        out_specs=pl.BlockSpec((1, tm, tn), lambda b, i, j: (b, i, j)),
        out_shape=jax.ShapeDtypeStruct((bsz, seq, n), out_dtype),
        compiler_params=_params("parallel", "parallel", "parallel"),
        name=name,
    )(h, w, extra)


def _rwkv_prep_kernel(p_ref, mu_ref, wd_ref, w0_ref, wi_ref, a0_ref, wg_ref, kk_ref, ka_ref,
                      r_o, k_o, v_o, lw_o, kk_o, a_o, g_o, carry, *, width, gate_rank):
    @pl.when(pl.program_id(1) == 0)
    def _():
        carry[...] = jnp.zeros_like(carry)

    p = p_ref[0]
    tt = p.shape[0]
    row = lax.broadcasted_iota(jnp.int32, p.shape, 0)
    prev = jnp.where(row == 0, carry[...], pltpu.roll(p, 1, axis=0))
    carry[...] = p[tt - 1:tt, :]
    pm = p + mu_ref[...] * (prev - p)
    w = width
    r = pm[:, 0:w]
    k = pm[:, w:2 * w]
    v = pm[:, 2 * w:3 * w]
    d_lo = pm[:, 3 * w:3 * w + LANES]
    a_lo = pm[:, 3 * w + LANES:3 * w + 2 * LANES]
    g_lo = pm[:, 3 * w + 2 * LANES:3 * w + 2 * LANES + gate_rank]
    w_pre = w0_ref[...] + jnp.dot(jnp.tanh(d_lo), wd_ref[...], preferred_element_type=F32,
                                  precision=HIGHEST)
    lw_o[0] = -jnp.exp(-0.5) * _sigmoid(w_pre)
    a = _sigmoid(a0_ref[...] + jnp.dot(a_lo, wi_ref[...], preferred_element_type=F32,
                                       precision=HIGHEST))
    g_o[0] = jnp.dot(_sigmoid(g_lo).astype(BF16), wg_ref[...], preferred_element_type=F32)
    r_o[0] = r
    v_o[0] = v
    a_o[0] = a
    kk_o[0] = k * kk_ref[...]
    k_o[0] = k * (1.0 + (a - 1.0) * ka_ref[...])


def _rwkv_prep(p, mu, wd, w0, wi, a0, wg, k_k, k_a, width, gate_rank):
    bsz, seq, n = p.shape
    tt = _tile(seq, 256)
    row = lambda arr: arr.reshape(1, -1)
    full = lambda shape: pl.BlockSpec(shape, lambda b, i: (0,) * len(shape))
    out_sds = jax.ShapeDtypeStruct((bsz, seq, width), F32)
    out_spec = pl.BlockSpec((1, tt, width), lambda b, i: (b, i, 0))
    return pl.pallas_call(
        functools.partial(_rwkv_prep_kernel, width=width, gate_rank=gate_rank),
        grid=(bsz, seq // tt),
        in_specs=[pl.BlockSpec((1, tt, n), lambda b, i: (b, i, 0)),
                  full((1, n)), full(wd.shape), full((1, width)), full(wi.shape), full((1, width)),
                  full(wg.shape), full((1, width)), full((1, width))],
        out_specs=[out_spec] * 7,
        out_shape=[out_sds] * 7,
        scratch_shapes=[pltpu.VMEM((1, n), F32)],
        compiler_params=_params("parallel", "arbitrary"),
        name="rwkv_prep",
    )(p, row(mu), wd, row(w0), wi, row(a0), wg, row(k_k), row(k_a))


def _rwkv_scan_kernel(r_ref, k_ref, v_ref, lw_ref, kk_ref, a_ref, g_ref, rk_ref, gg_ref, gb_ref,
                      o_ref, s_scr, *, n_chunks, head_dim):
    L = RWKV_CHUNK
    L2 = 2 * L

    @pl.when(pl.program_id(2) == 0)
    def _():
        s_scr[...] = jnp.zeros_like(s_scr)

    lane = lax.broadcasted_iota(jnp.int32, (1, LANES), 1)
    m0 = (lane < head_dim).astype(F32)
    m1 = 1.0 - m0
    r2 = lax.broadcasted_iota(jnp.int32, (L2, L2), 0)
    c2 = lax.broadcasted_iota(jnp.int32, (L2, L2), 1)
    same = (r2 < L) == (c2 < L)
    strict = jnp.logical_and(same, c2 < r2)
    incl = jnp.logical_and(same, c2 <= r2)
    eye = (r2 == c2).astype(F32)
    rowi = lax.broadcasted_iota(jnp.int32, (L, LANES), 0)
    n_pairs = r_ref.shape[2] // LANES

    def stack_f32(x):
        return jnp.concatenate([x * m0, x * m1], axis=0)

    def stack(x):
        return stack_f32(x).astype(BF16)

    def head_sum(x):
        s0 = jnp.sum(x * m0, axis=-1, keepdims=True)
        s1 = jnp.sum(x * m1, axis=-1, keepdims=True)
        return s0 * m0 + s1 * m1

    def nt(a, b):
        return lax.dot_general(a, b, (((1,), (1,)), ((), ())), preferred_element_type=F32)

    def mm(a, b):
        return jnp.dot(a.astype(BF16), b.astype(BF16), preferred_element_type=F32)

    blk_masks = []
    size = 8
    while size <= L:
        blk_masks.append(jnp.bitwise_xor(r2, c2) < size)
        size *= 2
    off_masks = [jnp.logical_and(hi, jnp.logical_not(lo))
                 for lo, hi in zip(blk_masks[:-1], blk_masks[1:])]
    chunks = range(n_chunks)

    ctx = []
    for c, p in [(c, p) for c in chunks for p in range(n_pairs)]:
        sl = pl.ds(c * L, L)
        ln = pl.ds(p * LANES, LANES)
        r = r_ref[0, sl, ln]
        k = k_ref[0, sl, ln]
        v = v_ref[0, sl, ln]
        lw = lw_ref[0, sl, ln]
        kk = kk_ref[0, sl, ln]
        aic = a_ref[0, sl, ln]
        rk = rk_ref[:, ln]
        cum = lw
        sh = 1
        while sh < L:
            cum = cum + jnp.where(rowi >= sh, pltpu.roll(cum, sh, axis=0), 0.0)
            sh *= 2
        tot = cum[L - 1:L, :]
        w_rem = jnp.exp(tot - cum)
        w_inv = jnp.exp(-cum)
        kk = kk / jnp.maximum(jnp.sqrt(head_sum(kk * kk)), 1e-12)
        b_vec = kk * aic
        ar = jnp.concatenate([stack(-kk * jnp.exp(cum - lw)), stack(r * jnp.exp(cum))], axis=0)
        bk = jnp.concatenate([stack(b_vec * w_inv), stack(k * w_inv)], axis=0)
        v_st = stack(v)
        b_rem_t = stack_f32(b_vec * w_rem).T.astype(BF16)
        k_rem_t = stack_f32(k * w_rem).T.astype(BF16)
        decay = jnp.broadcast_to(jnp.exp(tot), (LANES, LANES)).T
        ctx.append(dict(sl=sl, ln=ln, p=p, ar=ar, bk=bk, v_st=v_st, decay=decay, b_rem_t=b_rem_t,
                        k_rem_t=k_rem_t, bonus=head_sum(r * k * rk) * v))

    for x in ctx:
        big = nt(x["ar"], x["bk"])
        x["n_ab"] = jnp.where(strict, big[0:L2, 0:L2], 0.0)
        m_ak = jnp.where(strict, big[0:L2, L2:2 * L2], 0.0)
        m_rk = jnp.where(incl, big[L2:2 * L2, L2:2 * L2], 0.0)
        x["m_rb"] = jnp.where(incl, big[L2:2 * L2, 0:L2], 0.0).astype(BF16)
        x["m_akrk"] = jnp.concatenate([m_ak, m_rk], axis=0).astype(BF16)
    for x in ctx:
        x["m_v"] = mm(x["m_akrk"], x["v_st"])
        x["kv"] = mm(x["k_rem_t"], x["v_st"])

    for x in ctx:
        x["d8"] = jnp.where(blk_masks[0], x["n_ab"], 0.0)
        x["t"] = eye + x["d8"]
    for x in ctx:
        x["pw"] = mm(x["d8"], x["d8"])
    for x in ctx:
        x["t"] = x["t"] + mm(x["t"], x["pw"])
    for x in ctx:
        x["pw"] = mm(x["pw"], x["pw"])
    for x in ctx:
        x["t"] = x["t"] + mm(x["t"], x["pw"])
    for off in off_masks:
        for x in ctx:
            x["nt"] = mm(jnp.where(off, x["n_ab"], 0.0), x["t"])
        for x in ctx:
            x["t"] = x["t"] + mm(x["t"], x["nt"])

    for x in ctx:
        rhs = jnp.concatenate([x["ar"][0:L2], x["m_v"][0:L2].astype(BF16)], axis=1)
        x["ta_uv"] = mm(x["t"], rhs).astype(BF16)
    for x in ctx:
        w_q = mm(x["b_rem_t"], x["ta_uv"])
        g_y = mm(x["m_rb"], x["ta_uv"])
        x["q"] = w_q[:, LANES:] + x["kv"]
        x["y_loc"] = g_y[:, LANES:] + x["m_v"][L2:2 * L2]
        g = g_y[:, :LANES] + x["ar"][L2:2 * L2].astype(F32)
        x["wg"] = jnp.concatenate([w_q[:, :LANES], g], axis=0).astype(BF16)

    s = [s_scr[p] for p in range(n_pairs)]
    for x in ctx:
        p = x["p"]
        ws = mm(x["wg"], s[p])
        x["y_st"] = ws[L2:2 * L2] + x["y_loc"]
        s[p] = x["decay"] * s[p] + ws[0:L2] + x["q"]
    for p in range(n_pairs):
        s_scr[p] = s[p]

    inv_n = 1.0 / head_dim
    for x in ctx:
        y_st = x["y_st"]
        y = y_st[0:L] + y_st[L:L2]
        mean = head_sum(y) * inv_n
        yc = y - mean
        var = head_sum(yc * yc) * inv_n
        y_gn = yc * lax.rsqrt(var + GN_EPS) * gg_ref[:, x["ln"]] + gb_ref[:, x["ln"]]
        o_ref[0, x["sl"], x["ln"]] = ((y_gn + x["bonus"]) * g_ref[0, x["sl"], x["ln"]]).astype(o_ref.dtype)


def _rwkv_scan(r, k, v, lw, kk, a, g, r_k, gn_g, gn_b, head_dim):
    bsz, seq, width = r.shape
    assert 2 * head_dim == LANES and width % LANES == 0
    tc = _tile(seq, 8 * RWKV_CHUNK, RWKV_CHUNK)
    pw = _tile(width, RWKV_PAIRS * LANES, LANES)
    tok = pl.BlockSpec((1, tc, pw), lambda b, h, i: (b, i, h))
    par = pl.BlockSpec((1, pw), lambda b, h, i: (0, h))
    row = lambda arr: arr.reshape(1, width)
    return pl.pallas_call(
        functools.partial(_rwkv_scan_kernel, n_chunks=tc // RWKV_CHUNK, head_dim=head_dim),
        grid=(bsz, width // pw, seq // tc),
        in_specs=[tok] * 7 + [par] * 3,
        out_specs=tok,
        out_shape=jax.ShapeDtypeStruct((bsz, seq, width), BF16),
        scratch_shapes=[pltpu.VMEM((pw // LANES, LANES, LANES), F32)],
        compiler_params=_params("parallel", "parallel", "arbitrary"),
        name="rwkv_scan",
    )(r, k, v, lw, kk, a, g, row(r_k), row(gn_g), row(gn_b))


def _logf_cumsum_kernel(f_ref, b_ref, o_ref, carry):
    @pl.when(pl.program_id(1) == 0)
    def _():
        carry[...] = jnp.zeros_like(carry)

    z = f_ref[0] + b_ref[...]
    lf = jnp.minimum(z, 0.0) - jnp.log(1.0 + jnp.exp(-jnp.abs(z)))
    ts = z.shape[1]
    ri = lax.broadcasted_iota(jnp.int32, (LANES, LANES), 0)
    ci = lax.broadcasted_iota(jnp.int32, (LANES, LANES), 1)
    upper = (ri <= ci).astype(F32)
    run = carry[...]
    for j in range(ts // LANES):
        blk = jnp.dot(lf[:, j * LANES:(j + 1) * LANES], upper, preferred_element_type=F32,
                      precision=HIGHEST) + run
        o_ref[0, :, j * LANES:(j + 1) * LANES] = blk
        run = blk[:, LANES - 1:LANES]
    carry[...] = run


def _logf_cumsum(f_t, b_f):
    bsz, nh, seq = f_t.shape
    ts = _tile(seq, 2048, LANES)
    return pl.pallas_call(
        _logf_cumsum_kernel,
        grid=(bsz, seq // ts),
        in_specs=[pl.BlockSpec((1, nh, ts), lambda b, i: (b, 0, i)),
                  pl.BlockSpec((nh, 1), lambda b, i: (0, 0))],
        out_specs=pl.BlockSpec((1, nh, ts), lambda b, i: (b, 0, i)),
        out_shape=jax.ShapeDtypeStruct((bsz, nh, seq), F32),
        scratch_shapes=[pltpu.VMEM((nh, 1), F32)],
        compiler_params=_params("parallel", "arbitrary"),
        name="fox_logf_cumsum",
    )(f_t, b_f.reshape(nh, 1))


def _fox_kernel(qi_ref, ki_ref, q_ref, k_ref, v_ref, ck_ref, og_ref, o_ref,
                m_scr, l_scr, acc_scr, *, rows):
    p = pl.program_id(2)
    qi = qi_ref[p]
    ki = ki_ref[p]
    tq = q_ref.shape[1]
    tk = k_ref.shape[1]

    @pl.when(ki == 0)
    def _():
        m_scr[...] = jnp.full_like(m_scr, NEG_BIG)
        l_scr[...] = jnp.zeros_like(l_scr)
        acc_scr[...] = jnp.zeros_like(acc_scr)

    n_rc = tq // rows

    n_kq = tq // tk

    def step(diag):
        k = k_ref[0]
        v = v_ref[0]
        ck = ck_ref[0, 0] * LOG2E
        col0 = 0 if diag is None else diag * tk
        live = [rc for rc in range(n_rc) if diag is None or col0 < (rc + 1) * rows]

        def n_cols(rc):
            return tk if diag is None else min(tk, (rc + 1) * rows - col0)

        def logits(rc):
            kc = n_cols(rc)
            s = lax.dot_general(q_ref[0, pl.ds(rc * rows, rows), :], k[:kc], (((1,), (1,)), ((), ())),
                                preferred_element_type=F32) - ck[:, :kc]
            if diag is not None and col0 + kc - 1 > rc * rows:
                ri = lax.broadcasted_iota(jnp.int32, s.shape, 0) + rc * rows
                ci = lax.broadcasted_iota(jnp.int32, s.shape, 1) + col0
                s = jnp.where(ci <= ri, s, NEG_BIG)
            return s

        old = {rc: (m_scr[pl.ds(rc * rows, rows), :], l_scr[pl.ds(rc * rows, rows), :],
                    acc_scr[pl.ds(rc * rows, rows), :]) for rc in live}
        new = {}
        s_next = logits(live[0])
        for i, rc in enumerate(live):
            s = s_next
            if i + 1 < len(live):
                s_next = logits(live[i + 1])
            m_prev, l_prev, acc_prev = old[rc]
            tiles = [s[:, j * LANES:(j + 1) * LANES] for j in range(n_cols(rc) // LANES)]
            m_new = jnp.maximum(m_prev, jnp.max(functools.reduce(jnp.maximum, tiles),
                                                axis=-1, keepdims=True))
            alpha = jnp.exp2(m_prev - m_new)
            p_tiles = [jnp.exp2(t - m_new) for t in tiles]
            l_new = alpha * l_prev + functools.reduce(jnp.add, p_tiles)
            pr = jnp.concatenate([t.astype(BF16) for t in p_tiles], axis=1)
            acc_new = alpha * acc_prev + jnp.dot(pr, v[:n_cols(rc)], preferred_element_type=F32)
            new[rc] = (m_new, l_new, acc_new)
        for rc in live:
            rs = pl.ds(rc * rows, rows)
            m_scr[rs, :], l_scr[rs, :], acc_scr[rs, :] = new[rc]

    @pl.when(ki < qi * n_kq)
    def _():
        step(None)

    for diag in range(n_kq):
        @pl.when(ki == qi * n_kq + diag)
        def _(diag=diag):
            step(diag)

    @pl.when(ki == qi * n_kq + n_kq - 1)
    def _():
        o = acc_scr[...] / jnp.sum(l_scr[...], axis=-1, keepdims=True)
        o_ref[0] = (o * _sigmoid(og_ref[0])).astype(o_ref.dtype)


def _fox_attention(qk, v, cum, ogf, n_heads):
    bsz, seq, _ = v.shape
    tk = _tile(seq, FOX_TK, LANES)
    tq = _tile(seq, FOX_TQ, tk)
    qi_list, ki_list = [], []
    for qi in range(seq // tq):
        for ki in range((qi + 1) * (tq // tk)):
            qi_list.append(qi)
            ki_list.append(ki)
    qi_arr = jnp.asarray(qi_list, jnp.int32)
    ki_arr = jnp.asarray(ki_list, jnp.int32)
    cum_row = cum[:, :, None, :]
    nh = n_heads
    grid_spec = pltpu.PrefetchScalarGridSpec(
        num_scalar_prefetch=2,
        grid=(bsz, nh, len(qi_list)),
        in_specs=[pl.BlockSpec((1, tq, LANES), lambda b, h, p, qi, ki: (b, qi[p], h)),
                  pl.BlockSpec((1, tk, LANES), lambda b, h, p, qi, ki: (b, ki[p], nh + h)),
                  pl.BlockSpec((1, tk, LANES), lambda b, h, p, qi, ki: (b, ki[p], h)),
                  pl.BlockSpec((1, 1, 1, tk), lambda b, h, p, qi, ki: (b, h, 0, ki[p])),
                  pl.BlockSpec((1, tq, LANES), lambda b, h, p, qi, ki: (b, qi[p], h))],
        out_specs=pl.BlockSpec((1, tq, LANES), lambda b, h, p, qi, ki: (b, qi[p], h)),
        scratch_shapes=[pltpu.VMEM((tq, LANES), F32), pltpu.VMEM((tq, LANES), F32),
                        pltpu.VMEM((tq, LANES), F32)],
    )
    return pl.pallas_call(
        functools.partial(_fox_kernel, rows=_tile(tq, FOX_ROWS, LANES)),
        grid_spec=grid_spec,
        out_shape=jax.ShapeDtypeStruct((bsz, seq, nh * LANES), BF16),
        compiler_params=_params("parallel", "parallel", "arbitrary"),
        name="fox_attention",
    )(qi_arr, ki_arr, qk, qk, v, cum_row, ogf)


def _merge_kernel(ya_ref, yb_ref, ga_ref, gb_ref, wa_ref, wb_ref, o_ref):
    pa = jnp.dot(ya_ref[0], wa_ref[...], preferred_element_type=F32)
    pb = jnp.dot(yb_ref[0], wb_ref[...], preferred_element_type=F32)
    o_ref[0] = (ga_ref[0].astype(F32) * pa + gb_ref[0].astype(F32) * pb).astype(o_ref.dtype)


def _merge(y_a, y_b, gates, w_a, w_b):
    bsz, seq, wa = y_a.shape
    wb = y_b.shape[2]
    d = w_a.shape[1]
    tm = _tile(seq, 512)
    return pl.pallas_call(
        _merge_kernel,
        grid=(bsz, seq // tm),
        in_specs=[pl.BlockSpec((1, tm, wa), lambda b, i: (b, i, 0)),
                  pl.BlockSpec((1, tm, wb), lambda b, i: (b, i, 0)),
                  pl.BlockSpec((1, tm, d), lambda b, i: (b, i, 0)),
                  pl.BlockSpec((1, tm, d), lambda b, i: (b, i, 1)),
                  pl.BlockSpec((wa, d), lambda b, i: (0, 0)),
                  pl.BlockSpec((wb, d), lambda b, i: (0, 0))],
        out_specs=pl.BlockSpec((1, tm, d), lambda b, i: (b, i, 0)),
        out_shape=jax.ShapeDtypeStruct((bsz, seq, d), BF16),
        compiler_params=_params("parallel", "parallel"),
        name="branch_merge",
    )(y_a, y_b, gates, gates, w_a, w_b)


def _route_kernel(m_ref, x_ref, g1_ref, w_ref, n2_ref, sc_ref, sh_ref, wr_ref, br_ref,
                  x1_ref, route_ref, cnt_ref, base, *, n_experts):
    @pl.when(jnp.logical_and(pl.program_id(0) == 0, pl.program_id(1) == 0))
    def _():
        base[...] = jnp.zeros_like(base)

    x1 = x_ref[0] + g1_ref[0] * jnp.dot(m_ref[0], w_ref[...], preferred_element_type=F32)
    x1_ref[0] = x1
    h = _rms_mod(x1, n2_ref[...], sc_ref[0], sh_ref[0])
    h_hi = h.astype(BF16)
    h_lo = (h - h_hi.astype(F32)).astype(BF16)
    logits = jnp.dot(jnp.concatenate([h_hi, h_hi, h_lo], axis=1), wr_ref[...],
                     preferred_element_type=F32) + br_ref[...]
    tm = logits.shape[0]
    lane_i = lax.broadcasted_iota(jnp.int32, (tm, LANES), 1)
    lane = lane_i.astype(F32)
    vals = jnp.where(lane_i < n_experts, logits, -jnp.inf)
    top_v, top_i, hot = [], [], []
    for _ in range(TOP_K):
        mx = jnp.max(vals, axis=-1, keepdims=True)
        ix = jnp.min(jnp.where(vals == mx, lane, float(LANES)), axis=-1, keepdims=True)
        sel = lane == ix
        vals = jnp.where(sel, -jnp.inf, vals)
        top_v.append(mx)
        top_i.append(ix)
        hot.append(sel.astype(F32))
    ex = [jnp.exp(tv - top_v[0]) for tv in top_v]
    den = ex[0] + ex[1] + ex[2] + ex[3]
    cnt = hot[0] + hot[1] + hot[2] + hot[3]
    ri = lax.broadcasted_iota(jnp.int32, (tm, tm), 0)
    ci = lax.broadcasted_iota(jnp.int32, (tm, tm), 1)
    before = jnp.dot((ci < ri).astype(BF16), cnt.astype(BF16), preferred_element_type=F32)
    before = before + base[...]
    out = jnp.zeros((tm, LANES), F32)
    for kk in range(TOP_K):
        rank = jnp.sum(hot[kk] * before, axis=-1, keepdims=True)
        out = jnp.where(lane_i == kk, top_i[kk], out)
        out = jnp.where(lane_i == TOP_K + kk, ex[kk] / den, out)
        out = jnp.where(lane_i == 2 * TOP_K + kk, rank, out)
    route_ref[0] = out
    new_base = base[...] + jnp.sum(cnt, axis=0, keepdims=True)
    base[...] = new_base
    cnt_ref[...] = jnp.broadcast_to(new_base, cnt_ref.shape)


def _route(merged, x, gate1, w_out, norm2_g, scale2, shift2, w_router, b_router):
    bsz, seq, d = x.shape
    n_experts = w_router.shape[1]
    tm = _tile(seq, 512)
    wr = jnp.zeros((d, LANES), F32).at[:, :n_experts].set(w_router)
    wr_hi = wr.astype(BF16)
    wr_lo = (wr - wr_hi.astype(F32)).astype(BF16)
    wr = jnp.concatenate([wr_hi, wr_lo, wr_hi], axis=0)
    br = jnp.zeros((1, LANES), F32).at[0, :n_experts].set(b_router)
    mod = pl.BlockSpec((1, 1, d), lambda b, i: (b, 0, 0))
    tok = pl.BlockSpec((1, tm, d), lambda b, i: (b, i, 0))
    return pl.pallas_call(
        functools.partial(_route_kernel, n_experts=n_experts),
        grid=(bsz, seq // tm),
        in_specs=[tok, tok, mod,
                  pl.BlockSpec((d, d), lambda b, i: (0, 0)),
                  pl.BlockSpec((1, d), lambda b, i: (0, 0)), mod, mod,
                  pl.BlockSpec((3 * d, LANES), lambda b, i: (0, 0)),
                  pl.BlockSpec((1, LANES), lambda b, i: (0, 0))],
        out_specs=[tok,
                   pl.BlockSpec((1, tm, LANES), lambda b, i: (b, i, 0)),
                   pl.BlockSpec((8, LANES), lambda b, i: (0, 0))],
        out_shape=[jax.ShapeDtypeStruct((bsz, seq, d), F32),
                   jax.ShapeDtypeStruct((bsz, seq, LANES), F32),
                   jax.ShapeDtypeStruct((8, LANES), F32)],
        scratch_shapes=[pltpu.VMEM((1, LANES), F32)],
        compiler_params=_params("arbitrary", "arbitrary"),
        name="residual_router",
    )(merged, x, gate1, w_out, norm2_g.reshape(1, d), scale2, shift2, wr, br)


def _dispatch_kernel(slot_ref, x_ref, n2_ref, sc_ref, sh_ref, xs_in_ref, xs_ref, hbuf, sem):
    del xs_in_ref
    i = pl.program_id(0)
    tm = hbuf.shape[0]
    hbuf[...] = _rms_mod(x_ref[...], n2_ref[...], sc_ref[...], sh_ref[...])

    def row_copy(r, s):
        return pltpu.make_async_copy(hbuf.at[pl.ds(r, 1)], xs_ref.at[pl.ds(s, 1)], sem)

    def issue(r, carry):
        for kk in range(TOP_K):
            row_copy(r, slot_ref[(i * tm + r) * TOP_K + kk]).start()
        return carry

    lax.fori_loop(0, tm, issue, 0)

    def drain(r, carry):
        for kk in range(TOP_K):
            row_copy(0, 0).wait()
        return carry

    lax.fori_loop(0, tm, drain, 0)


def _dispatch(x1_b, slots_b, norm2_g, scale2_b, shift2_b, xs):
    seq, d = x1_b.shape
    tm = _tile(seq, 256)
    grid_spec = pltpu.PrefetchScalarGridSpec(
        num_scalar_prefetch=1,
        grid=(seq // tm,),
        in_specs=[pl.BlockSpec((tm, d), lambda i, s: (i, 0)),
                  pl.BlockSpec((1, d), lambda i, s: (0, 0)),
                  pl.BlockSpec((1, d), lambda i, s: (0, 0)),
                  pl.BlockSpec((1, d), lambda i, s: (0, 0)),
                  pl.BlockSpec(memory_space=pl.ANY)],
        out_specs=pl.BlockSpec(memory_space=pl.ANY),
        scratch_shapes=[pltpu.VMEM((tm, d), F32), pltpu.SemaphoreType.DMA(())],
    )
    return pl.pallas_call(
        _dispatch_kernel,
        grid_spec=grid_spec,
        out_shape=jax.ShapeDtypeStruct(xs.shape, xs.dtype),
        input_output_aliases={5: 0},
        compiler_params=_params("arbitrary"),
        name="moe_dispatch",
    )(slots_b, x1_b, norm2_g.reshape(1, d), scale2_b, shift2_b, xs)


def _expert_kernel(be_ref, na_ref, x_ref, wg_ref, wl_ref, bg_ref, bl_ref, wd_ref, bd_ref,
                   o_ref, acc, *, n_f):
    b = pl.program_id(0)
    f = pl.program_id(1)

    @pl.when(jnp.logical_and(b == 0, f == 0))
    def _():
        acc[...] = jnp.zeros_like(acc)

    @pl.when(b < na_ref[0])
    def _():
        x = x_ref[...].astype(BF16)
        gate = jnp.dot(x, wg_ref[0], preferred_element_type=F32) + bg_ref[0]
        lin = jnp.dot(x, wl_ref[0], preferred_element_type=F32) + bl_ref[0]
        gate = jnp.minimum(gate, SWIGLU_LIMIT)
        lin = jnp.clip(lin, -SWIGLU_LIMIT, SWIGLU_LIMIT)
        act = gate * _sigmoid(SWIGLU_ALPHA * gate) * (lin + 1.0)
        contrib = jnp.dot(act.astype(BF16), wd_ref[0], preferred_element_type=F32)
        if n_f == 1:
            o_ref[...] = contrib + bd_ref[0]
        else:
            total = jnp.where(f > 0, acc[...], 0.0) + contrib
            acc[...] = total
            o_ref[...] = total + bd_ref[0]

    @pl.when(jnp.logical_and(b >= na_ref[0], f == n_f - 1))
    def _():
        o_ref[...] = jnp.zeros_like(o_ref)


def _experts(xs, blk_e, n_active, w_gu, b_gu, w_dn, b_dn):
    n_slots, d = xs.shape
    n_e, _, two_ff = w_gu.shape
    d_ff = two_ff // 2
    bm = EXPERT_ROWS
    n_blocks = n_slots // bm
    tf = _tile(d_ff, EXPERT_FF_TILE, LANES)
    n_f = d_ff // tf

    def blk(b, na):
        return jnp.minimum(b, na[0] - 1)

    def ff(b, f, na):
        return jnp.where(b < na[0], f, n_f - 1)

    wmode = dict(pipeline_mode=pl.Buffered(1)) if n_f == 1 else {}
    grid_spec = pltpu.PrefetchScalarGridSpec(
        num_scalar_prefetch=2,
        grid=(n_blocks, n_f),
        in_specs=[pl.BlockSpec((bm, d), lambda b, f, be, na: (blk(b, na), 0)),
                  pl.BlockSpec((1, d, tf), lambda b, f, be, na: (be[blk(b, na)], 0, ff(b, f, na)), **wmode),
                  pl.BlockSpec((1, d, tf), lambda b, f, be, na: (be[blk(b, na)], 0, n_f + ff(b, f, na)),
                               **wmode),
                  pl.BlockSpec((1, 1, tf), lambda b, f, be, na: (be[blk(b, na)], 0, ff(b, f, na))),
                  pl.BlockSpec((1, 1, tf), lambda b, f, be, na: (be[blk(b, na)], 0, n_f + ff(b, f, na))),
                  pl.BlockSpec((1, tf, d), lambda b, f, be, na: (be[blk(b, na)], ff(b, f, na), 0), **wmode),
                  pl.BlockSpec((1, 1, d), lambda b, f, be, na: (be[blk(b, na)], 0, 0))],
        out_specs=pl.BlockSpec((bm, d), lambda b, f, be, na: (b, 0)),
        scratch_shapes=[pltpu.VMEM((bm, d) if n_f > 1 else (8, LANES), F32)],
    )
    return pl.pallas_call(
        functools.partial(_expert_kernel, n_f=n_f),
        grid_spec=grid_spec,
        out_shape=jax.ShapeDtypeStruct((n_slots, d), F32),
        compiler_params=_params("arbitrary", "arbitrary"),
        name="moe_experts",
    )(blk_e, n_active, xs, w_gu, w_gu, b_gu.reshape(n_e, 1, two_ff), b_gu.reshape(n_e, 1, two_ff),
      w_dn, b_dn.reshape(n_e, 1, d))


def _combine_kernel(slot_ref, x_ref, route_ref, g2_ref, gf_ref, ys_ref, o_ref, buf, sem):
    i = pl.program_id(0)
    tm = x_ref.shape[0]

    def row_copy(kk, r, s):
        return pltpu.make_async_copy(ys_ref.at[pl.ds(s, 1)], buf.at[kk, pl.ds(r, 1)], sem)

    def issue(r, carry):
        for kk in range(TOP_K):
            row_copy(kk, r, slot_ref[(i * tm + r) * TOP_K + kk]).start()
        return carry

    lax.fori_loop(0, tm, issue, 0)

    def drain(r, carry):
        for kk in range(TOP_K):
            row_copy(0, 0, 0).wait()
        return carry

    lax.fori_loop(0, tm, drain, 0)

    route = route_ref[...]
    y = jnp.zeros(x_ref.shape, F32)
    for kk in range(TOP_K):
        y = y + buf[kk] * route[:, TOP_K + kk:TOP_K + kk + 1]
    x2 = x_ref[...] + g2_ref[...] * y
    ms = jnp.mean(x2 * x2, axis=-1, keepdims=True)
    o_ref[...] = x2 * lax.rsqrt(ms + NORM_EPS) * gf_ref[...]


def _combine(x1_b, route_b, slots_b, gate2_b, norm_final_g, ys):
    seq, d = x1_b.shape
    tm = _tile(seq, 256)
    grid_spec = pltpu.PrefetchScalarGridSpec(
        num_scalar_prefetch=1,
        grid=(seq // tm,),
        in_specs=[pl.BlockSpec((tm, d), lambda i, s: (i, 0)),
                  pl.BlockSpec((tm, LANES), lambda i, s: (i, 0)),
                  pl.BlockSpec((1, d), lambda i, s: (0, 0)),
                  pl.BlockSpec((1, d), lambda i, s: (0, 0)),
                  pl.BlockSpec(memory_space=pl.ANY)],
        out_specs=pl.BlockSpec((tm, d), lambda i, s: (i, 0)),
        scratch_shapes=[pltpu.VMEM((TOP_K, tm, d), F32), pltpu.SemaphoreType.DMA(())],
    )
    return pl.pallas_call(
        _combine_kernel,
        grid_spec=grid_spec,
        out_shape=jax.ShapeDtypeStruct((seq, d), F32),
        compiler_params=_params("arbitrary"),
        name="moe_combine",
    )(slots_b, x1_b, route_b, gate2_b, norm_final_g.reshape(1, d), ys)


def _pad_cols(w, n):
    return jnp.pad(w, ((0, 0), (0, n - w.shape[1])))


def kernel(x, c, w_ada, b_ada, norm1_g, w_in, rwkv_mu, w_decay_up, decay_w0, w_iclr_up, iclr_a0, w_gate_up_rwkv, rwkv_k_k, rwkv_k_a, rwkv_r_k, rwkv_gn_g, rwkv_gn_b, w_out_a, fox_b_f, fox_q_norm, fox_k_norm, w_out_b, w_out, norm2_g, w_router, b_router, w_expert_gu, b_expert_gu, w_expert_down, b_expert_down, norm_final_g):
    bsz, seq, d = x.shape
    rw_heads, rw_hd = rwkv_r_k.shape
    rw = rw_heads * rw_hd
    dr, ir, gr = w_decay_up.shape[0], w_iclr_up.shape[0], w_gate_up_rwkv.shape[0]
    fh = fox_b_f.shape[0]
    fw = w_out_b.shape[0]
    fhd = fw // fh
    assert fhd == LANES and dr <= LANES and ir <= LANES and gr % LANES == 0
    n_experts = w_router.shape[1]

    mod = _ada(c, w_ada, b_ada)
    shift1, scale1, gate1, shift2, scale2, gate2 = (m[:, None, :] for m in jnp.split(mod, 6, axis=-1))

    o_r = 0
    o_f = 3 * rw + dr + ir + gr
    o_g = o_f + 3 * fw + fh + fw
    col = lambda a, n: w_in[:, a:a + n]
    w_rwkv = jnp.concatenate([col(0, 3 * rw), _pad_cols(col(3 * rw, dr), LANES),
                              _pad_cols(col(3 * rw + dr, ir), LANES), col(3 * rw + dr + ir, gr)],
                             axis=1).astype(BF16)
    mu = jnp.concatenate([rwkv_mu[:3 * rw], jnp.pad(rwkv_mu[3 * rw:3 * rw + dr], (0, LANES - dr)),
                          jnp.pad(rwkv_mu[3 * rw + dr:3 * rw + dr + ir], (0, LANES - ir)),
                          rwkv_mu[3 * rw + dr + ir:]])
    w_qk = col(o_f, 2 * fw).astype(BF16)
    w_v = col(o_f + 2 * fw, fw).astype(BF16)
    w_ogf = jnp.concatenate([col(o_f + 3 * fw + fh, fw), _pad_cols(col(o_f + 3 * fw, fh), LANES)],
                            axis=1).astype(BF16)
    w_gates = col(o_g, 2 * d).astype(BF16)
    qk_gain = jnp.concatenate([jnp.tile(fox_q_norm * (fhd ** -0.5 * LOG2E), fh), jnp.tile(fox_k_norm, fh)])

    def row(extra, n):
        return (jnp.zeros((n,), F32) if extra is None else extra).reshape(1, n)

    gates, h1 = _inproj(x, norm1_g, scale1, shift1, w_gates, row(None, w_gates.shape[1]), BF16,
                        _ep_sigmoid, "inproj_merge_gates")

    def proj(w, extra, dtype, ep, name):
        return _proj(h1, w, row(extra, w.shape[1]), dtype, ep, name)

    p_rwkv = proj(w_rwkv, None, F32, _ep_identity, "inproj_rwkv")
    qk = proj(w_qk, qk_gain, BF16, _ep_headnorm, "inproj_fox_qk")
    v_fox = proj(w_v, None, BF16, _ep_identity, "inproj_fox_v")
    ogf = proj(w_ogf, None, F32, _ep_identity, "inproj_fox_gate_forget")

    wd = jnp.pad(w_decay_up, ((0, LANES - dr), (0, 0)))
    wi = jnp.pad(w_iclr_up, ((0, LANES - ir), (0, 0)))
    r, k2, v, lw, kk, aic, g = _rwkv_prep(p_rwkv, mu, wd, decay_w0, wi, iclr_a0,
                                          w_gate_up_rwkv.astype(BF16), rwkv_k_k, rwkv_k_a, rw, gr)
    y_a = _rwkv_scan(r, k2, v, lw, kk, aic, g, rwkv_r_k, rwkv_gn_g, rwkv_gn_b, rw_hd)

    f_t = jnp.transpose(ogf[:, :, fw:fw + fh], (0, 2, 1))
    cum = _logf_cumsum(f_t, fox_b_f)
    y_b = _fox_attention(qk, v_fox, cum, ogf, fh)

    merged = _merge(y_a, y_b, gates, w_out_a.astype(BF16), w_out_b.astype(BF16))
    x1, route, counts = _route(merged, x, gate1, w_out.astype(BF16), norm2_g, scale2, shift2,
                               w_router, b_router)

    n_tok = bsz * seq
    n_assign = n_tok * TOP_K
    bm = EXPERT_ROWS
    n_blocks = -(-n_assign // bm) + n_experts
    cnt = counts[0, :n_experts].astype(jnp.int32)
    padded = (cnt + bm - 1) // bm * bm
    pad_end = jnp.cumsum(padded)
    pad_start = pad_end - padded
    top_i = route[:, :, 0:TOP_K].astype(jnp.int32)
    rank = route[:, :, 2 * TOP_K:3 * TOP_K].astype(jnp.int32)
    e_ids = jnp.arange(n_experts, dtype=jnp.int32)
    start_of = jnp.sum(jnp.where(top_i[..., None] == e_ids, pad_start, 0), axis=-1)
    slots = (start_of + rank).reshape(bsz, seq * TOP_K)
    blk_first = jnp.arange(n_blocks, dtype=jnp.int32) * bm
    blk_e = jnp.minimum(jnp.sum((pad_end[None, :] <= blk_first[:, None]).astype(jnp.int32), axis=1),
                        n_experts - 1)
    n_active = (pad_end[-1:] // bm).astype(jnp.int32)

    xs = jnp.zeros((n_blocks * bm, d), F32)
    for b in range(bsz):
        xs = _dispatch(x1[b], slots[b], norm2_g, scale2[b], shift2[b], xs)
    ys = _experts(xs, blk_e, n_active, w_expert_gu.astype(BF16), b_expert_gu,
                  w_expert_down.astype(BF16), b_expert_down)
    outs = [_combine(x1[b], route[b], slots[b], gate2[b], norm_final_g, ys) for b in range(bsz)]
    return jnp.stack(outs, axis=0)
```

```python
import functools

import jax
import jax.numpy as jnp
from jax import lax
from jax.experimental import pallas as pl
from jax.experimental.pallas import tpu as pltpu

F32 = jnp.float32
BF16 = jnp.bfloat16
HIGHEST = lax.Precision.HIGHEST

TOP_K = 4
NORM_EPS = 1e-6
GN_EPS = 64e-5
SWIGLU_LIMIT = 7.0
SWIGLU_ALPHA = 1.702
LANES = 128
RWKV_CHUNK = 64
RWKV_PAIRS = 4
VMEM_LIMIT_BYTES = 56 * 1024 * 1024
EXPERT_ROWS = 512
EXPERT_FF_TILE = 1024
NEG_BIG = -1e30
LOG2E = 1.4426950408889634
FOX_ROWS = 256
FOX_TQ = 2048
FOX_TK = 1024


def _tile(n, pref, mult=8):
    t = min(pref, n)
    t -= t % mult
    while t >= mult:
        if n % t == 0:
            return t
        t -= mult
    return n


def _params(*sem):
    return pltpu.CompilerParams(dimension_semantics=sem, vmem_limit_bytes=VMEM_LIMIT_BYTES)


def _sigmoid(x):
    return 1.0 / (1.0 + jnp.exp(-x))


def _rms_mod(x, g, scale, shift):
    ms = jnp.mean(x * x, axis=-1, keepdims=True)
    return x * lax.rsqrt(ms + NORM_EPS) * g * (1.0 + scale) + shift


def _ada_kernel(c_ref, w_ref, b_ref, o_ref):
    c = c_ref[...]
    s = c * _sigmoid(c)
    o_ref[...] = jnp.dot(s, w_ref[...], preferred_element_type=F32, precision=HIGHEST) + b_ref[...]


def _ada(c, w_ada, b_ada):
    bsz, d = c.shape
    n = w_ada.shape[1]
    rows = 8
    c_pad = jnp.zeros((rows, d), F32).at[:bsz].set(c)
    tn = _tile(n, 1024, LANES)
    out = pl.pallas_call(
        _ada_kernel,
        grid=(n // tn,),
        in_specs=[pl.BlockSpec((rows, d), lambda j: (0, 0)),
                  pl.BlockSpec((d, tn), lambda j: (0, j)),
                  pl.BlockSpec((1, tn), lambda j: (0, j))],
        out_specs=pl.BlockSpec((rows, tn), lambda j: (0, j)),
        out_shape=jax.ShapeDtypeStruct((rows, n), F32),
        compiler_params=_params("parallel"),
        name="adaln",
    )(c_pad, w_ada, b_ada.reshape(1, n))
    return out[:bsz]


def _inproj_kernel(x_ref, g_ref, sc_ref, sh_ref, w_ref, e_ref, o_ref, h_ref, *, epilogue):
    @pl.when(pl.program_id(2) == 0)
    def _():
        h = _rms_mod(x_ref[0], g_ref[...], sc_ref[0], sh_ref[0])
        h_ref[0] = h.astype(h_ref.dtype)

    acc = jnp.dot(h_ref[0], w_ref[...], preferred_element_type=F32)
    o_ref[0] = epilogue(acc, e_ref[...]).astype(o_ref.dtype)


def _proj_kernel(h_ref, w_ref, e_ref, o_ref, *, epilogue):
    acc = jnp.dot(h_ref[0], w_ref[...], preferred_element_type=F32)
    o_ref[0] = epilogue(acc, e_ref[...]).astype(o_ref.dtype)


def _ep_identity(acc, extra):
    return acc


def _ep_sigmoid(acc, extra):
    return _sigmoid(acc)


def _ep_headnorm(acc, extra):
    outs = []
    for h in range(acc.shape[1] // LANES):
        a = acc[:, h * LANES:(h + 1) * LANES]
        ms = jnp.mean(a * a, axis=-1, keepdims=True)
        outs.append(a * lax.rsqrt(ms + NORM_EPS))
    return jnp.concatenate(outs, axis=1) * extra


def _inproj(x, g, scale, shift, w, extra, out_dtype, epilogue, name):
    bsz, seq, d = x.shape
    n = w.shape[1]
    tm = _tile(seq, 1024)
    tn = _tile(n, 512, LANES)
    return pl.pallas_call(
        functools.partial(_inproj_kernel, epilogue=epilogue),
        grid=(bsz, seq // tm, n // tn),
        in_specs=[pl.BlockSpec((1, tm, d), lambda b, i, j: (b, i, 0)),
                  pl.BlockSpec((1, d), lambda b, i, j: (0, 0)),
                  pl.BlockSpec((1, 1, d), lambda b, i, j: (b, 0, 0)),
                  pl.BlockSpec((1, 1, d), lambda b, i, j: (b, 0, 0)),
                  pl.BlockSpec((d, tn), lambda b, i, j: (0, j)),
                  pl.BlockSpec((1, tn), lambda b, i, j: (0, j))],
        out_specs=[pl.BlockSpec((1, tm, tn), lambda b, i, j: (b, i, j)),
                   pl.BlockSpec((1, tm, d), lambda b, i, j: (b, i, 0))],
        out_shape=[jax.ShapeDtypeStruct((bsz, seq, n), out_dtype),
                   jax.ShapeDtypeStruct((bsz, seq, d), BF16)],
        compiler_params=_params("parallel", "parallel", "arbitrary"),
        name=name,
    )(x, g.reshape(1, d), scale, shift, w, extra)


def _proj(h, w, extra, out_dtype, epilogue, name):
    bsz, seq, d = h.shape
    n = w.shape[1]
    tm = _tile(seq, 1024)
    tn = _tile(n, 512, LANES)
    return pl.pallas_call(
        functools.partial(_proj_kernel, epilogue=epilogue),
        grid=(bsz, seq // tm, n // tn),
        in_specs=[pl.BlockSpec((1, tm, d), lambda b, i, j: (b, i, 0)),
                  pl.BlockSpec((d, tn), lambda b, i, j: (0, j)),
                  pl.BlockSpec((1, tn), lambda b, i, j: (0, j))],
        out_specs=pl.BlockSpec((1, tm, tn), lambda b, i, j: (b, i, j)),
        out_shape=jax.ShapeDtypeStruct((bsz, seq, n), out_dtype),
        compiler_params=_params("parallel", "parallel", "parallel"),
        name=name,
    )(h, w, extra)


def _rwkv_prep_kernel(p_ref, mu_ref, wd_ref, w0_ref, wi_ref, a0_ref, wg_ref, kk_ref, ka_ref,
                      r_o, k_o, v_o, lw_o, kk_o, a_o, g_o, carry, *, width, gate_rank):
    @pl.when(pl.program_id(1) == 0)
    def _():
        carry[...] = jnp.zeros_like(carry)

    p = p_ref[0]
    tt = p.shape[0]
    row = lax.broadcasted_iota(jnp.int32, p.shape, 0)
    prev = jnp.where(row == 0, carry[...], pltpu.roll(p, 1, axis=0))
    carry[...] = p[tt - 1:tt, :]
    pm = p + mu_ref[...] * (prev - p)
    w = width
    r = pm[:, 0:w]
    k = pm[:, w:2 * w]
    v = pm[:, 2 * w:3 * w]
    d_lo = pm[:, 3 * w:3 * w + LANES]
    a_lo = pm[:, 3 * w + LANES:3 * w + 2 * LANES]
    g_lo = pm[:, 3 * w + 2 * LANES:3 * w + 2 * LANES + gate_rank]
    w_pre = w0_ref[...] + jnp.dot(jnp.tanh(d_lo), wd_ref[...], preferred_element_type=F32,
                                  precision=HIGHEST)
    lw_o[0] = -jnp.exp(-0.5) * _sigmoid(w_pre)
    a = _sigmoid(a0_ref[...] + jnp.dot(a_lo, wi_ref[...], preferred_element_type=F32,
                                       precision=HIGHEST))
    g_o[0] = jnp.dot(_sigmoid(g_lo).astype(BF16), wg_ref[...], preferred_element_type=F32)
    r_o[0] = r
    v_o[0] = v
    a_o[0] = a
    kk_o[0] = k * kk_ref[...]
    k_o[0] = k * (1.0 + (a - 1.0) * ka_ref[...])


def _rwkv_prep(p, mu, wd, w0, wi, a0, wg, k_k, k_a, width, gate_rank):
    bsz, seq, n = p.shape
    tt = _tile(seq, 256)
    row = lambda arr: arr.reshape(1, -1)
    full = lambda shape: pl.BlockSpec(shape, lambda b, i: (0,) * len(shape))
    out_sds = jax.ShapeDtypeStruct((bsz, seq, width), F32)
    out_spec = pl.BlockSpec((1, tt, width), lambda b, i: (b, i, 0))
    return pl.pallas_call(
        functools.partial(_rwkv_prep_kernel, width=width, gate_rank=gate_rank),
        grid=(bsz, seq // tt),
        in_specs=[pl.BlockSpec((1, tt, n), lambda b, i: (b, i, 0)),
                  full((1, n)), full(wd.shape), full((1, width)), full(wi.shape), full((1, width)),
                  full(wg.shape), full((1, width)), full((1, width))],
        out_specs=[out_spec] * 7,
        out_shape=[out_sds] * 7,
        scratch_shapes=[pltpu.VMEM((1, n), F32)],
        compiler_params=_params("parallel", "arbitrary"),
        name="rwkv_prep",
    )(p, row(mu), wd, row(w0), wi, row(a0), wg, row(k_k), row(k_a))


def _rwkv_scan_kernel(r_ref, k_ref, v_ref, lw_ref, kk_ref, a_ref, g_ref, rk_ref, gg_ref, gb_ref,
                      o_ref, s_scr, *, n_chunks, head_dim):
    L = RWKV_CHUNK
    L2 = 2 * L

    @pl.when(pl.program_id(2) == 0)
    def _():
        s_scr[...] = jnp.zeros_like(s_scr)

    lane = lax.broadcasted_iota(jnp.int32, (1, LANES), 1)
    m0 = (lane < head_dim).astype(F32)
    m1 = 1.0 - m0
    r2 = lax.broadcasted_iota(jnp.int32, (L2, L2), 0)
    c2 = lax.broadcasted_iota(jnp.int32, (L2, L2), 1)
    same = (r2 < L) == (c2 < L)
    strict = jnp.logical_and(same, c2 < r2)
    incl = jnp.logical_and(same, c2 <= r2)
    eye = (r2 == c2).astype(F32)
    rowi = lax.broadcasted_iota(jnp.int32, (L, LANES), 0)
    n_pairs = r_ref.shape[2] // LANES

    def stack_f32(x):
        return jnp.concatenate([x * m0, x * m1], axis=0)

    def stack(x):
        return stack_f32(x).astype(BF16)

    def head_sum(x):
        s0 = jnp.sum(x * m0, axis=-1, keepdims=True)
        s1 = jnp.sum(x * m1, axis=-1, keepdims=True)
        return s0 * m0 + s1 * m1

    def nt(a, b):
        return lax.dot_general(a, b, (((1,), (1,)), ((), ())), preferred_element_type=F32)

    def mm(a, b):
        return jnp.dot(a.astype(BF16), b.astype(BF16), preferred_element_type=F32)

    blk_masks = []
    size = 8
    while size <= L:
        blk_masks.append(jnp.bitwise_xor(r2, c2) < size)
        size *= 2
    off_masks = [jnp.logical_and(hi, jnp.logical_not(lo))
                 for lo, hi in zip(blk_masks[:-1], blk_masks[1:])]
    chunks = range(n_chunks)

    ctx = []
    for c, p in [(c, p) for c in chunks for p in range(n_pairs)]:
        sl = pl.ds(c * L, L)
        ln = pl.ds(p * LANES, LANES)
        r = r_ref[0, sl, ln]
        k = k_ref[0, sl, ln]
        v = v_ref[0, sl, ln]
        lw = lw_ref[0, sl, ln]
        kk = kk_ref[0, sl, ln]
        aic = a_ref[0, sl, ln]
        rk = rk_ref[:, ln]
        cum = lw
        sh = 1
        while sh < L:
            cum = cum + jnp.where(rowi >= sh, pltpu.roll(cum, sh, axis=0), 0.0)
            sh *= 2
        tot = cum[L - 1:L, :]
        w_rem = jnp.exp(tot - cum)
        w_inv = jnp.exp(-cum)
        kk = kk / jnp.maximum(jnp.sqrt(head_sum(kk * kk)), 1e-12)
        b_vec = kk * aic
        ar = jnp.concatenate([stack(-kk * jnp.exp(cum - lw)), stack(r * jnp.exp(cum))], axis=0)
        bk = jnp.concatenate([stack(b_vec * w_inv), stack(k * w_inv)], axis=0)
        v_st = stack(v)
        b_rem_t = stack_f32(b_vec * w_rem).T.astype(BF16)
        k_rem_t = stack_f32(k * w_rem).T.astype(BF16)
        decay = jnp.broadcast_to(jnp.exp(tot), (LANES, LANES)).T
        ctx.append(dict(sl=sl, ln=ln, p=p, ar=ar, bk=bk, v_st=v_st, decay=decay, b_rem_t=b_rem_t,
                        k_rem_t=k_rem_t, bonus=head_sum(r * k * rk) * v))

    for x in ctx:
        big = nt(x["ar"], x["bk"])
        x["n_ab"] = jnp.where(strict, big[0:L2, 0:L2], 0.0)
        m_ak = jnp.where(strict, big[0:L2, L2:2 * L2], 0.0)
        m_rk = jnp.where(incl, big[L2:2 * L2, L2:2 * L2], 0.0)
        x["m_rb"] = jnp.where(incl, big[L2:2 * L2, 0:L2], 0.0).astype(BF16)
        x["m_akrk"] = jnp.concatenate([m_ak, m_rk], axis=0).astype(BF16)
    for x in ctx:
        x["m_v"] = mm(x["m_akrk"], x["v_st"])
        x["kv"] = mm(x["k_rem_t"], x["v_st"])

    for x in ctx:
        x["d8"] = jnp.where(blk_masks[0], x["n_ab"], 0.0)
        x["t"] = eye + x["d8"]
    for x in ctx:
        x["pw"] = mm(x["d8"], x["d8"])
    for x in ctx:
        x["t"] = x["t"] + mm(x["t"], x["pw"])
    for x in ctx:
        x["pw"] = mm(x["pw"], x["pw"])
    for x in ctx:
        x["t"] = x["t"] + mm(x["t"], x["pw"])
    for off in off_masks:
        for x in ctx:
            x["nt"] = mm(jnp.where(off, x["n_ab"], 0.0), x["t"])
        for x in ctx:
            x["t"] = x["t"] + mm(x["t"], x["nt"])

    for x in ctx:
        rhs = jnp.concatenate([x["ar"][0:L2], x["m_v"][0:L2].astype(BF16)], axis=1)
        x["ta_uv"] = mm(x["t"], rhs).astype(BF16)
    for x in ctx:
        w_q = mm(x["b_rem_t"], x["ta_uv"])
        g_y = mm(x["m_rb"], x["ta_uv"])
        x["q"] = w_q[:, LANES:] + x["kv"]
        x["y_loc"] = g_y[:, LANES:] + x["m_v"][L2:2 * L2]
        g = g_y[:, :LANES] + x["ar"][L2:2 * L2].astype(F32)
        x["wg"] = jnp.concatenate([w_q[:, :LANES], g], axis=0).astype(BF16)

    s = [s_scr[p] for p in range(n_pairs)]
    for x in ctx:
        p = x["p"]
        ws = mm(x["wg"], s[p])
        x["y_st"] = ws[L2:2 * L2] + x["y_loc"]
        s[p] = x["decay"] * s[p] + ws[0:L2] + x["q"]
    for p in range(n_pairs):
        s_scr[p] = s[p]

    inv_n = 1.0 / head_dim
    for x in ctx:
        y_st = x["y_st"]
        y = y_st[0:L] + y_st[L:L2]
        mean = head_sum(y) * inv_n
        yc = y - mean
        var = head_sum(yc * yc) * inv_n
        y_gn = yc * lax.rsqrt(var + GN_EPS) * gg_ref[:, x["ln"]] + gb_ref[:, x["ln"]]
        o_ref[0, x["sl"], x["ln"]] = ((y_gn + x["bonus"]) * g_ref[0, x["sl"], x["ln"]]).astype(o_ref.dtype)


def _rwkv_scan(r, k, v, lw, kk, a, g, r_k, gn_g, gn_b, head_dim):
    bsz, seq, width = r.shape
    assert 2 * head_dim == LANES and width % LANES == 0
    tc = _tile(seq, 8 * RWKV_CHUNK, RWKV_CHUNK)
    pw = _tile(width, RWKV_PAIRS * LANES, LANES)
    tok = pl.BlockSpec((1, tc, pw), lambda b, h, i: (b, i, h))
    par = pl.BlockSpec((1, pw), lambda b, h, i: (0, h))
    row = lambda arr: arr.reshape(1, width)
    return pl.pallas_call(
        functools.partial(_rwkv_scan_kernel, n_chunks=tc // RWKV_CHUNK, head_dim=head_dim),
        grid=(bsz, width // pw, seq // tc),
        in_specs=[tok] * 7 + [par] * 3,
        out_specs=tok,
        out_shape=jax.ShapeDtypeStruct((bsz, seq, width), BF16),
        scratch_shapes=[pltpu.VMEM((pw // LANES, LANES, LANES), F32)],
        compiler_params=_params("parallel", "parallel", "arbitrary"),
        name="rwkv_scan",
    )(r, k, v, lw, kk, a, g, row(r_k), row(gn_g), row(gn_b))


def _logf_cumsum_kernel(f_ref, b_ref, o_ref, carry):
    @pl.when(pl.program_id(1) == 0)
    def _():
        carry[...] = jnp.zeros_like(carry)

    z = f_ref[0] + b_ref[...]
    lf = jnp.minimum(z, 0.0) - jnp.log(1.0 + jnp.exp(-jnp.abs(z)))
    ts = z.shape[1]
    ri = lax.broadcasted_iota(jnp.int32, (LANES, LANES), 0)
    ci = lax.broadcasted_iota(jnp.int32, (LANES, LANES), 1)
    upper = (ri <= ci).astype(F32)
    run = carry[...]
    for j in range(ts // LANES):
        blk = jnp.dot(lf[:, j * LANES:(j + 1) * LANES], upper, preferred_element_type=F32,
                      precision=HIGHEST) + run
        o_ref[0, :, j * LANES:(j + 1) * LANES] = blk
        run = blk[:, LANES - 1:LANES]
    carry[...] = run


def _logf_cumsum(f_t, b_f):
    bsz, nh, seq = f_t.shape
    ts = _tile(seq, 2048, LANES)
    return pl.pallas_call(
        _logf_cumsum_kernel,
        grid=(bsz, seq // ts),
        in_specs=[pl.BlockSpec((1, nh, ts), lambda b, i: (b, 0, i)),
                  pl.BlockSpec((nh, 1), lambda b, i: (0, 0))],
        out_specs=pl.BlockSpec((1, nh, ts), lambda b, i: (b, 0, i)),
        out_shape=jax.ShapeDtypeStruct((bsz, nh, seq), F32),
        scratch_shapes=[pltpu.VMEM((nh, 1), F32)],
        compiler_params=_params("parallel", "arbitrary"),
        name="fox_logf_cumsum",
    )(f_t, b_f.reshape(nh, 1))


def _fox_kernel(qi_ref, ki_ref, q_ref, k_ref, v_ref, ck_ref, og_ref, o_ref,
                m_scr, l_scr, acc_scr, *, rows):
    p = pl.program_id(2)
    qi = qi_ref[p]
    ki = ki_ref[p]
    tq = q_ref.shape[1]
    tk = k_ref.shape[1]

    @pl.when(ki == 0)
    def _():
        m_scr[...] = jnp.full_like(m_scr, NEG_BIG)
        l_scr[...] = jnp.zeros_like(l_scr)
        acc_scr[...] = jnp.zeros_like(acc_scr)

    n_rc = tq // rows

    n_kq = tq // tk

    def step(diag):
        k = k_ref[0]
        v = v_ref[0]
        ck = ck_ref[0, 0] * LOG2E
        col0 = 0 if diag is None else diag * tk
        live = [rc for rc in range(n_rc) if diag is None or col0 < (rc + 1) * rows]

        def n_cols(rc):
            return tk if diag is None else min(tk, (rc + 1) * rows - col0)

        def logits(rc):
            kc = n_cols(rc)
            s = lax.dot_general(q_ref[0, pl.ds(rc * rows, rows), :], k[:kc], (((1,), (1,)), ((), ())),
                                preferred_element_type=F32) - ck[:, :kc]
            if diag is not None and col0 + kc - 1 > rc * rows:
                ri = lax.broadcasted_iota(jnp.int32, s.shape, 0) + rc * rows
                ci = lax.broadcasted_iota(jnp.int32, s.shape, 1) + col0
                s = jnp.where(ci <= ri, s, NEG_BIG)
            return s

        old = {rc: (m_scr[pl.ds(rc * rows, rows), :], l_scr[pl.ds(rc * rows, rows), :],
                    acc_scr[pl.ds(rc * rows, rows), :]) for rc in live}
        new = {}
        s_next = logits(live[0])
        for i, rc in enumerate(live):
            s = s_next
            if i + 1 < len(live):
                s_next = logits(live[i + 1])
            m_prev, l_prev, acc_prev = old[rc]
            tiles = [s[:, j * LANES:(j + 1) * LANES] for j in range(n_cols(rc) // LANES)]
            m_new = jnp.maximum(m_prev, jnp.max(functools.reduce(jnp.maximum, tiles),
                                                axis=-1, keepdims=True))
            alpha = jnp.exp2(m_prev - m_new)
            p_tiles = [jnp.exp2(t - m_new) for t in tiles]
            l_new = alpha * l_prev + functools.reduce(jnp.add, p_tiles)
            pr = jnp.concatenate([t.astype(BF16) for t in p_tiles], axis=1)
            acc_new = alpha * acc_prev + jnp.dot(pr, v[:n_cols(rc)], preferred_element_type=F32)
            new[rc] = (m_new, l_new, acc_new)
        for rc in live:
            rs = pl.ds(rc * rows, rows)
            m_scr[rs, :], l_scr[rs, :], acc_scr[rs, :] = new[rc]

    @pl.when(ki < qi * n_kq)
    def _():
        step(None)

    for diag in range(n_kq):
        @pl.when(ki == qi * n_kq + diag)
        def _(diag=diag):
            step(diag)

    @pl.when(ki == qi * n_kq + n_kq - 1)
    def _():
        o = acc_scr[...] / jnp.sum(l_scr[...], axis=-1, keepdims=True)
        o_ref[0] = (o * _sigmoid(og_ref[0])).astype(o_ref.dtype)


def _fox_attention(qk, v, cum, ogf, n_heads):
    bsz, seq, _ = v.shape
    tk = _tile(seq, FOX_TK, LANES)
    tq = _tile(seq, FOX_TQ, tk)
    qi_list, ki_list = [], []
    for qi in range(seq // tq):
        for ki in range((qi + 1) * (tq // tk)):
            qi_list.append(qi)
            ki_list.append(ki)
    qi_arr = jnp.asarray(qi_list, jnp.int32)
    ki_arr = jnp.asarray(ki_list, jnp.int32)
    cum_row = cum[:, :, None, :]
    nh = n_heads
    grid_spec = pltpu.PrefetchScalarGridSpec(
        num_scalar_prefetch=2,
        grid=(bsz, nh, len(qi_list)),
        in_specs=[pl.BlockSpec((1, tq, LANES), lambda b, h, p, qi, ki: (b, qi[p], h)),
                  pl.BlockSpec((1, tk, LANES), lambda b, h, p, qi, ki: (b, ki[p], nh + h)),
                  pl.BlockSpec((1, tk, LANES), lambda b, h, p, qi, ki: (b, ki[p], h)),
                  pl.BlockSpec((1, 1, 1, tk), lambda b, h, p, qi, ki: (b, h, 0, ki[p])),
                  pl.BlockSpec((1, tq, LANES), lambda b, h, p, qi, ki: (b, qi[p], h))],
        out_specs=pl.BlockSpec((1, tq, LANES), lambda b, h, p, qi, ki: (b, qi[p], h)),
        scratch_shapes=[pltpu.VMEM((tq, LANES), F32), pltpu.VMEM((tq, LANES), F32),
                        pltpu.VMEM((tq, LANES), F32)],
    )
    return pl.pallas_call(
        functools.partial(_fox_kernel, rows=_tile(tq, FOX_ROWS, LANES)),
        grid_spec=grid_spec,
        out_shape=jax.ShapeDtypeStruct((bsz, seq, nh * LANES), BF16),
        compiler_params=_params("parallel", "parallel", "arbitrary"),
        name="fox_attention",
    )(qi_arr, ki_arr, qk, qk, v, cum_row, ogf)


def _merge_kernel(ya_ref, yb_ref, ga_ref, gb_ref, wa_ref, wb_ref, o_ref):
    pa = jnp.dot(ya_ref[0], wa_ref[...], preferred_element_type=F32)
    pb = jnp.dot(yb_ref[0], wb_ref[...], preferred_element_type=F32)
    o_ref[0] = (ga_ref[0].astype(F32) * pa + gb_ref[0].astype(F32) * pb).astype(o_ref.dtype)


def _merge(y_a, y_b, gates, w_a, w_b):
    bsz, seq, wa = y_a.shape
    wb = y_b.shape[2]
    d = w_a.shape[1]
    tm = _tile(seq, 512)
    return pl.pallas_call(
        _merge_kernel,
        grid=(bsz, seq // tm),
        in_specs=[pl.BlockSpec((1, tm, wa), lambda b, i: (b, i, 0)),
                  pl.BlockSpec((1, tm, wb), lambda b, i: (b, i, 0)),
                  pl.BlockSpec((1, tm, d), lambda b, i: (b, i, 0)),
                  pl.BlockSpec((1, tm, d), lambda b, i: (b, i, 1)),
                  pl.BlockSpec((wa, d), lambda b, i: (0, 0)),
                  pl.BlockSpec((wb, d), lambda b, i: (0, 0))],
        out_specs=pl.BlockSpec((1, tm, d), lambda b, i: (b, i, 0)),
        out_shape=jax.ShapeDtypeStruct((bsz, seq, d), BF16),
        compiler_params=_params("parallel", "parallel"),
        name="branch_merge",
    )(y_a, y_b, gates, gates, w_a, w_b)


def _route_kernel(m_ref, x_ref, g1_ref, w_ref, n2_ref, sc_ref, sh_ref, wr_ref, br_ref,
                  x1_ref, route_ref, cnt_ref, base, *, n_experts):
    @pl.when(jnp.logical_and(pl.program_id(0) == 0, pl.program_id(1) == 0))
    def _():
        base[...] = jnp.zeros_like(base)

    x1 = x_ref[0] + g1_ref[0] * jnp.dot(m_ref[0], w_ref[...], preferred_element_type=F32)
    x1_ref[0] = x1
    h = _rms_mod(x1, n2_ref[...], sc_ref[0], sh_ref[0])
    h_hi = h.astype(BF16)
    h_lo = (h - h_hi.astype(F32)).astype(BF16)
    logits = jnp.dot(jnp.concatenate([h_hi, h_hi, h_lo], axis=1), wr_ref[...],
                     preferred_element_type=F32) + br_ref[...]
    tm = logits.shape[0]
    lane_i = lax.broadcasted_iota(jnp.int32, (tm, LANES), 1)
    lane = lane_i.astype(F32)
    vals = jnp.where(lane_i < n_experts, logits, -jnp.inf)
    top_v, top_i, hot = [], [], []
    for _ in range(TOP_K):
        mx = jnp.max(vals, axis=-1, keepdims=True)
        ix = jnp.min(jnp.where(vals == mx, lane, float(LANES)), axis=-1, keepdims=True)
        sel = lane == ix
        vals = jnp.where(sel, -jnp.inf, vals)
        top_v.append(mx)
        top_i.append(ix)
        hot.append(sel.astype(F32))
    ex = [jnp.exp(tv - top_v[0]) for tv in top_v]
    den = ex[0] + ex[1] + ex[2] + ex[3]
    cnt = hot[0] + hot[1] + hot[2] + hot[3]
    ri = lax.broadcasted_iota(jnp.int32, (tm, tm), 0)
    ci = lax.broadcasted_iota(jnp.int32, (tm, tm), 1)
    before = jnp.dot((ci < ri).astype(BF16), cnt.astype(BF16), preferred_element_type=F32)
    before = before + base[...]
    out = jnp.zeros((tm, LANES), F32)
    for kk in range(TOP_K):
        rank = jnp.sum(hot[kk] * before, axis=-1, keepdims=True)
        out = jnp.where(lane_i == kk, top_i[kk], out)
        out = jnp.where(lane_i == TOP_K + kk, ex[kk] / den, out)
        out = jnp.where(lane_i == 2 * TOP_K + kk, rank, out)
    route_ref[0] = out
    new_base = base[...] + jnp.sum(cnt, axis=0, keepdims=True)
    base[...] = new_base
    cnt_ref[...] = jnp.broadcast_to(new_base, cnt_ref.shape)


def _route(merged, x, gate1, w_out, norm2_g, scale2, shift2, w_router, b_router):
    bsz, seq, d = x.shape
    n_experts = w_router.shape[1]
    tm = _tile(seq, 512)
    wr = jnp.zeros((d, LANES), F32).at[:, :n_experts].set(w_router)
    wr_hi = wr.astype(BF16)
    wr_lo = (wr - wr_hi.astype(F32)).astype(BF16)
    wr = jnp.concatenate([wr_hi, wr_lo, wr_hi], axis=0)
    br = jnp.zeros((1, LANES), F32).at[0, :n_experts].set(b_router)
    mod = pl.BlockSpec((1, 1, d), lambda b, i: (b, 0, 0))
    tok = pl.BlockSpec((1, tm, d), lambda b, i: (b, i, 0))
    return pl.pallas_call(
        functools.partial(_route_kernel, n_experts=n_experts),
        grid=(bsz, seq // tm),
        in_specs=[tok, tok, mod,
                  pl.BlockSpec((d, d), lambda b, i: (0, 0)),
                  pl.BlockSpec((1, d), lambda b, i: (0, 0)), mod, mod,
                  pl.BlockSpec((3 * d, LANES), lambda b, i: (0, 0)),
                  pl.BlockSpec((1, LANES), lambda b, i: (0, 0))],
        out_specs=[tok,
                   pl.BlockSpec((1, tm, LANES), lambda b, i: (b, i, 0)),
                   pl.BlockSpec((8, LANES), lambda b, i: (0, 0))],
        out_shape=[jax.ShapeDtypeStruct((bsz, seq, d), F32),
                   jax.ShapeDtypeStruct((bsz, seq, LANES), F32),
                   jax.ShapeDtypeStruct((8, LANES), F32)],
        scratch_shapes=[pltpu.VMEM((1, LANES), F32)],
        compiler_params=_params("arbitrary", "arbitrary"),
        name="residual_router",
    )(merged, x, gate1, w_out, norm2_g.reshape(1, d), scale2, shift2, wr, br)


def _dispatch_kernel(slot_ref, x_ref, n2_ref, sc_ref, sh_ref, xs_in_ref, xs_ref, hbuf, sem):
    del xs_in_ref
    i = pl.program_id(0)
    tm = hbuf.shape[0]
    hbuf[...] = _rms_mod(x_ref[...], n2_ref[...], sc_ref[0], sh_ref[0])

    def row_copy(r, s):
        return pltpu.make_async_copy(hbuf.at[pl.ds(r, 1)], xs_ref.at[pl.ds(s, 1)], sem)

    def issue(r, carry):
        for kk in range(TOP_K):
            row_copy(r, slot_ref[(i * tm + r) * TOP_K + kk]).start()
        return carry

    lax.fori_loop(0, tm, issue, 0)

    def drain(r, carry):
        for kk in range(TOP_K):
            row_copy(0, 0).wait()
        return carry

    lax.fori_loop(0, tm, drain, 0)


def _dispatch(x1, slots, norm2_g, scale2, shift2, xs):
    n_tok, d = x1.shape
    seq = n_tok // scale2.shape[0]
    tm = _tile(seq, 256)
    per_b = seq // tm
    grid_spec = pltpu.PrefetchScalarGridSpec(
        num_scalar_prefetch=1,
        grid=(n_tok // tm,),
        in_specs=[pl.BlockSpec((tm, d), lambda i, s: (i, 0)),
                  pl.BlockSpec((1, d), lambda i, s: (0, 0)),
                  pl.BlockSpec((1, 1, d), lambda i, s: (i // per_b, 0, 0)),
                  pl.BlockSpec((1, 1, d), lambda i, s: (i // per_b, 0, 0)),
                  pl.BlockSpec(memory_space=pl.ANY)],
        out_specs=pl.BlockSpec(memory_space=pl.ANY),
        scratch_shapes=[pltpu.VMEM((tm, d), F32), pltpu.SemaphoreType.DMA(())],
    )
    return pl.pallas_call(
        _dispatch_kernel,
        grid_spec=grid_spec,
        out_shape=jax.ShapeDtypeStruct(xs.shape, xs.dtype),
        input_output_aliases={5: 0},
        compiler_params=_params("arbitrary"),
        name="moe_dispatch",
    )(slots, x1, norm2_g.reshape(1, d), scale2, shift2, xs)


def _expert_kernel(be_ref, na_ref, x_ref, wg_ref, wl_ref, bg_ref, bl_ref, wd_ref, bd_ref,
                   o_ref, acc, *, n_f):
    b = pl.program_id(0)
    f = pl.program_id(1)

    @pl.when(jnp.logical_and(b == 0, f == 0))
    def _():
        acc[...] = jnp.zeros_like(acc)

    @pl.when(b < na_ref[0])
    def _():
        x = x_ref[...].astype(BF16)
        gate = jnp.dot(x, wg_ref[0], preferred_element_type=F32) + bg_ref[0]
        lin = jnp.dot(x, wl_ref[0], preferred_element_type=F32) + bl_ref[0]
        gate = jnp.minimum(gate, SWIGLU_LIMIT)
        lin = jnp.clip(lin, -SWIGLU_LIMIT, SWIGLU_LIMIT)
        act = gate * _sigmoid(SWIGLU_ALPHA * gate) * (lin + 1.0)
        contrib = jnp.dot(act.astype(BF16), wd_ref[0], preferred_element_type=F32)
        if n_f == 1:
            o_ref[...] = contrib + bd_ref[0]
        else:
            total = jnp.where(f > 0, acc[...], 0.0) + contrib
            acc[...] = total
            o_ref[...] = total + bd_ref[0]

    @pl.when(jnp.logical_and(b >= na_ref[0], f == n_f - 1))
    def _():
        o_ref[...] = jnp.zeros_like(o_ref)


def _experts(xs, blk_e, n_active, w_gu, b_gu, w_dn, b_dn):
    n_slots, d = xs.shape
    n_e, _, two_ff = w_gu.shape
    d_ff = two_ff // 2
    bm = EXPERT_ROWS
    n_blocks = n_slots // bm
    tf = _tile(d_ff, EXPERT_FF_TILE, LANES)
    n_f = d_ff // tf

    def blk(b, na):
        return jnp.minimum(b, na[0] - 1)

    def ff(b, f, na):
        return jnp.where(b < na[0], f, n_f - 1)

    wmode = dict(pipeline_mode=pl.Buffered(1)) if n_f == 1 else {}
    grid_spec = pltpu.PrefetchScalarGridSpec(
        num_scalar_prefetch=2,
        grid=(n_blocks, n_f),
        in_specs=[pl.BlockSpec((bm, d), lambda b, f, be, na: (blk(b, na), 0)),
                  pl.BlockSpec((1, d, tf), lambda b, f, be, na: (be[blk(b, na)], 0, ff(b, f, na)), **wmode),
                  pl.BlockSpec((1, d, tf), lambda b, f, be, na: (be[blk(b, na)], 0, n_f + ff(b, f, na)),
                               **wmode),
                  pl.BlockSpec((1, 1, tf), lambda b, f, be, na: (be[blk(b, na)], 0, ff(b, f, na))),
                  pl.BlockSpec((1, 1, tf), lambda b, f, be, na: (be[blk(b, na)], 0, n_f + ff(b, f, na))),
                  pl.BlockSpec((1, tf, d), lambda b, f, be, na: (be[blk(b, na)], ff(b, f, na), 0), **wmode),
                  pl.BlockSpec((1, 1, d), lambda b, f, be, na: (be[blk(b, na)], 0, 0))],
        out_specs=pl.BlockSpec((bm, d), lambda b, f, be, na: (b, 0)),
        scratch_shapes=[pltpu.VMEM((bm, d) if n_f > 1 else (8, LANES), F32)],
    )
    return pl.pallas_call(
        functools.partial(_expert_kernel, n_f=n_f),
        grid_spec=grid_spec,
        out_shape=jax.ShapeDtypeStruct((n_slots, d), F32),
        compiler_params=_params("arbitrary", "arbitrary"),
        name="moe_experts",
    )(blk_e, n_active, xs, w_gu, w_gu, b_gu.reshape(n_e, 1, two_ff), b_gu.reshape(n_e, 1, two_ff),
      w_dn, b_dn.reshape(n_e, 1, d))


def _combine_kernel(slot_ref, x_ref, route_ref, g2_ref, gf_ref, ys_ref, o_ref, buf, sem):
    i = pl.program_id(0)
    tm = x_ref.shape[0]

    def row_copy(kk, r, s):
        return pltpu.make_async_copy(ys_ref.at[pl.ds(s, 1)], buf.at[kk, pl.ds(r, 1)], sem)

    def issue(r, carry):
        for kk in range(TOP_K):
            row_copy(kk, r, slot_ref[(i * tm + r) * TOP_K + kk]).start()
        return carry

    lax.fori_loop(0, tm, issue, 0)

    def drain(r, carry):
        for kk in range(TOP_K):
            row_copy(0, 0, 0).wait()
        return carry

    lax.fori_loop(0, tm, drain, 0)

    route = route_ref[...]
    y = jnp.zeros(x_ref.shape, F32)
    for kk in range(TOP_K):
        y = y + buf[kk] * route[:, TOP_K + kk:TOP_K + kk + 1]
    x2 = x_ref[...] + g2_ref[0] * y
    ms = jnp.mean(x2 * x2, axis=-1, keepdims=True)
    o_ref[...] = x2 * lax.rsqrt(ms + NORM_EPS) * gf_ref[...]


def _combine(x1, route, slots, gate2, norm_final_g, ys):
    n_tok, d = x1.shape
    seq = n_tok // gate2.shape[0]
    tm = _tile(seq, 256)
    per_b = seq // tm
    grid_spec = pltpu.PrefetchScalarGridSpec(
        num_scalar_prefetch=1,
        grid=(n_tok // tm,),
        in_specs=[pl.BlockSpec((tm, d), lambda i, s: (i, 0)),
                  pl.BlockSpec((tm, LANES), lambda i, s: (i, 0)),
                  pl.BlockSpec((1, 1, d), lambda i, s: (i // per_b, 0, 0)),
                  pl.BlockSpec((1, d), lambda i, s: (0, 0)),
                  pl.BlockSpec(memory_space=pl.ANY)],
        out_specs=pl.BlockSpec((tm, d), lambda i, s: (i, 0)),
        scratch_shapes=[pltpu.VMEM((TOP_K, tm, d), F32), pltpu.SemaphoreType.DMA(())],
    )
    return pl.pallas_call(
        _combine_kernel,
        grid_spec=grid_spec,
        out_shape=jax.ShapeDtypeStruct((n_tok, d), F32),
        compiler_params=_params("arbitrary"),
        name="moe_combine",
    )(slots, x1, route, gate2, norm_final_g.reshape(1, d), ys)


def _pad_cols(w, n):
    return jnp.pad(w, ((0, 0), (0, n - w.shape[1])))


def kernel(x, c, w_ada, b_ada, norm1_g, w_in, rwkv_mu, w_decay_up, decay_w0, w_iclr_up, iclr_a0, w_gate_up_rwkv, rwkv_k_k, rwkv_k_a, rwkv_r_k, rwkv_gn_g, rwkv_gn_b, w_out_a, fox_b_f, fox_q_norm, fox_k_norm, w_out_b, w_out, norm2_g, w_router, b_router, w_expert_gu, b_expert_gu, w_expert_down, b_expert_down, norm_final_g):
    bsz, seq, d = x.shape
    rw_heads, rw_hd = rwkv_r_k.shape
    rw = rw_heads * rw_hd
    dr, ir, gr = w_decay_up.shape[0], w_iclr_up.shape[0], w_gate_up_rwkv.shape[0]
    fh = fox_b_f.shape[0]
    fw = w_out_b.shape[0]
    fhd = fw // fh
    assert fhd == LANES and dr <= LANES and ir <= LANES and gr % LANES == 0
    n_experts = w_router.shape[1]

    mod = _ada(c, w_ada, b_ada)
    shift1, scale1, gate1, shift2, scale2, gate2 = (m[:, None, :] for m in jnp.split(mod, 6, axis=-1))

    o_r = 0
    o_f = 3 * rw + dr + ir + gr
    o_g = o_f + 3 * fw + fh + fw
    col = lambda a, n: w_in[:, a:a + n]
    w_rwkv = jnp.concatenate([col(0, 3 * rw), _pad_cols(col(3 * rw, dr), LANES),
                              _pad_cols(col(3 * rw + dr, ir), LANES), col(3 * rw + dr + ir, gr)],
                             axis=1).astype(BF16)
    mu = jnp.concatenate([rwkv_mu[:3 * rw], jnp.pad(rwkv_mu[3 * rw:3 * rw + dr], (0, LANES - dr)),
                          jnp.pad(rwkv_mu[3 * rw + dr:3 * rw + dr + ir], (0, LANES - ir)),
                          rwkv_mu[3 * rw + dr + ir:]])
    w_qk = col(o_f, 2 * fw).astype(BF16)
    w_v = col(o_f + 2 * fw, fw).astype(BF16)
    w_ogf = jnp.concatenate([col(o_f + 3 * fw + fh, fw), _pad_cols(col(o_f + 3 * fw, fh), LANES)],
                            axis=1).astype(BF16)
    w_gates = col(o_g, 2 * d).astype(BF16)
    qk_gain = jnp.concatenate([jnp.tile(fox_q_norm * (fhd ** -0.5 * LOG2E), fh), jnp.tile(fox_k_norm, fh)])

    def row(extra, n):
        return (jnp.zeros((n,), F32) if extra is None else extra).reshape(1, n)

    gates, h1 = _inproj(x, norm1_g, scale1, shift1, w_gates, row(None, w_gates.shape[1]), BF16,
                        _ep_sigmoid, "inproj_merge_gates")

    def proj(w, extra, dtype, ep, name):
        return _proj(h1, w, row(extra, w.shape[1]), dtype, ep, name)

    p_rwkv = proj(w_rwkv, None, F32, _ep_identity, "inproj_rwkv")
    qk = proj(w_qk, qk_gain, BF16, _ep_headnorm, "inproj_fox_qk")
    v_fox = proj(w_v, None, BF16, _ep_identity, "inproj_fox_v")
    ogf = proj(w_ogf, None, F32, _ep_identity, "inproj_fox_gate_forget")

    wd = jnp.pad(w_decay_up, ((0, LANES - dr), (0, 0)))
    wi = jnp.pad(w_iclr_up, ((0, LANES - ir), (0, 0)))
    r, k2, v, lw, kk, aic, g = _rwkv_prep(p_rwkv, mu, wd, decay_w0, wi, iclr_a0,
                                          w_gate_up_rwkv.astype(BF16), rwkv_k_k, rwkv_k_a, rw, gr)
    y_a = _rwkv_scan(r, k2, v, lw, kk, aic, g, rwkv_r_k, rwkv_gn_g, rwkv_gn_b, rw_hd)

    f_t = jnp.transpose(ogf[:, :, fw:fw + fh], (0, 2, 1))
    cum = _logf_cumsum(f_t, fox_b_f)
    y_b = _fox_attention(qk, v_fox, cum, ogf, fh)

    merged = _merge(y_a, y_b, gates, w_out_a.astype(BF16), w_out_b.astype(BF16))
    x1, route, counts = _route(merged, x, gate1, w_out.astype(BF16), norm2_g, scale2, shift2,
                               w_router, b_router)

    n_tok = bsz * seq
    n_assign = n_tok * TOP_K
    bm = EXPERT_ROWS
    n_blocks = -(-n_assign // bm) + n_experts
    cnt = counts[0, :n_experts].astype(jnp.int32)
    padded = (cnt + bm - 1) // bm * bm
    pad_end = jnp.cumsum(padded)
    pad_start = pad_end - padded
    top_i = route[:, :, 0:TOP_K].astype(jnp.int32)
    rank = route[:, :, 2 * TOP_K:3 * TOP_K].astype(jnp.int32)
    e_ids = jnp.arange(n_experts, dtype=jnp.int32)
    start_of = jnp.sum(jnp.where(top_i[..., None] == e_ids, pad_start, 0), axis=-1)
    slots = (start_of + rank).reshape(n_assign)
    blk_first = jnp.arange(n_blocks, dtype=jnp.int32) * bm
    blk_e = jnp.minimum(jnp.sum((pad_end[None, :] <= blk_first[:, None]).astype(jnp.int32), axis=1),
                        n_experts - 1)
    n_active = (pad_end[-1:] // bm).astype(jnp.int32)

    x1_tok = x1.reshape(n_tok, d)
    xs = _dispatch(x1_tok, slots, norm2_g, scale2, shift2, jnp.zeros((n_blocks * bm, d), F32))
    ys = _experts(xs, blk_e, n_active, w_expert_gu.astype(BF16), b_expert_gu,
                  w_expert_down.astype(BF16), b_expert_down)
    out = _combine(x1_tok, route.reshape(n_tok, LANES), slots, gate2, norm_final_g, ys)
    return out.reshape(bsz, seq, d)
```

```python
import functools

import jax
import jax.numpy as jnp
from jax import lax
from jax.experimental import pallas as pl
from jax.experimental.pallas import tpu as pltpu

F32 = jnp.float32
BF16 = jnp.bfloat16
HIGHEST = lax.Precision.HIGHEST

TOP_K = 4
NORM_EPS = 1e-6
GN_EPS = 64e-5
SWIGLU_LIMIT = 7.0
SWIGLU_ALPHA = 1.702
LANES = 128
RWKV_CHUNK = 64
RWKV_PAIRS = 4
VMEM_LIMIT_BYTES = 56 * 1024 * 1024
EXPERT_ROWS = 512
EXPERT_FF_TILE = 1024
NEG_BIG = -1e30
LOG2E = 1.4426950408889634
FOX_ROWS = 256
FOX_TQ = 2048
FOX_TK = 1024


def _tile(n, pref, mult=8):
    t = min(pref, n)
    t -= t % mult
    while t >= mult:
        if n % t == 0:
            return t
        t -= mult
    return n


def _params(*sem):
    return pltpu.CompilerParams(dimension_semantics=sem, vmem_limit_bytes=VMEM_LIMIT_BYTES)


def _sigmoid(x):
    return 1.0 / (1.0 + jnp.exp(-x))


def _rms_mod(x, g, scale, shift):
    ms = jnp.mean(x * x, axis=-1, keepdims=True)
    return x * lax.rsqrt(ms + NORM_EPS) * g * (1.0 + scale) + shift


def _ada_kernel(c_ref, w_ref, b_ref, o_ref):
    c = c_ref[...]
    s = c * _sigmoid(c)
    o_ref[...] = jnp.dot(s, w_ref[...], preferred_element_type=F32, precision=HIGHEST) + b_ref[...]


def _ada(c, w_ada, b_ada):
    bsz, d = c.shape
    n = w_ada.shape[1]
    rows = 8
    c_pad = jnp.zeros((rows, d), F32).at[:bsz].set(c)
    tn = _tile(n, 1024, LANES)
    out = pl.pallas_call(
        _ada_kernel,
        grid=(n // tn,),
        in_specs=[pl.BlockSpec((rows, d), lambda j: (0, 0)),
                  pl.BlockSpec((d, tn), lambda j: (0, j)),
                  pl.BlockSpec((1, tn), lambda j: (0, j))],
        out_specs=pl.BlockSpec((rows, tn), lambda j: (0, j)),
        out_shape=jax.ShapeDtypeStruct((rows, n), F32),
        compiler_params=_params("parallel"),
        name="adaln",
    )(c_pad, w_ada, b_ada.reshape(1, n))
    return out[:bsz]


def _inproj_kernel(x_ref, g_ref, sc_ref, sh_ref, w_ref, e_ref, o_ref, h_ref, *, epilogue):
    @pl.when(pl.program_id(2) == 0)
    def _():
        h = _rms_mod(x_ref[0], g_ref[...], sc_ref[0], sh_ref[0])
        h_ref[0] = h.astype(h_ref.dtype)

    acc = jnp.dot(h_ref[0], w_ref[...], preferred_element_type=F32)
    o_ref[0] = epilogue(acc, e_ref[...]).astype(o_ref.dtype)


def _proj_kernel(h_ref, w_ref, e_ref, o_ref, *, epilogue):
    acc = jnp.dot(h_ref[0], w_ref[...], preferred_element_type=F32)
    o_ref[0] = epilogue(acc, e_ref[...]).astype(o_ref.dtype)


def _ep_identity(acc, extra):
    return acc


def _ep_sigmoid(acc, extra):
    return _sigmoid(acc)


def _ep_headnorm(acc, extra):
    outs = []
    for h in range(acc.shape[1] // LANES):
        a = acc[:, h * LANES:(h + 1) * LANES]
        ms = jnp.mean(a * a, axis=-1, keepdims=True)
        outs.append(a * lax.rsqrt(ms + NORM_EPS))
    return jnp.concatenate(outs, axis=1) * extra


def _inproj(x, g, scale, shift, w, extra, out_dtype, epilogue, name):
    bsz, seq, d = x.shape
    n = w.shape[1]
    tm = _tile(seq, 1024)
    tn = _tile(n, 512, LANES)
    return pl.pallas_call(
        functools.partial(_inproj_kernel, epilogue=epilogue),
        grid=(bsz, seq // tm, n // tn),
        in_specs=[pl.BlockSpec((1, tm, d), lambda b, i, j: (b, i, 0)),
                  pl.BlockSpec((1, d), lambda b, i, j: (0, 0)),
                  pl.BlockSpec((1, 1, d), lambda b, i, j: (b, 0, 0)),
                  pl.BlockSpec((1, 1, d), lambda b, i, j: (b, 0, 0)),
                  pl.BlockSpec((d, tn), lambda b, i, j: (0, j)),
                  pl.BlockSpec((1, tn), lambda b, i, j: (0, j))],
        out_specs=[pl.BlockSpec((1, tm, tn), lambda b, i, j: (b, i, j)),
                   pl.BlockSpec((1, tm, d), lambda b, i, j: (b, i, 0))],
        out_shape=[jax.ShapeDtypeStruct((bsz, seq, n), out_dtype),
                   jax.ShapeDtypeStruct((bsz, seq, d), BF16)],
        compiler_params=_params("parallel", "parallel", "arbitrary"),
        name=name,
    )(x, g.reshape(1, d), scale, shift, w, extra)


def _proj(h, w, extra, out_dtype, epilogue, name):
    bsz, seq, d = h.shape
    n = w.shape[1]
    tm = _tile(seq, 1024)
    tn = _tile(n, 512, LANES)
    return pl.pallas_call(
        functools.partial(_proj_kernel, epilogue=epilogue),
        grid=(bsz, seq // tm, n // tn),
        in_specs=[pl.BlockSpec((1, tm, d), lambda b, i, j: (b, i, 0)),
                  pl.BlockSpec((d, tn), lambda b, i, j: (0, j)),
                  pl.BlockSpec((1, tn), lambda b, i, j: (0, j))],
        out_specs=pl.BlockSpec((1, tm, tn), lambda b, i, j: (b, i, j)),
        out_shape=jax.ShapeDtypeStruct((bsz, seq, n), out_dtype),
        compiler_params=_params("parallel", "parallel", "parallel"),
        name=name,
    )(h, w, extra)


def _rwkv_prep_kernel(p_ref, mu_ref, wd_ref, w0_ref, wi_ref, a0_ref, wg_ref, kk_ref, ka_ref,
                      r_o, k_o, v_o, lw_o, kk_o, a_o, g_o, carry, *, width, gate_rank):
    @pl.when(pl.program_id(1) == 0)
    def _():
        carry[...] = jnp.zeros_like(carry)

    p = p_ref[0]
    tt = p.shape[0]
    row = lax.broadcasted_iota(jnp.int32, p.shape, 0)
    prev = jnp.where(row == 0, carry[...], pltpu.roll(p, 1, axis=0))
    carry[...] = p[tt - 1:tt, :]
    pm = p + mu_ref[...] * (prev - p)
    w = width
    r = pm[:, 0:w]
    k = pm[:, w:2 * w]
    v = pm[:, 2 * w:3 * w]
    d_lo = pm[:, 3 * w:3 * w + LANES]
    a_lo = pm[:, 3 * w + LANES:3 * w + 2 * LANES]
    g_lo = pm[:, 3 * w + 2 * LANES:3 * w + 2 * LANES + gate_rank]
    w_pre = w0_ref[...] + jnp.dot(jnp.tanh(d_lo), wd_ref[...], preferred_element_type=F32,
                                  precision=HIGHEST)
    lw_o[0] = -jnp.exp(-0.5) * _sigmoid(w_pre)
    a = _sigmoid(a0_ref[...] + jnp.dot(a_lo, wi_ref[...], preferred_element_type=F32,
                                       precision=HIGHEST))
    g_o[0] = jnp.dot(_sigmoid(g_lo).astype(BF16), wg_ref[...], preferred_element_type=F32)
    r_o[0] = r
    v_o[0] = v
    a_o[0] = a
    kk_o[0] = k * kk_ref[...]
    k_o[0] = k * (1.0 + (a - 1.0) * ka_ref[...])


def _rwkv_prep(p, mu, wd, w0, wi, a0, wg, k_k, k_a, width, gate_rank):
    bsz, seq, n = p.shape
    tt = _tile(seq, 256)
    row = lambda arr: arr.reshape(1, -1)
    full = lambda shape: pl.BlockSpec(shape, lambda b, i: (0,) * len(shape))
    out_sds = jax.ShapeDtypeStruct((bsz, seq, width), F32)
    out_spec = pl.BlockSpec((1, tt, width), lambda b, i: (b, i, 0))
    return pl.pallas_call(
        functools.partial(_rwkv_prep_kernel, width=width, gate_rank=gate_rank),
        grid=(bsz, seq // tt),
        in_specs=[pl.BlockSpec((1, tt, n), lambda b, i: (b, i, 0)),
                  full((1, n)), full(wd.shape), full((1, width)), full(wi.shape), full((1, width)),
                  full(wg.shape), full((1, width)), full((1, width))],
        out_specs=[out_spec] * 7,
        out_shape=[out_sds] * 7,
        scratch_shapes=[pltpu.VMEM((1, n), F32)],
        compiler_params=_params("parallel", "arbitrary"),
        name="rwkv_prep",
    )(p, row(mu), wd, row(w0), wi, row(a0), wg, row(k_k), row(k_a))


def _rwkv_scan_kernel(r_ref, k_ref, v_ref, lw_ref, kk_ref, a_ref, g_ref, rk_ref, gg_ref, gb_ref,
                      o_ref, s_scr, *, n_chunks, head_dim):
    L = RWKV_CHUNK
    L2 = 2 * L

    @pl.when(pl.program_id(2) == 0)
    def _():
        s_scr[...] = jnp.zeros_like(s_scr)

    lane = lax.broadcasted_iota(jnp.int32, (1, LANES), 1)
    m0 = (lane < head_dim).astype(F32)
    m1 = 1.0 - m0
    r2 = lax.broadcasted_iota(jnp.int32, (L2, L2), 0)
    c2 = lax.broadcasted_iota(jnp.int32, (L2, L2), 1)
    same = (r2 < L) == (c2 < L)
    strict = jnp.logical_and(same, c2 < r2)
    incl = jnp.logical_and(same, c2 <= r2)
    eye = (r2 == c2).astype(F32)
    rowi = lax.broadcasted_iota(jnp.int32, (L, LANES), 0)
    n_pairs = r_ref.shape[2] // LANES

    def stack_f32(x):
        return jnp.concatenate([x * m0, x * m1], axis=0)

    def stack(x):
        return stack_f32(x).astype(BF16)

    def head_sum(x):
        s0 = jnp.sum(x * m0, axis=-1, keepdims=True)
        s1 = jnp.sum(x * m1, axis=-1, keepdims=True)
        return s0 * m0 + s1 * m1

    def nt(a, b):
        return lax.dot_general(a, b, (((1,), (1,)), ((), ())), preferred_element_type=F32)

    def mm(a, b):
        return jnp.dot(a.astype(BF16), b.astype(BF16), preferred_element_type=F32)

    blk_masks = []
    size = 8
    while size <= L:
        blk_masks.append(jnp.bitwise_xor(r2, c2) < size)
        size *= 2
    off_masks = [jnp.logical_and(hi, jnp.logical_not(lo))
                 for lo, hi in zip(blk_masks[:-1], blk_masks[1:])]
    chunks = range(n_chunks)

    ctx = []
    for c, p in [(c, p) for c in chunks for p in range(n_pairs)]:
        sl = pl.ds(c * L, L)
        ln = pl.ds(p * LANES, LANES)
        r = r_ref[0, sl, ln]
        k = k_ref[0, sl, ln]
        v = v_ref[0, sl, ln]
        lw = lw_ref[0, sl, ln]
        kk = kk_ref[0, sl, ln]
        aic = a_ref[0, sl, ln]
        rk = rk_ref[:, ln]
        cum = lw
        sh = 1
        while sh < L:
            cum = cum + jnp.where(rowi >= sh, pltpu.roll(cum, sh, axis=0), 0.0)
            sh *= 2
        tot = cum[L - 1:L, :]
        w_rem = jnp.exp(tot - cum)
        w_inv = jnp.exp(-cum)
        kk = kk / jnp.maximum(jnp.sqrt(head_sum(kk * kk)), 1e-12)
        b_vec = kk * aic
        ar = jnp.concatenate([stack(-kk * jnp.exp(cum - lw)), stack(r * jnp.exp(cum))], axis=0)
        bk = jnp.concatenate([stack(b_vec * w_inv), stack(k * w_inv)], axis=0)
        v_st = stack(v)
        b_rem_t = stack_f32(b_vec * w_rem).T.astype(BF16)
        k_rem_t = stack_f32(k * w_rem).T.astype(BF16)
        decay = jnp.broadcast_to(jnp.exp(tot), (LANES, LANES)).T
        ctx.append(dict(sl=sl, ln=ln, p=p, ar=ar, bk=bk, v_st=v_st, decay=decay, b_rem_t=b_rem_t,
                        k_rem_t=k_rem_t, bonus=head_sum(r * k * rk) * v))

    for x in ctx:
        big = nt(x["ar"], x["bk"])
        x["n_ab"] = jnp.where(strict, big[0:L2, 0:L2], 0.0)
        m_ak = jnp.where(strict, big[0:L2, L2:2 * L2], 0.0)
        m_rk = jnp.where(incl, big[L2:2 * L2, L2:2 * L2], 0.0)
        x["m_rb"] = jnp.where(incl, big[L2:2 * L2, 0:L2], 0.0).astype(BF16)
        x["m_akrk"] = jnp.concatenate([m_ak, m_rk], axis=0).astype(BF16)
    for x in ctx:
        x["m_v"] = mm(x["m_akrk"], x["v_st"])
        x["kv"] = mm(x["k_rem_t"], x["v_st"])

    for x in ctx:
        x["d8"] = jnp.where(blk_masks[0], x["n_ab"], 0.0)
        x["t"] = eye + x["d8"]
    for x in ctx:
        x["pw"] = mm(x["d8"], x["d8"])
    for x in ctx:
        x["t"] = x["t"] + mm(x["t"], x["pw"])
    for x in ctx:
        x["pw"] = mm(x["pw"], x["pw"])
    for x in ctx:
        x["t"] = x["t"] + mm(x["t"], x["pw"])
    for off in off_masks:
        for x in ctx:
            x["nt"] = mm(jnp.where(off, x["n_ab"], 0.0), x["t"])
        for x in ctx:
            x["t"] = x["t"] + mm(x["t"], x["nt"])

    for x in ctx:
        rhs = jnp.concatenate([x["ar"][0:L2], x["m_v"][0:L2].astype(BF16)], axis=1)
        x["ta_uv"] = mm(x["t"], rhs).astype(BF16)
    for x in ctx:
        w_q = mm(x["b_rem_t"], x["ta_uv"])
        g_y = mm(x["m_rb"], x["ta_uv"])
        x["q"] = w_q[:, LANES:] + x["kv"]
        x["y_loc"] = g_y[:, LANES:] + x["m_v"][L2:2 * L2]
        g = g_y[:, :LANES] + x["ar"][L2:2 * L2].astype(F32)
        x["wg"] = jnp.concatenate([w_q[:, :LANES], g], axis=0).astype(BF16)

    s = [s_scr[p] for p in range(n_pairs)]
    for x in ctx:
        p = x["p"]
        ws = mm(x["wg"], s[p])
        x["y_st"] = ws[L2:2 * L2] + x["y_loc"]
        s[p] = x["decay"] * s[p] + ws[0:L2] + x["q"]
    for p in range(n_pairs):
        s_scr[p] = s[p]

    inv_n = 1.0 / head_dim
    for x in ctx:
        y_st = x["y_st"]
        y = y_st[0:L] + y_st[L:L2]
        mean = head_sum(y) * inv_n
        yc = y - mean
        var = head_sum(yc * yc) * inv_n
        y_gn = yc * lax.rsqrt(var + GN_EPS) * gg_ref[:, x["ln"]] + gb_ref[:, x["ln"]]
        o_ref[0, x["sl"], x["ln"]] = ((y_gn + x["bonus"]) * g_ref[0, x["sl"], x["ln"]]).astype(o_ref.dtype)


def _rwkv_scan(r, k, v, lw, kk, a, g, r_k, gn_g, gn_b, head_dim):
    bsz, seq, width = r.shape
    assert 2 * head_dim == LANES and width % LANES == 0
    tc = _tile(seq, 8 * RWKV_CHUNK, RWKV_CHUNK)
    pw = _tile(width, RWKV_PAIRS * LANES, LANES)
    tok = pl.BlockSpec((1, tc, pw), lambda b, h, i: (b, i, h))
    par = pl.BlockSpec((1, pw), lambda b, h, i: (0, h))
    row = lambda arr: arr.reshape(1, width)
    return pl.pallas_call(
        functools.partial(_rwkv_scan_kernel, n_chunks=tc // RWKV_CHUNK, head_dim=head_dim),
        grid=(bsz, width // pw, seq // tc),
        in_specs=[tok] * 7 + [par] * 3,
        out_specs=tok,
        out_shape=jax.ShapeDtypeStruct((bsz, seq, width), BF16),
        scratch_shapes=[pltpu.VMEM((pw // LANES, LANES, LANES), F32)],
        compiler_params=_params("parallel", "parallel", "arbitrary"),
        name="rwkv_scan",
    )(r, k, v, lw, kk, a, g, row(r_k), row(gn_g), row(gn_b))


def _logf_cumsum_kernel(f_ref, b_ref, o_ref, carry):
    @pl.when(pl.program_id(1) == 0)
    def _():
        carry[...] = jnp.zeros_like(carry)

    z = f_ref[0] + b_ref[...]
    lf = jnp.minimum(z, 0.0) - jnp.log(1.0 + jnp.exp(-jnp.abs(z)))
    ts = z.shape[1]
    ri = lax.broadcasted_iota(jnp.int32, (LANES, LANES), 0)
    ci = lax.broadcasted_iota(jnp.int32, (LANES, LANES), 1)
    upper = (ri <= ci).astype(F32)
    run = carry[...]
    for j in range(ts // LANES):
        blk = jnp.dot(lf[:, j * LANES:(j + 1) * LANES], upper, preferred_element_type=F32,
                      precision=HIGHEST) + run
        o_ref[0, :, j * LANES:(j + 1) * LANES] = blk
        run = blk[:, LANES - 1:LANES]
    carry[...] = run


def _logf_cumsum(f_t, b_f):
    bsz, nh, seq = f_t.shape
    ts = _tile(seq, 2048, LANES)
    return pl.pallas_call(
        _logf_cumsum_kernel,
        grid=(bsz, seq // ts),
        in_specs=[pl.BlockSpec((1, nh, ts), lambda b, i: (b, 0, i)),
                  pl.BlockSpec((nh, 1), lambda b, i: (0, 0))],
        out_specs=pl.BlockSpec((1, nh, ts), lambda b, i: (b, 0, i)),
        out_shape=jax.ShapeDtypeStruct((bsz, nh, seq), F32),
        scratch_shapes=[pltpu.VMEM((nh, 1), F32)],
        compiler_params=_params("parallel", "arbitrary"),
        name="fox_logf_cumsum",
    )(f_t, b_f.reshape(nh, 1))


def _fox_kernel(qi_ref, ki_ref, q_ref, k_ref, v_ref, ck_ref, og_ref, o_ref,
                m_scr, l_scr, acc_scr, *, rows):
    p = pl.program_id(2)
    qi = qi_ref[p]
    ki = ki_ref[p]
    tq = q_ref.shape[1]
    tk = k_ref.shape[1]

    @pl.when(ki == 0)
    def _():
        m_scr[...] = jnp.full_like(m_scr, NEG_BIG)
        l_scr[...] = jnp.zeros_like(l_scr)
        acc_scr[...] = jnp.zeros_like(acc_scr)

    n_rc = tq // rows

    n_kq = tq // tk

    def step(diag):
        k = k_ref[0]
        v = v_ref[0]
        ck = ck_ref[0, 0] * LOG2E
        col0 = 0 if diag is None else diag * tk
        live = [rc for rc in range(n_rc) if diag is None or col0 < (rc + 1) * rows]

        def n_cols(rc):
            return tk if diag is None else min(tk, (rc + 1) * rows - col0)

        def logits(rc):
            kc = n_cols(rc)
            s = lax.dot_general(q_ref[0, pl.ds(rc * rows, rows), :], k[:kc], (((1,), (1,)), ((), ())),
                                preferred_element_type=F32) - ck[:, :kc]
            if diag is not None and col0 + kc - 1 > rc * rows:
                ri = lax.broadcasted_iota(jnp.int32, s.shape, 0) + rc * rows
                ci = lax.broadcasted_iota(jnp.int32, s.shape, 1) + col0
                s = jnp.where(ci <= ri, s, NEG_BIG)
            return s

        old = {rc: (m_scr[pl.ds(rc * rows, rows), :], l_scr[pl.ds(rc * rows, rows), :],
                    acc_scr[pl.ds(rc * rows, rows), :]) for rc in live}
        new = {}
        s_next = logits(live[0])
        for i, rc in enumerate(live):
            s = s_next
            if i + 1 < len(live):
                s_next = logits(live[i + 1])
            m_prev, l_prev, acc_prev = old[rc]
            tiles = [s[:, j * LANES:(j + 1) * LANES] for j in range(n_cols(rc) // LANES)]
            m_new = jnp.maximum(m_prev, jnp.max(functools.reduce(jnp.maximum, tiles),
                                                axis=-1, keepdims=True))
            alpha = jnp.exp2(m_prev - m_new)
            p_tiles = [jnp.exp2(t - m_new) for t in tiles]
            l_new = alpha * l_prev + functools.reduce(jnp.add, p_tiles)
            pr = jnp.concatenate([t.astype(BF16) for t in p_tiles], axis=1)
            acc_new = alpha * acc_prev + jnp.dot(pr, v[:n_cols(rc)], preferred_element_type=F32)
            new[rc] = (m_new, l_new, acc_new)
        for rc in live:
            rs = pl.ds(rc * rows, rows)
            m_scr[rs, :], l_scr[rs, :], acc_scr[rs, :] = new[rc]

    @pl.when(ki < qi * n_kq)
    def _():
        step(None)

    for diag in range(n_kq):
        @pl.when(ki == qi * n_kq + diag)
        def _(diag=diag):
            step(diag)

    @pl.when(ki == qi * n_kq + n_kq - 1)
    def _():
        o = acc_scr[...] / jnp.sum(l_scr[...], axis=-1, keepdims=True)
        o_ref[0] = (o * _sigmoid(og_ref[0])).astype(o_ref.dtype)


def _fox_attention(qk, v, cum, ogf, n_heads):
    bsz, seq, _ = v.shape
    tk = _tile(seq, FOX_TK, LANES)
    tq = _tile(seq, FOX_TQ, tk)
    qi_list, ki_list = [], []
    for qi in range(seq // tq):
        for ki in range((qi + 1) * (tq // tk)):
            qi_list.append(qi)
            ki_list.append(ki)
    qi_arr = jnp.asarray(qi_list, jnp.int32)
    ki_arr = jnp.asarray(ki_list, jnp.int32)
    cum_row = cum[:, :, None, :]
    nh = n_heads
    grid_spec = pltpu.PrefetchScalarGridSpec(
        num_scalar_prefetch=2,
        grid=(bsz, nh, len(qi_list)),
        in_specs=[pl.BlockSpec((1, tq, LANES), lambda b, h, p, qi, ki: (b, qi[p], h)),
                  pl.BlockSpec((1, tk, LANES), lambda b, h, p, qi, ki: (b, ki[p], nh + h)),
                  pl.BlockSpec((1, tk, LANES), lambda b, h, p, qi, ki: (b, ki[p], h)),
                  pl.BlockSpec((1, 1, 1, tk), lambda b, h, p, qi, ki: (b, h, 0, ki[p])),
                  pl.BlockSpec((1, tq, LANES), lambda b, h, p, qi, ki: (b, qi[p], h))],
        out_specs=pl.BlockSpec((1, tq, LANES), lambda b, h, p, qi, ki: (b, qi[p], h)),
        scratch_shapes=[pltpu.VMEM((tq, LANES), F32), pltpu.VMEM((tq, LANES), F32),
                        pltpu.VMEM((tq, LANES), F32)],
    )
    return pl.pallas_call(
        functools.partial(_fox_kernel, rows=_tile(tq, FOX_ROWS, LANES)),
        grid_spec=grid_spec,
        out_shape=jax.ShapeDtypeStruct((bsz, seq, nh * LANES), BF16),
        compiler_params=_params("parallel", "parallel", "arbitrary"),
        name="fox_attention",
    )(qi_arr, ki_arr, qk, qk, v, cum_row, ogf)


def _merge_kernel(ya_ref, yb_ref, ga_ref, gb_ref, wa_ref, wb_ref, o_ref):
    pa = jnp.dot(ya_ref[0], wa_ref[...], preferred_element_type=F32)
    pb = jnp.dot(yb_ref[0], wb_ref[...], preferred_element_type=F32)
    o_ref[0] = (ga_ref[0].astype(F32) * pa + gb_ref[0].astype(F32) * pb).astype(o_ref.dtype)


def _merge(y_a, y_b, gates, w_a, w_b):
    bsz, seq, wa = y_a.shape
    wb = y_b.shape[2]
    d = w_a.shape[1]
    tm = _tile(seq, 512)
    return pl.pallas_call(
        _merge_kernel,
        grid=(bsz, seq // tm),
        in_specs=[pl.BlockSpec((1, tm, wa), lambda b, i: (b, i, 0)),
                  pl.BlockSpec((1, tm, wb), lambda b, i: (b, i, 0)),
                  pl.BlockSpec((1, tm, d), lambda b, i: (b, i, 0)),
                  pl.BlockSpec((1, tm, d), lambda b, i: (b, i, 1)),
                  pl.BlockSpec((wa, d), lambda b, i: (0, 0)),
                  pl.BlockSpec((wb, d), lambda b, i: (0, 0))],
        out_specs=pl.BlockSpec((1, tm, d), lambda b, i: (b, i, 0)),
        out_shape=jax.ShapeDtypeStruct((bsz, seq, d), BF16),
        compiler_params=_params("parallel", "parallel"),
        name="branch_merge",
    )(y_a, y_b, gates, gates, w_a, w_b)


def _route_kernel(m_ref, x_ref, g1_ref, w_ref, n2_ref, sc_ref, sh_ref, wr_ref, br_ref,
                  x1_ref, route_ref, cnt_ref, base, *, n_experts):
    @pl.when(jnp.logical_and(pl.program_id(0) == 0, pl.program_id(1) == 0))
    def _():
        base[...] = jnp.zeros_like(base)

    x1 = x_ref[0] + g1_ref[0] * jnp.dot(m_ref[0], w_ref[...], preferred_element_type=F32)
    x1_ref[0] = x1
    h = _rms_mod(x1, n2_ref[...], sc_ref[0], sh_ref[0])
    h_hi = h.astype(BF16)
    h_lo = (h - h_hi.astype(F32)).astype(BF16)
    logits = jnp.dot(jnp.concatenate([h_hi, h_hi, h_lo], axis=1), wr_ref[...],
                     preferred_element_type=F32) + br_ref[...]
    tm = logits.shape[0]
    lane_i = lax.broadcasted_iota(jnp.int32, (tm, LANES), 1)
    lane = lane_i.astype(F32)
    vals = jnp.where(lane_i < n_experts, logits, -jnp.inf)
    top_v, top_i, hot = [], [], []
    for _ in range(TOP_K):
        mx = jnp.max(vals, axis=-1, keepdims=True)
        ix = jnp.min(jnp.where(vals == mx, lane, float(LANES)), axis=-1, keepdims=True)
        sel = lane == ix
        vals = jnp.where(sel, -jnp.inf, vals)
        top_v.append(mx)
        top_i.append(ix)
        hot.append(sel.astype(F32))
    ex = [jnp.exp(tv - top_v[0]) for tv in top_v]
    den = ex[0] + ex[1] + ex[2] + ex[3]
    cnt = hot[0] + hot[1] + hot[2] + hot[3]
    ri = lax.broadcasted_iota(jnp.int32, (tm, tm), 0)
    ci = lax.broadcasted_iota(jnp.int32, (tm, tm), 1)
    before = jnp.dot((ci < ri).astype(BF16), cnt.astype(BF16), preferred_element_type=F32)
    before = before + base[...]
    out = jnp.zeros((tm, LANES), F32)
    for kk in range(TOP_K):
        rank = jnp.sum(hot[kk] * before, axis=-1, keepdims=True)
        out = jnp.where(lane_i == kk, top_i[kk], out)
        out = jnp.where(lane_i == TOP_K + kk, ex[kk] / den, out)
        out = jnp.where(lane_i == 2 * TOP_K + kk, rank, out)
    route_ref[0] = out
    new_base = base[...] + jnp.sum(cnt, axis=0, keepdims=True)
    base[...] = new_base
    cnt_ref[...] = jnp.broadcast_to(new_base, cnt_ref.shape)


def _route(merged, x, gate1, w_out, norm2_g, scale2, shift2, w_router, b_router):
    bsz, seq, d = x.shape
    n_experts = w_router.shape[1]
    tm = _tile(seq, 512)
    wr = jnp.zeros((d, LANES), F32).at[:, :n_experts].set(w_router)
    wr_hi = wr.astype(BF16)
    wr_lo = (wr - wr_hi.astype(F32)).astype(BF16)
    wr = jnp.concatenate([wr_hi, wr_lo, wr_hi], axis=0)
    br = jnp.zeros((1, LANES), F32).at[0, :n_experts].set(b_router)
    mod = pl.BlockSpec((1, 1, d), lambda b, i: (b, 0, 0))
    tok = pl.BlockSpec((1, tm, d), lambda b, i: (b, i, 0))
    return pl.pallas_call(
        functools.partial(_route_kernel, n_experts=n_experts),
        grid=(bsz, seq // tm),
        in_specs=[tok, tok, mod,
                  pl.BlockSpec((d, d), lambda b, i: (0, 0)),
                  pl.BlockSpec((1, d), lambda b, i: (0, 0)), mod, mod,
                  pl.BlockSpec((3 * d, LANES), lambda b, i: (0, 0)),
                  pl.BlockSpec((1, LANES), lambda b, i: (0, 0))],
        out_specs=[tok,
                   pl.BlockSpec((1, tm, LANES), lambda b, i: (b, i, 0)),
                   pl.BlockSpec((8, LANES), lambda b, i: (0, 0))],
        out_shape=[jax.ShapeDtypeStruct((bsz, seq, d), F32),
                   jax.ShapeDtypeStruct((bsz, seq, LANES), F32),
                   jax.ShapeDtypeStruct((8, LANES), F32)],
        scratch_shapes=[pltpu.VMEM((1, LANES), F32)],
        compiler_params=_params("arbitrary", "arbitrary"),
        name="residual_router",
    )(merged, x, gate1, w_out, norm2_g.reshape(1, d), scale2, shift2, wr, br)


def _dispatch_kernel(slot_ref, zblk_ref, x_ref, n2_ref, sc_ref, sh_ref, xs_ref, hbuf, zbuf, sem, zsem,
                     *, n_zero):
    i = pl.program_id(0)
    tm = hbuf.shape[0]
    bm = zbuf.shape[0]

    @pl.when(i == 0)
    def _():
        zbuf[...] = jnp.zeros_like(zbuf)

        def zero_copy(j):
            start = pl.multiple_of(zblk_ref[j] * bm, bm)
            return pltpu.make_async_copy(zbuf, xs_ref.at[pl.ds(start, bm)], zsem)

        for j in range(n_zero):
            @pl.when(zblk_ref[j] >= 0)
            def _(j=j):
                zero_copy(j).start()

        for j in range(n_zero):
            @pl.when(zblk_ref[j] >= 0)
            def _(j=j):
                zero_copy(j).wait()

    hbuf[...] = _rms_mod(x_ref[...], n2_ref[...], sc_ref[0], sh_ref[0])

    def row_copy(r, s):
        return pltpu.make_async_copy(hbuf.at[pl.ds(r, 1)], xs_ref.at[pl.ds(s, 1)], sem)

    def issue(r, carry):
        for kk in range(TOP_K):
            row_copy(r, slot_ref[(i * tm + r) * TOP_K + kk]).start()
        return carry

    lax.fori_loop(0, tm, issue, 0)

    def drain(r, carry):
        for kk in range(TOP_K):
            row_copy(0, 0).wait()
        return carry

    lax.fori_loop(0, tm, drain, 0)


def _dispatch(x1, slots, zero_blocks, norm2_g, scale2, shift2, n_slots):
    n_tok, d = x1.shape
    seq = n_tok // scale2.shape[0]
    tm = _tile(seq, 256)
    per_b = seq // tm
    grid_spec = pltpu.PrefetchScalarGridSpec(
        num_scalar_prefetch=2,
        grid=(n_tok // tm,),
        in_specs=[pl.BlockSpec((tm, d), lambda i, s, z: (i, 0)),
                  pl.BlockSpec((1, d), lambda i, s, z: (0, 0)),
                  pl.BlockSpec((1, 1, d), lambda i, s, z: (i // per_b, 0, 0)),
                  pl.BlockSpec((1, 1, d), lambda i, s, z: (i // per_b, 0, 0))],
        out_specs=pl.BlockSpec(memory_space=pl.ANY),
        scratch_shapes=[pltpu.VMEM((tm, d), F32), pltpu.VMEM((EXPERT_ROWS, d), F32),
                        pltpu.SemaphoreType.DMA(()), pltpu.SemaphoreType.DMA(())],
    )
    return pl.pallas_call(
        functools.partial(_dispatch_kernel, n_zero=zero_blocks.shape[0]),
        grid_spec=grid_spec,
        out_shape=jax.ShapeDtypeStruct((n_slots, d), F32),
        compiler_params=_params("arbitrary"),
        name="moe_dispatch",
    )(slots, zero_blocks, x1, norm2_g.reshape(1, d), scale2, shift2)


def _expert_kernel(be_ref, na_ref, x_ref, wg_ref, wl_ref, bg_ref, bl_ref, wd_ref, bd_ref,
                   o_ref, acc, *, n_f):
    b = pl.program_id(0)
    f = pl.program_id(1)

    @pl.when(jnp.logical_and(b == 0, f == 0))
    def _():
        acc[...] = jnp.zeros_like(acc)

    @pl.when(b < na_ref[0])
    def _():
        x = x_ref[...].astype(BF16)
        gate = jnp.dot(x, wg_ref[0], preferred_element_type=F32) + bg_ref[0]
        lin = jnp.dot(x, wl_ref[0], preferred_element_type=F32) + bl_ref[0]
        gate = jnp.minimum(gate, SWIGLU_LIMIT)
        lin = jnp.clip(lin, -SWIGLU_LIMIT, SWIGLU_LIMIT)
        act = gate * _sigmoid(SWIGLU_ALPHA * gate) * (lin + 1.0)
        contrib = jnp.dot(act.astype(BF16), wd_ref[0], preferred_element_type=F32)
        if n_f == 1:
            o_ref[...] = contrib + bd_ref[0]
        else:
            total = jnp.where(f > 0, acc[...], 0.0) + contrib
            acc[...] = total
            o_ref[...] = total + bd_ref[0]

    @pl.when(jnp.logical_and(b >= na_ref[0], f == n_f - 1))
    def _():
        o_ref[...] = jnp.zeros_like(o_ref)


def _experts(xs, blk_e, n_active, w_gu, b_gu, w_dn, b_dn):
    n_slots, d = xs.shape
    n_e, _, two_ff = w_gu.shape
    d_ff = two_ff // 2
    bm = EXPERT_ROWS
    n_blocks = n_slots // bm
    tf = _tile(d_ff, EXPERT_FF_TILE, LANES)
    n_f = d_ff // tf

    def blk(b, na):
        return jnp.minimum(b, na[0] - 1)

    def ff(b, f, na):
        return jnp.where(b < na[0], f, n_f - 1)

    wmode = dict(pipeline_mode=pl.Buffered(1)) if n_f == 1 else {}
    grid_spec = pltpu.PrefetchScalarGridSpec(
        num_scalar_prefetch=2,
        grid=(n_blocks, n_f),
        in_specs=[pl.BlockSpec((bm, d), lambda b, f, be, na: (blk(b, na), 0)),
                  pl.BlockSpec((1, d, tf), lambda b, f, be, na: (be[blk(b, na)], 0, ff(b, f, na)), **wmode),
                  pl.BlockSpec((1, d, tf), lambda b, f, be, na: (be[blk(b, na)], 0, n_f + ff(b, f, na)),
                               **wmode),
                  pl.BlockSpec((1, 1, tf), lambda b, f, be, na: (be[blk(b, na)], 0, ff(b, f, na))),
                  pl.BlockSpec((1, 1, tf), lambda b, f, be, na: (be[blk(b, na)], 0, n_f + ff(b, f, na))),
                  pl.BlockSpec((1, tf, d), lambda b, f, be, na: (be[blk(b, na)], ff(b, f, na), 0), **wmode),
                  pl.BlockSpec((1, 1, d), lambda b, f, be, na: (be[blk(b, na)], 0, 0))],
        out_specs=pl.BlockSpec((bm, d), lambda b, f, be, na: (b, 0)),
        scratch_shapes=[pltpu.VMEM((bm, d) if n_f > 1 else (8, LANES), F32)],
    )
    return pl.pallas_call(
        functools.partial(_expert_kernel, n_f=n_f),
        grid_spec=grid_spec,
        out_shape=jax.ShapeDtypeStruct((n_slots, d), F32),
        compiler_params=_params("arbitrary", "arbitrary"),
        name="moe_experts",
    )(blk_e, n_active, xs, w_gu, w_gu, b_gu.reshape(n_e, 1, two_ff), b_gu.reshape(n_e, 1, two_ff),
      w_dn, b_dn.reshape(n_e, 1, d))


def _combine_kernel(slot_ref, x_ref, route_ref, g2_ref, gf_ref, ys_ref, o_ref, buf, sem):
    i = pl.program_id(0)
    tm = x_ref.shape[0]

    def row_copy(kk, r, s):
        return pltpu.make_async_copy(ys_ref.at[pl.ds(s, 1)], buf.at[kk, pl.ds(r, 1)], sem)

    def issue(r, carry):
        for kk in range(TOP_K):
            row_copy(kk, r, slot_ref[(i * tm + r) * TOP_K + kk]).start()
        return carry

    lax.fori_loop(0, tm, issue, 0)

    def drain(r, carry):
        for kk in range(TOP_K):
            row_copy(0, 0, 0).wait()
        return carry

    lax.fori_loop(0, tm, drain, 0)

    route = route_ref[...]
    y = jnp.zeros(x_ref.shape, F32)
    for kk in range(TOP_K):
        y = y + buf[kk] * route[:, TOP_K + kk:TOP_K + kk + 1]
    x2 = x_ref[...] + g2_ref[0] * y
    ms = jnp.mean(x2 * x2, axis=-1, keepdims=True)
    o_ref[...] = x2 * lax.rsqrt(ms + NORM_EPS) * gf_ref[...]


def _combine(x1, route, slots, gate2, norm_final_g, ys):
    n_tok, d = x1.shape
    seq = n_tok // gate2.shape[0]
    tm = _tile(seq, 256)
    per_b = seq // tm
    grid_spec = pltpu.PrefetchScalarGridSpec(
        num_scalar_prefetch=1,
        grid=(n_tok // tm,),
        in_specs=[pl.BlockSpec((tm, d), lambda i, s: (i, 0)),
                  pl.BlockSpec((tm, LANES), lambda i, s: (i, 0)),
                  pl.BlockSpec((1, 1, d), lambda i, s: (i // per_b, 0, 0)),
                  pl.BlockSpec((1, d), lambda i, s: (0, 0)),
                  pl.BlockSpec(memory_space=pl.ANY)],
        out_specs=pl.BlockSpec((tm, d), lambda i, s: (i, 0)),
        scratch_shapes=[pltpu.VMEM((TOP_K, tm, d), F32), pltpu.SemaphoreType.DMA(())],
    )
    return pl.pallas_call(
        _combine_kernel,
        grid_spec=grid_spec,
        out_shape=jax.ShapeDtypeStruct((n_tok, d), F32),
        compiler_params=_params("arbitrary"),
        name="moe_combine",
    )(slots, x1, route, gate2, norm_final_g.reshape(1, d), ys)


def _pad_cols(w, n):
    return jnp.pad(w, ((0, 0), (0, n - w.shape[1])))


def kernel(x, c, w_ada, b_ada, norm1_g, w_in, rwkv_mu, w_decay_up, decay_w0, w_iclr_up, iclr_a0, w_gate_up_rwkv, rwkv_k_k, rwkv_k_a, rwkv_r_k, rwkv_gn_g, rwkv_gn_b, w_out_a, fox_b_f, fox_q_norm, fox_k_norm, w_out_b, w_out, norm2_g, w_router, b_router, w_expert_gu, b_expert_gu, w_expert_down, b_expert_down, norm_final_g):
    bsz, seq, d = x.shape
    rw_heads, rw_hd = rwkv_r_k.shape
    rw = rw_heads * rw_hd
    dr, ir, gr = w_decay_up.shape[0], w_iclr_up.shape[0], w_gate_up_rwkv.shape[0]
    fh = fox_b_f.shape[0]
    fw = w_out_b.shape[0]
    fhd = fw // fh
    assert fhd == LANES and dr <= LANES and ir <= LANES and gr % LANES == 0
    n_experts = w_router.shape[1]

    mod = _ada(c, w_ada, b_ada)
    shift1, scale1, gate1, shift2, scale2, gate2 = (m[:, None, :] for m in jnp.split(mod, 6, axis=-1))

    o_r = 0
    o_f = 3 * rw + dr + ir + gr
    o_g = o_f + 3 * fw + fh + fw
    col = lambda a, n: w_in[:, a:a + n]
    w_rwkv = jnp.concatenate([col(0, 3 * rw), _pad_cols(col(3 * rw, dr), LANES),
                              _pad_cols(col(3 * rw + dr, ir), LANES), col(3 * rw + dr + ir, gr)],
                             axis=1).astype(BF16)
    mu = jnp.concatenate([rwkv_mu[:3 * rw], jnp.pad(rwkv_mu[3 * rw:3 * rw + dr], (0, LANES - dr)),
                          jnp.pad(rwkv_mu[3 * rw + dr:3 * rw + dr + ir], (0, LANES - ir)),
                          rwkv_mu[3 * rw + dr + ir:]])
    w_qk = col(o_f, 2 * fw).astype(BF16)
    w_v = col(o_f + 2 * fw, fw).astype(BF16)
    w_ogf = jnp.concatenate([col(o_f + 3 * fw + fh, fw), _pad_cols(col(o_f + 3 * fw, fh), LANES)],
                            axis=1).astype(BF16)
    w_gates = col(o_g, 2 * d).astype(BF16)
    qk_gain = jnp.concatenate([jnp.tile(fox_q_norm * (fhd ** -0.5 * LOG2E), fh), jnp.tile(fox_k_norm, fh)])

    def row(extra, n):
        return (jnp.zeros((n,), F32) if extra is None else extra).reshape(1, n)

    gates, h1 = _inproj(x, norm1_g, scale1, shift1, w_gates, row(None, w_gates.shape[1]), BF16,
                        _ep_sigmoid, "inproj_merge_gates")

    def proj(w, extra, dtype, ep, name):
        return _proj(h1, w, row(extra, w.shape[1]), dtype, ep, name)

    p_rwkv = proj(w_rwkv, None, F32, _ep_identity, "inproj_rwkv")
    qk = proj(w_qk, qk_gain, BF16, _ep_headnorm, "inproj_fox_qk")
    v_fox = proj(w_v, None, BF16, _ep_identity, "inproj_fox_v")
    ogf = proj(w_ogf, None, F32, _ep_identity, "inproj_fox_gate_forget")

    wd = jnp.pad(w_decay_up, ((0, LANES - dr), (0, 0)))
    wi = jnp.pad(w_iclr_up, ((0, LANES - ir), (0, 0)))
    r, k2, v, lw, kk, aic, g = _rwkv_prep(p_rwkv, mu, wd, decay_w0, wi, iclr_a0,
                                          w_gate_up_rwkv.astype(BF16), rwkv_k_k, rwkv_k_a, rw, gr)
    y_a = _rwkv_scan(r, k2, v, lw, kk, aic, g, rwkv_r_k, rwkv_gn_g, rwkv_gn_b, rw_hd)

    f_t = jnp.transpose(ogf[:, :, fw:fw + fh], (0, 2, 1))
    cum = _logf_cumsum(f_t, fox_b_f)
    y_b = _fox_attention(qk, v_fox, cum, ogf, fh)

    merged = _merge(y_a, y_b, gates, w_out_a.astype(BF16), w_out_b.astype(BF16))
    x1, route, counts = _route(merged, x, gate1, w_out.astype(BF16), norm2_g, scale2, shift2,
                               w_router, b_router)

    n_tok = bsz * seq
    n_assign = n_tok * TOP_K
    bm = EXPERT_ROWS
    n_blocks = -(-n_assign // bm) + n_experts
    cnt = counts[0, :n_experts].astype(jnp.int32)
    padded = (cnt + bm - 1) // bm * bm
    pad_end = jnp.cumsum(padded)
    pad_start = pad_end - padded
    top_i = route[:, :, 0:TOP_K].astype(jnp.int32)
    rank = route[:, :, 2 * TOP_K:3 * TOP_K].astype(jnp.int32)
    e_ids = jnp.arange(n_experts, dtype=jnp.int32)
    start_of = jnp.sum(jnp.where(top_i[..., None] == e_ids, pad_start, 0), axis=-1)
    slots = (start_of + rank).reshape(n_assign)
    blk_first = jnp.arange(n_blocks, dtype=jnp.int32) * bm
    blk_e = jnp.minimum(jnp.sum((pad_end[None, :] <= blk_first[:, None]).astype(jnp.int32), axis=1),
                        n_experts - 1)
    n_active = (pad_end[-1:] // bm).astype(jnp.int32)

    last_blk = jnp.where(padded > 0, pad_end // bm - 1, -1)
    trail = n_active[0] + jnp.arange(n_experts, dtype=jnp.int32)
    zero_blocks = jnp.concatenate([last_blk, jnp.where(trail < n_blocks, trail, -1)]).astype(jnp.int32)
    x1_tok = x1.reshape(n_tok, d)
    xs = _dispatch(x1_tok, slots, zero_blocks, norm2_g, scale2, shift2, n_blocks * bm)
    ys = _experts(xs, blk_e, n_active, w_expert_gu.astype(BF16), b_expert_gu,
                  w_expert_down.astype(BF16), b_expert_down)
    out = _combine(x1_tok, route.reshape(n_tok, LANES), slots, gate2, norm_final_g, ys)
    return out.reshape(bsz, seq, d)
```

```python
import functools

import jax
import jax.numpy as jnp
from jax import lax
from jax.experimental import pallas as pl
from jax.experimental.pallas import tpu as pltpu

F32 = jnp.float32
BF16 = jnp.bfloat16
HIGHEST = lax.Precision.HIGHEST

TOP_K = 4
NORM_EPS = 1e-6
GN_EPS = 64e-5
SWIGLU_LIMIT = 7.0
SWIGLU_ALPHA = 1.702
LANES = 128
RWKV_CHUNK = 64
RWKV_PAIRS = 4
VMEM_LIMIT_BYTES = 56 * 1024 * 1024
EXPERT_ROWS = 512
EXPERT_FF_TILE = 1024
NEG_BIG = -1e30
LOG2E = 1.4426950408889634
FOX_ROWS = 256
FOX_TQ = 2048
FOX_TK = 1024


def _tile(n, pref, mult=8):
    t = min(pref, n)
    t -= t % mult
    while t >= mult:
        if n % t == 0:
            return t
        t -= mult
    return n


def _params(*sem):
    return pltpu.CompilerParams(dimension_semantics=sem, vmem_limit_bytes=VMEM_LIMIT_BYTES)


def _sigmoid(x):
    return 1.0 / (1.0 + jnp.exp(-x))


def _rms_mod(x, g, scale, shift):
    ms = jnp.mean(x * x, axis=-1, keepdims=True)
    return x * lax.rsqrt(ms + NORM_EPS) * g * (1.0 + scale) + shift


def _ada_kernel(c_ref, w_ref, b_ref, o_ref):
    c = c_ref[...]
    s = c * _sigmoid(c)
    o_ref[...] = jnp.dot(s, w_ref[...], preferred_element_type=F32, precision=HIGHEST) + b_ref[...]


def _ada(c, w_ada, b_ada):
    bsz, d = c.shape
    n = w_ada.shape[1]
    rows = 8
    c_pad = jnp.zeros((rows, d), F32).at[:bsz].set(c)
    tn = _tile(n, 1024, LANES)
    out = pl.pallas_call(
        _ada_kernel,
        grid=(n // tn,),
        in_specs=[pl.BlockSpec((rows, d), lambda j: (0, 0)),
                  pl.BlockSpec((d, tn), lambda j: (0, j)),
                  pl.BlockSpec((1, tn), lambda j: (0, j))],
        out_specs=pl.BlockSpec((rows, tn), lambda j: (0, j)),
        out_shape=jax.ShapeDtypeStruct((rows, n), F32),
        compiler_params=_params("parallel"),
        name="adaln",
    )(c_pad, w_ada, b_ada.reshape(1, n))
    return out[:bsz]


def _inproj_kernel(x_ref, g_ref, sc_ref, sh_ref, w_ref, e_ref, o_ref, h_ref, *, epilogue):
    @pl.when(pl.program_id(2) == 0)
    def _():
        h = _rms_mod(x_ref[0], g_ref[...], sc_ref[0], sh_ref[0])
        h_ref[0] = h.astype(h_ref.dtype)

    acc = jnp.dot(h_ref[0], w_ref[...], preferred_element_type=F32)
    o_ref[0] = epilogue(acc, e_ref[...]).astype(o_ref.dtype)


def _proj_kernel(h_ref, w_ref, e_ref, o_ref, *, epilogue):
    acc = jnp.dot(h_ref[0], w_ref[...], preferred_element_type=F32)
    o_ref[0] = epilogue(acc, e_ref[...]).astype(o_ref.dtype)


def _ep_identity(acc, extra):
    return acc


def _ep_sigmoid(acc, extra):
    return _sigmoid(acc)


def _ep_headnorm(acc, extra):
    outs = []
    for h in range(acc.shape[1] // LANES):
        a = acc[:, h * LANES:(h + 1) * LANES]
        ms = jnp.mean(a * a, axis=-1, keepdims=True)
        outs.append(a * lax.rsqrt(ms + NORM_EPS))
    return jnp.concatenate(outs, axis=1) * extra


def _inproj(x, g, scale, shift, w, extra, out_dtype, epilogue, name):
    bsz, seq, d = x.shape
    n = w.shape[1]
    tm = _tile(seq, 1024)
    tn = _tile(n, 512, LANES)
    return pl.pallas_call(
        functools.partial(_inproj_kernel, epilogue=epilogue),
        grid=(bsz, seq // tm, n // tn),
        in_specs=[pl.BlockSpec((1, tm, d), lambda b, i, j: (b, i, 0)),
                  pl.BlockSpec((1, d), lambda b, i, j: (0, 0)),
                  pl.BlockSpec((1, 1, d), lambda b, i, j: (b, 0, 0)),
                  pl.BlockSpec((1, 1, d), lambda b, i, j: (b, 0, 0)),
                  pl.BlockSpec((d, tn), lambda b, i, j: (0, j)),
                  pl.BlockSpec((1, tn), lambda b, i, j: (0, j))],
        out_specs=[pl.BlockSpec((1, tm, tn), lambda b, i, j: (b, i, j)),
                   pl.BlockSpec((1, tm, d), lambda b, i, j: (b, i, 0))],
        out_shape=[jax.ShapeDtypeStruct((bsz, seq, n), out_dtype),
                   jax.ShapeDtypeStruct((bsz, seq, d), BF16)],
        compiler_params=_params("parallel", "parallel", "arbitrary"),
        name=name,
    )(x, g.reshape(1, d), scale, shift, w, extra)


def _proj(h, w, extra, out_dtype, epilogue, name):
    bsz, seq, d = h.shape
    n = w.shape[1]
    tm = _tile(seq, 1024)
    tn = _tile(n, 512, LANES)
    return pl.pallas_call(
        functools.partial(_proj_kernel, epilogue=epilogue),
        grid=(bsz, seq // tm, n // tn),
        in_specs=[pl.BlockSpec((1, tm, d), lambda b, i, j: (b, i, 0)),
                  pl.BlockSpec((d, tn), lambda b, i, j: (0, j)),
                  pl.BlockSpec((1, tn), lambda b, i, j: (0, j))],
        out_specs=pl.BlockSpec((1, tm, tn), lambda b, i, j: (b, i, j)),
        out_shape=jax.ShapeDtypeStruct((bsz, seq, n), out_dtype),
        compiler_params=_params("parallel", "parallel", "parallel"),
        name=name,
    )(h, w, extra)


def _rwkv_prep_kernel(p_ref, mu_ref, wd_ref, w0_ref, wi_ref, a0_ref, wg_ref, kk_ref, ka_ref,
                      r_o, k_o, v_o, lw_o, kk_o, a_o, g_o, carry, *, width, gate_rank):
    @pl.when(pl.program_id(1) == 0)
    def _():
        carry[...] = jnp.zeros_like(carry)

    p = p_ref[0]
    tt = p.shape[0]
    row = lax.broadcasted_iota(jnp.int32, p.shape, 0)
    prev = jnp.where(row == 0, carry[...], pltpu.roll(p, 1, axis=0))
    carry[...] = p[tt - 1:tt, :]
    pm = p + mu_ref[...] * (prev - p)
    w = width
    r = pm[:, 0:w]
    k = pm[:, w:2 * w]
    v = pm[:, 2 * w:3 * w]
    d_lo = pm[:, 3 * w:3 * w + LANES]
    a_lo = pm[:, 3 * w + LANES:3 * w + 2 * LANES]
    g_lo = pm[:, 3 * w + 2 * LANES:3 * w + 2 * LANES + gate_rank]
    w_pre = w0_ref[...] + jnp.dot(jnp.tanh(d_lo), wd_ref[...], preferred_element_type=F32,
                                  precision=HIGHEST)
    lw_o[0] = -jnp.exp(-0.5) * _sigmoid(w_pre)
    a = _sigmoid(a0_ref[...] + jnp.dot(a_lo, wi_ref[...], preferred_element_type=F32,
                                       precision=HIGHEST))
    g_o[0] = jnp.dot(_sigmoid(g_lo).astype(BF16), wg_ref[...], preferred_element_type=F32)
    r_o[0] = r
    v_o[0] = v
    a_o[0] = a
    kk_o[0] = k * kk_ref[...]
    k_o[0] = k * (1.0 + (a - 1.0) * ka_ref[...])


def _rwkv_prep(p, mu, wd, w0, wi, a0, wg, k_k, k_a, width, gate_rank):
    bsz, seq, n = p.shape
    tt = _tile(seq, 256)
    row = lambda arr: arr.reshape(1, -1)
    full = lambda shape: pl.BlockSpec(shape, lambda b, i: (0,) * len(shape))
    out_sds = jax.ShapeDtypeStruct((bsz, seq, width), F32)
    out_spec = pl.BlockSpec((1, tt, width), lambda b, i: (b, i, 0))
    return pl.pallas_call(
        functools.partial(_rwkv_prep_kernel, width=width, gate_rank=gate_rank),
        grid=(bsz, seq // tt),
        in_specs=[pl.BlockSpec((1, tt, n), lambda b, i: (b, i, 0)),
                  full((1, n)), full(wd.shape), full((1, width)), full(wi.shape), full((1, width)),
                  full(wg.shape), full((1, width)), full((1, width))],
        out_specs=[out_spec] * 7,
        out_shape=[out_sds] * 7,
        scratch_shapes=[pltpu.VMEM((1, n), F32)],
        compiler_params=_params("parallel", "arbitrary"),
        name="rwkv_prep",
    )(p, row(mu), wd, row(w0), wi, row(a0), wg, row(k_k), row(k_a))


def _rwkv_scan_kernel(r_ref, k_ref, v_ref, lw_ref, kk_ref, a_ref, g_ref, rk_ref, gg_ref, gb_ref,
                      o_ref, s_scr, *, n_chunks, head_dim):
    L = RWKV_CHUNK
    L2 = 2 * L

    @pl.when(pl.program_id(2) == 0)
    def _():
        s_scr[...] = jnp.zeros_like(s_scr)

    lane = lax.broadcasted_iota(jnp.int32, (1, LANES), 1)
    m0 = (lane < head_dim).astype(F32)
    m1 = 1.0 - m0
    r2 = lax.broadcasted_iota(jnp.int32, (L2, L2), 0)
    c2 = lax.broadcasted_iota(jnp.int32, (L2, L2), 1)
    same = (r2 < L) == (c2 < L)
    strict = jnp.logical_and(same, c2 < r2)
    incl = jnp.logical_and(same, c2 <= r2)
    eye = (r2 == c2).astype(F32)
    rowi = lax.broadcasted_iota(jnp.int32, (L, LANES), 0)
    n_pairs = r_ref.shape[2] // LANES

    def stack_f32(x):
        return jnp.concatenate([x * m0, x * m1], axis=0)

    def stack(x):
        return stack_f32(x).astype(BF16)

    def head_sum(x):
        s0 = jnp.sum(x * m0, axis=-1, keepdims=True)
        s1 = jnp.sum(x * m1, axis=-1, keepdims=True)
        return s0 * m0 + s1 * m1

    def nt(a, b):
        return lax.dot_general(a, b, (((1,), (1,)), ((), ())), preferred_element_type=F32)

    def mm(a, b):
        return jnp.dot(a.astype(BF16), b.astype(BF16), preferred_element_type=F32)

    blk_masks = []
    size = 8
    while size <= L:
        blk_masks.append(jnp.bitwise_xor(r2, c2) < size)
        size *= 2
    off_masks = [jnp.logical_and(hi, jnp.logical_not(lo))
                 for lo, hi in zip(blk_masks[:-1], blk_masks[1:])]
    chunks = range(n_chunks)

    ctx = []
    for c, p in [(c, p) for c in chunks for p in range(n_pairs)]:
        sl = pl.ds(c * L, L)
        ln = pl.ds(p * LANES, LANES)
        r = r_ref[0, sl, ln]
        k = k_ref[0, sl, ln]
        v = v_ref[0, sl, ln]
        lw = lw_ref[0, sl, ln]
        kk = kk_ref[0, sl, ln]
        aic = a_ref[0, sl, ln]
        rk = rk_ref[:, ln]
        cum = lw
        sh = 1
        while sh < L:
            cum = cum + jnp.where(rowi >= sh, pltpu.roll(cum, sh, axis=0), 0.0)
            sh *= 2
        tot = cum[L - 1:L, :]
        w_rem = jnp.exp(tot - cum)
        w_inv = jnp.exp(-cum)
        kk = kk / jnp.maximum(jnp.sqrt(head_sum(kk * kk)), 1e-12)
        b_vec = kk * aic
        ar = jnp.concatenate([stack(-kk * jnp.exp(cum - lw)), stack(r * jnp.exp(cum))], axis=0)
        bk = jnp.concatenate([stack(b_vec * w_inv), stack(k * w_inv)], axis=0)
        v_st = stack(v)
        b_rem_t = stack_f32(b_vec * w_rem).T.astype(BF16)
        k_rem_t = stack_f32(k * w_rem).T.astype(BF16)
        decay = jnp.broadcast_to(jnp.exp(tot), (LANES, LANES)).T
        ctx.append(dict(sl=sl, ln=ln, p=p, ar=ar, bk=bk, v_st=v_st, decay=decay, b_rem_t=b_rem_t,
                        k_rem_t=k_rem_t, bonus=head_sum(r * k * rk) * v))

    for x in ctx:
        big = nt(x["ar"], x["bk"])
        x["n_ab"] = jnp.where(strict, big[0:L2, 0:L2], 0.0)
        m_ak = jnp.where(strict, big[0:L2, L2:2 * L2], 0.0)
        m_rk = jnp.where(incl, big[L2:2 * L2, L2:2 * L2], 0.0)
        x["m_rb"] = jnp.where(incl, big[L2:2 * L2, 0:L2], 0.0).astype(BF16)
        x["m_akrk"] = jnp.concatenate([m_ak, m_rk], axis=0).astype(BF16)
    for x in ctx:
        x["m_v"] = mm(x["m_akrk"], x["v_st"])
        x["kv"] = mm(x["k_rem_t"], x["v_st"])

    for x in ctx:
        x["d8"] = jnp.where(blk_masks[0], x["n_ab"], 0.0)
        x["t"] = eye + x["d8"]
    for x in ctx:
        x["pw"] = mm(x["d8"], x["d8"])
    for x in ctx:
        x["t"] = x["t"] + mm(x["t"], x["pw"])
    for x in ctx:
        x["pw"] = mm(x["pw"], x["pw"])
    for x in ctx:
        x["t"] = x["t"] + mm(x["t"], x["pw"])
    for off in off_masks:
        for x in ctx:
            x["nt"] = mm(jnp.where(off, x["n_ab"], 0.0), x["t"])
        for x in ctx:
            x["t"] = x["t"] + mm(x["t"], x["nt"])

    for x in ctx:
        rhs = jnp.concatenate([x["ar"][0:L2], x["m_v"][0:L2].astype(BF16)], axis=1)
        x["ta_uv"] = mm(x["t"], rhs).astype(BF16)
    for x in ctx:
        w_q = mm(x["b_rem_t"], x["ta_uv"])
        g_y = mm(x["m_rb"], x["ta_uv"])
        x["q"] = w_q[:, LANES:] + x["kv"]
        x["y_loc"] = g_y[:, LANES:] + x["m_v"][L2:2 * L2]
        g = g_y[:, :LANES] + x["ar"][L2:2 * L2].astype(F32)
        x["wg"] = jnp.concatenate([w_q[:, :LANES], g], axis=0).astype(BF16)

    s = [s_scr[p] for p in range(n_pairs)]
    for x in ctx:
        p = x["p"]
        ws = mm(x["wg"], s[p])
        x["y_st"] = ws[L2:2 * L2] + x["y_loc"]
        s[p] = x["decay"] * s[p] + ws[0:L2] + x["q"]
    for p in range(n_pairs):
        s_scr[p] = s[p]

    inv_n = 1.0 / head_dim
    for x in ctx:
        y_st = x["y_st"]
        y = y_st[0:L] + y_st[L:L2]
        mean = head_sum(y) * inv_n
        yc = y - mean
        var = head_sum(yc * yc) * inv_n
        y_gn = yc * lax.rsqrt(var + GN_EPS) * gg_ref[:, x["ln"]] + gb_ref[:, x["ln"]]
        o_ref[0, x["sl"], x["ln"]] = ((y_gn + x["bonus"]) * g_ref[0, x["sl"], x["ln"]]).astype(o_ref.dtype)


def _rwkv_scan(r, k, v, lw, kk, a, g, r_k, gn_g, gn_b, head_dim):
    bsz, seq, width = r.shape
    assert 2 * head_dim == LANES and width % LANES == 0
    tc = _tile(seq, 8 * RWKV_CHUNK, RWKV_CHUNK)
    pw = _tile(width, RWKV_PAIRS * LANES, LANES)
    tok = pl.BlockSpec((1, tc, pw), lambda b, h, i: (b, i, h))
    par = pl.BlockSpec((1, pw), lambda b, h, i: (0, h))
    row = lambda arr: arr.reshape(1, width)
    return pl.pallas_call(
        functools.partial(_rwkv_scan_kernel, n_chunks=tc // RWKV_CHUNK, head_dim=head_dim),
        grid=(bsz, width // pw, seq // tc),
        in_specs=[tok] * 7 + [par] * 3,
        out_specs=tok,
        out_shape=jax.ShapeDtypeStruct((bsz, seq, width), BF16),
        scratch_shapes=[pltpu.VMEM((pw // LANES, LANES, LANES), F32)],
        compiler_params=_params("parallel", "parallel", "arbitrary"),
        name="rwkv_scan",
    )(r, k, v, lw, kk, a, g, row(r_k), row(gn_g), row(gn_b))


def _logf_cumsum_kernel(f_ref, b_ref, o_ref, carry):
    @pl.when(pl.program_id(1) == 0)
    def _():
        carry[...] = jnp.zeros_like(carry)

    z = f_ref[0] + b_ref[...]
    lf = jnp.minimum(z, 0.0) - jnp.log(1.0 + jnp.exp(-jnp.abs(z)))
    ts = z.shape[1]
    ri = lax.broadcasted_iota(jnp.int32, (LANES, LANES), 0)
    ci = lax.broadcasted_iota(jnp.int32, (LANES, LANES), 1)
    upper = (ri <= ci).astype(F32)
    run = carry[...]
    for j in range(ts // LANES):
        blk = jnp.dot(lf[:, j * LANES:(j + 1) * LANES], upper, preferred_element_type=F32,
                      precision=HIGHEST) + run
        o_ref[0, :, j * LANES:(j + 1) * LANES] = blk
        run = blk[:, LANES - 1:LANES]
    carry[...] = run


def _logf_cumsum(f_t, b_f):
    bsz, nh, seq = f_t.shape
    ts = _tile(seq, 2048, LANES)
    return pl.pallas_call(
        _logf_cumsum_kernel,
        grid=(bsz, seq // ts),
        in_specs=[pl.BlockSpec((1, nh, ts), lambda b, i: (b, 0, i)),
                  pl.BlockSpec((nh, 1), lambda b, i: (0, 0))],
        out_specs=pl.BlockSpec((1, nh, ts), lambda b, i: (b, 0, i)),
        out_shape=jax.ShapeDtypeStruct((bsz, nh, seq), F32),
        scratch_shapes=[pltpu.VMEM((nh, 1), F32)],
        compiler_params=_params("parallel", "arbitrary"),
        name="fox_logf_cumsum",
    )(f_t, b_f.reshape(nh, 1))


def _fox_kernel(qi_ref, ki_ref, q_ref, k_ref, v_ref, ck_ref, og_ref, o_ref,
                m_scr, l_scr, acc_scr, *, rows):
    p = pl.program_id(2)
    qi = qi_ref[p]
    ki = ki_ref[p]
    tq = q_ref.shape[1]
    tk = k_ref.shape[1]

    @pl.when(ki == 0)
    def _():
        m_scr[...] = jnp.full_like(m_scr, NEG_BIG)
        l_scr[...] = jnp.zeros_like(l_scr)
        acc_scr[...] = jnp.zeros_like(acc_scr)

    n_rc = tq // rows

    n_kq = tq // tk

    def step(diag):
        k = k_ref[0]
        v = v_ref[0]
        ck = ck_ref[0, 0] * LOG2E
        col0 = 0 if diag is None else diag * tk
        live = [rc for rc in range(n_rc) if diag is None or col0 < (rc + 1) * rows]

        def n_cols(rc):
            return tk if diag is None else min(tk, (rc + 1) * rows - col0)

        def logits(rc):
            kc = n_cols(rc)
            s = lax.dot_general(q_ref[0, pl.ds(rc * rows, rows), :], k[:kc], (((1,), (1,)), ((), ())),
                                preferred_element_type=F32) - ck[:, :kc]
            if diag is not None and col0 + kc - 1 > rc * rows:
                ri = lax.broadcasted_iota(jnp.int32, s.shape, 0) + rc * rows
                ci = lax.broadcasted_iota(jnp.int32, s.shape, 1) + col0
                s = jnp.where(ci <= ri, s, NEG_BIG)
            return s

        old = {rc: (m_scr[pl.ds(rc * rows, rows), :], l_scr[pl.ds(rc * rows, rows), :],
                    acc_scr[pl.ds(rc * rows, rows), :]) for rc in live}
        new = {}
        s_next = logits(live[0])
        for i, rc in enumerate(live):
            s = s_next
            if i + 1 < len(live):
                s_next = logits(live[i + 1])
            m_prev, l_prev, acc_prev = old[rc]
            tiles = [s[:, j * LANES:(j + 1) * LANES] for j in range(n_cols(rc) // LANES)]
            m_new = jnp.maximum(m_prev, jnp.max(functools.reduce(jnp.maximum, tiles),
                                                axis=-1, keepdims=True))
            alpha = jnp.exp2(m_prev - m_new)
            p_tiles = [jnp.exp2(t - m_new) for t in tiles]
            l_new = alpha * l_prev + functools.reduce(jnp.add, p_tiles)
            pr = jnp.concatenate([t.astype(BF16) for t in p_tiles], axis=1)
            acc_new = alpha * acc_prev + jnp.dot(pr, v[:n_cols(rc)], preferred_element_type=F32)
            new[rc] = (m_new, l_new, acc_new)
        for rc in live:
            rs = pl.ds(rc * rows, rows)
            m_scr[rs, :], l_scr[rs, :], acc_scr[rs, :] = new[rc]

    @pl.when(ki < qi * n_kq)
    def _():
        step(None)

    for diag in range(n_kq):
        @pl.when(ki == qi * n_kq + diag)
        def _(diag=diag):
            step(diag)

    @pl.when(ki == qi * n_kq + n_kq - 1)
    def _():
        o = acc_scr[...] / jnp.sum(l_scr[...], axis=-1, keepdims=True)
        o_ref[0] = (o * _sigmoid(og_ref[0])).astype(o_ref.dtype)


def _fox_attention(qk, v, cum, ogf, n_heads):
    bsz, seq, _ = v.shape
    tk = _tile(seq, FOX_TK, LANES)
    tq = _tile(seq, FOX_TQ, tk)
    qi_list, ki_list = [], []
    for qi in range(seq // tq):
        for ki in range((qi + 1) * (tq // tk)):
            qi_list.append(qi)
            ki_list.append(ki)
    qi_arr = jnp.asarray(qi_list, jnp.int32)
    ki_arr = jnp.asarray(ki_list, jnp.int32)
    cum_row = cum[:, :, None, :]
    nh = n_heads
    grid_spec = pltpu.PrefetchScalarGridSpec(
        num_scalar_prefetch=2,
        grid=(bsz, nh, len(qi_list)),
        in_specs=[pl.BlockSpec((1, tq, LANES), lambda b, h, p, qi, ki: (b, qi[p], h)),
                  pl.BlockSpec((1, tk, LANES), lambda b, h, p, qi, ki: (b, ki[p], nh + h)),
                  pl.BlockSpec((1, tk, LANES), lambda b, h, p, qi, ki: (b, ki[p], h)),
                  pl.BlockSpec((1, 1, 1, tk), lambda b, h, p, qi, ki: (b, h, 0, ki[p])),
                  pl.BlockSpec((1, tq, LANES), lambda b, h, p, qi, ki: (b, qi[p], h))],
        out_specs=pl.BlockSpec((1, tq, LANES), lambda b, h, p, qi, ki: (b, qi[p], h)),
        scratch_shapes=[pltpu.VMEM((tq, LANES), F32), pltpu.VMEM((tq, LANES), F32),
                        pltpu.VMEM((tq, LANES), F32)],
    )
    return pl.pallas_call(
        functools.partial(_fox_kernel, rows=_tile(tq, FOX_ROWS, LANES)),
        grid_spec=grid_spec,
        out_shape=jax.ShapeDtypeStruct((bsz, seq, nh * LANES), BF16),
        compiler_params=_params("parallel", "parallel", "arbitrary"),
        name="fox_attention",
    )(qi_arr, ki_arr, qk, qk, v, cum_row, ogf)


def _merge_kernel(ya_ref, yb_ref, ga_ref, gb_ref, wa_ref, wb_ref, o_ref):
    pa = jnp.dot(ya_ref[0], wa_ref[...], preferred_element_type=F32)
    pb = jnp.dot(yb_ref[0], wb_ref[...], preferred_element_type=F32)
    o_ref[0] = (ga_ref[0].astype(F32) * pa + gb_ref[0].astype(F32) * pb).astype(o_ref.dtype)


def _merge(y_a, y_b, gates, w_a, w_b):
    bsz, seq, wa = y_a.shape
    wb = y_b.shape[2]
    d = w_a.shape[1]
    tm = _tile(seq, 512)
    return pl.pallas_call(
        _merge_kernel,
        grid=(bsz, seq // tm),
        in_specs=[pl.BlockSpec((1, tm, wa), lambda b, i: (b, i, 0)),
                  pl.BlockSpec((1, tm, wb), lambda b, i: (b, i, 0)),
                  pl.BlockSpec((1, tm, d), lambda b, i: (b, i, 0)),
                  pl.BlockSpec((1, tm, d), lambda b, i: (b, i, 1)),
                  pl.BlockSpec((wa, d), lambda b, i: (0, 0)),
                  pl.BlockSpec((wb, d), lambda b, i: (0, 0))],
        out_specs=pl.BlockSpec((1, tm, d), lambda b, i: (b, i, 0)),
        out_shape=jax.ShapeDtypeStruct((bsz, seq, d), BF16),
        compiler_params=_params("parallel", "parallel"),
        name="branch_merge",
    )(y_a, y_b, gates, gates, w_a, w_b)


def _route_kernel(m_ref, x_ref, g1_ref, w_ref, n2_ref, sc_ref, sh_ref, wr_ref, br_ref,
                  x1_ref, route_ref, cnt_ref, base, *, n_experts):
    @pl.when(jnp.logical_and(pl.program_id(0) == 0, pl.program_id(1) == 0))
    def _():
        base[...] = jnp.zeros_like(base)

    x1 = x_ref[0] + g1_ref[0] * jnp.dot(m_ref[0], w_ref[...], preferred_element_type=F32)
    x1_ref[0] = x1
    h = _rms_mod(x1, n2_ref[...], sc_ref[0], sh_ref[0])
    h_hi = h.astype(BF16)
    h_lo = (h - h_hi.astype(F32)).astype(BF16)
    logits = jnp.dot(jnp.concatenate([h_hi, h_hi, h_lo], axis=1), wr_ref[...],
                     preferred_element_type=F32) + br_ref[...]
    tm = logits.shape[0]
    lane_i = lax.broadcasted_iota(jnp.int32, (tm, LANES), 1)
    lane = lane_i.astype(F32)
    vals = jnp.where(lane_i < n_experts, logits, -jnp.inf)
    top_v, top_i, hot = [], [], []
    for _ in range(TOP_K):
        mx = jnp.max(vals, axis=-1, keepdims=True)
        ix = jnp.min(jnp.where(vals == mx, lane, float(LANES)), axis=-1, keepdims=True)
        sel = lane == ix
        vals = jnp.where(sel, -jnp.inf, vals)
        top_v.append(mx)
        top_i.append(ix)
        hot.append(sel.astype(F32))
    ex = [jnp.exp(tv - top_v[0]) for tv in top_v]
    den = ex[0] + ex[1] + ex[2] + ex[3]
    cnt = hot[0] + hot[1] + hot[2] + hot[3]
    ri = lax.broadcasted_iota(jnp.int32, (tm, tm), 0)
    ci = lax.broadcasted_iota(jnp.int32, (tm, tm), 1)
    before = jnp.dot((ci < ri).astype(BF16), cnt.astype(BF16), preferred_element_type=F32)
    before = before + base[...]
    out = jnp.zeros((tm, LANES), F32)
    for kk in range(TOP_K):
        rank = jnp.sum(hot[kk] * before, axis=-1, keepdims=True)
        out = jnp.where(lane_i == kk, top_i[kk], out)
        out = jnp.where(lane_i == TOP_K + kk, ex[kk] / den, out)
        out = jnp.where(lane_i == 2 * TOP_K + kk, rank, out)
    route_ref[0] = out
    new_base = base[...] + jnp.sum(cnt, axis=0, keepdims=True)
    base[...] = new_base
    cnt_ref[...] = jnp.broadcast_to(new_base, cnt_ref.shape)


def _route(merged, x, gate1, w_out, norm2_g, scale2, shift2, w_router, b_router):
    bsz, seq, d = x.shape
    n_experts = w_router.shape[1]
    tm = _tile(seq, 512)
    wr = jnp.zeros((d, LANES), F32).at[:, :n_experts].set(w_router)
    wr_hi = wr.astype(BF16)
    wr_lo = (wr - wr_hi.astype(F32)).astype(BF16)
    wr = jnp.concatenate([wr_hi, wr_lo, wr_hi], axis=0)
    br = jnp.zeros((1, LANES), F32).at[0, :n_experts].set(b_router)
    mod = pl.BlockSpec((1, 1, d), lambda b, i: (b, 0, 0))
    tok = pl.BlockSpec((1, tm, d), lambda b, i: (b, i, 0))
    return pl.pallas_call(
        functools.partial(_route_kernel, n_experts=n_experts),
        grid=(bsz, seq // tm),
        in_specs=[tok, tok, mod,
                  pl.BlockSpec((d, d), lambda b, i: (0, 0)),
                  pl.BlockSpec((1, d), lambda b, i: (0, 0)), mod, mod,
                  pl.BlockSpec((3 * d, LANES), lambda b, i: (0, 0)),
                  pl.BlockSpec((1, LANES), lambda b, i: (0, 0))],
        out_specs=[tok,
                   pl.BlockSpec((1, tm, LANES), lambda b, i: (b, i, 0)),
                   pl.BlockSpec((8, LANES), lambda b, i: (0, 0))],
        out_shape=[jax.ShapeDtypeStruct((bsz, seq, d), F32),
                   jax.ShapeDtypeStruct((bsz, seq, LANES), F32),
                   jax.ShapeDtypeStruct((8, LANES), F32)],
        scratch_shapes=[pltpu.VMEM((1, LANES), F32)],
        compiler_params=_params("arbitrary", "arbitrary"),
        name="residual_router",
    )(merged, x, gate1, w_out, norm2_g.reshape(1, d), scale2, shift2, wr, br)


def _dispatch_kernel(slot_ref, zblk_ref, x_ref, n2_ref, sc_ref, sh_ref, xs_ref, hbuf, zbuf, sem, zsem,
                     *, n_zero):
    i = pl.program_id(0)
    tm = hbuf.shape[0]
    bm = zbuf.shape[0]

    @pl.when(i == 0)
    def _():
        zbuf[...] = jnp.zeros_like(zbuf)

        def zero_copy(j):
            start = pl.multiple_of(zblk_ref[j] * bm, bm)
            return pltpu.make_async_copy(zbuf, xs_ref.at[pl.ds(start, bm)], zsem)

        for j in range(n_zero):
            @pl.when(zblk_ref[j] >= 0)
            def _(j=j):
                zero_copy(j).start()

        for j in range(n_zero):
            @pl.when(zblk_ref[j] >= 0)
            def _(j=j):
                zero_copy(j).wait()

    hbuf[...] = _rms_mod(x_ref[...], n2_ref[...], sc_ref[0], sh_ref[0])

    def row_copy(r, s):
        return pltpu.make_async_copy(hbuf.at[pl.ds(r, 1)], xs_ref.at[pl.ds(s, 1)], sem)

    def issue(r, carry):
        for kk in range(TOP_K):
            row_copy(r, slot_ref[(i * tm + r) * TOP_K + kk]).start(priority=kk % 2)
        return carry

    lax.fori_loop(0, tm, issue, 0)

    def drain(r, carry):
        for kk in range(TOP_K):
            row_copy(0, 0).wait()
        return carry

    lax.fori_loop(0, tm, drain, 0)


def _dispatch(x1, slots, zero_blocks, norm2_g, scale2, shift2, n_slots):
    n_tok, d = x1.shape
    seq = n_tok // scale2.shape[0]
    tm = _tile(seq, 256)
    per_b = seq // tm
    grid_spec = pltpu.PrefetchScalarGridSpec(
        num_scalar_prefetch=2,
        grid=(n_tok // tm,),
        in_specs=[pl.BlockSpec((tm, d), lambda i, s, z: (i, 0)),
                  pl.BlockSpec((1, d), lambda i, s, z: (0, 0)),
                  pl.BlockSpec((1, 1, d), lambda i, s, z: (i // per_b, 0, 0)),
                  pl.BlockSpec((1, 1, d), lambda i, s, z: (i // per_b, 0, 0))],
        out_specs=pl.BlockSpec(memory_space=pl.ANY),
        scratch_shapes=[pltpu.VMEM((tm, d), F32), pltpu.VMEM((EXPERT_ROWS, d), F32),
                        pltpu.SemaphoreType.DMA(()), pltpu.SemaphoreType.DMA(())],
    )
    return pl.pallas_call(
        functools.partial(_dispatch_kernel, n_zero=zero_blocks.shape[0]),
        grid_spec=grid_spec,
        out_shape=jax.ShapeDtypeStruct((n_slots, d), F32),
        compiler_params=_params("arbitrary"),
        name="moe_dispatch",
    )(slots, zero_blocks, x1, norm2_g.reshape(1, d), scale2, shift2)


def _expert_kernel(be_ref, na_ref, x_ref, wg_ref, wl_ref, bg_ref, bl_ref, wd_ref, bd_ref,
                   o_ref, acc, *, n_f):
    b = pl.program_id(0)
    f = pl.program_id(1)

    @pl.when(jnp.logical_and(b == 0, f == 0))
    def _():
        acc[...] = jnp.zeros_like(acc)

    @pl.when(b < na_ref[0])
    def _():
        x = x_ref[...].astype(BF16)
        gate = jnp.dot(x, wg_ref[0], preferred_element_type=F32) + bg_ref[0]
        lin = jnp.dot(x, wl_ref[0], preferred_element_type=F32) + bl_ref[0]
        gate = jnp.minimum(gate, SWIGLU_LIMIT)
        lin = jnp.clip(lin, -SWIGLU_LIMIT, SWIGLU_LIMIT)
        act = gate * _sigmoid(SWIGLU_ALPHA * gate) * (lin + 1.0)
        contrib = jnp.dot(act.astype(BF16), wd_ref[0], preferred_element_type=F32)
        if n_f == 1:
            o_ref[...] = contrib + bd_ref[0]
        else:
            total = jnp.where(f > 0, acc[...], 0.0) + contrib
            acc[...] = total
            o_ref[...] = total + bd_ref[0]

    @pl.when(jnp.logical_and(b >= na_ref[0], f == n_f - 1))
    def _():
        o_ref[...] = jnp.zeros_like(o_ref)


def _experts(xs, blk_e, n_active, w_gu, b_gu, w_dn, b_dn):
    n_slots, d = xs.shape
    n_e, _, two_ff = w_gu.shape
    d_ff = two_ff // 2
    bm = EXPERT_ROWS
    n_blocks = n_slots // bm
    tf = _tile(d_ff, EXPERT_FF_TILE, LANES)
    n_f = d_ff // tf

    def blk(b, na):
        return jnp.minimum(b, na[0] - 1)

    def ff(b, f, na):
        return jnp.where(b < na[0], f, n_f - 1)

    wmode = dict(pipeline_mode=pl.Buffered(1)) if n_f == 1 else {}
    grid_spec = pltpu.PrefetchScalarGridSpec(
        num_scalar_prefetch=2,
        grid=(n_blocks, n_f),
        in_specs=[pl.BlockSpec((bm, d), lambda b, f, be, na: (blk(b, na), 0)),
                  pl.BlockSpec((1, d, tf), lambda b, f, be, na: (be[blk(b, na)], 0, ff(b, f, na)), **wmode),
                  pl.BlockSpec((1, d, tf), lambda b, f, be, na: (be[blk(b, na)], 0, n_f + ff(b, f, na)),
                               **wmode),
                  pl.BlockSpec((1, 1, tf), lambda b, f, be, na: (be[blk(b, na)], 0, ff(b, f, na))),
                  pl.BlockSpec((1, 1, tf), lambda b, f, be, na: (be[blk(b, na)], 0, n_f + ff(b, f, na))),
                  pl.BlockSpec((1, tf, d), lambda b, f, be, na: (be[blk(b, na)], ff(b, f, na), 0), **wmode),
                  pl.BlockSpec((1, 1, d), lambda b, f, be, na: (be[blk(b, na)], 0, 0))],
        out_specs=pl.BlockSpec((bm, d), lambda b, f, be, na: (b, 0)),
        scratch_shapes=[pltpu.VMEM((bm, d) if n_f > 1 else (8, LANES), F32)],
    )
    return pl.pallas_call(
        functools.partial(_expert_kernel, n_f=n_f),
        grid_spec=grid_spec,
        out_shape=jax.ShapeDtypeStruct((n_slots, d), F32),
        compiler_params=_params("arbitrary", "arbitrary"),
        name="moe_experts",
    )(blk_e, n_active, xs, w_gu, w_gu, b_gu.reshape(n_e, 1, two_ff), b_gu.reshape(n_e, 1, two_ff),
      w_dn, b_dn.reshape(n_e, 1, d))


def _combine_kernel(slot_ref, x_ref, route_ref, g2_ref, gf_ref, ys_ref, o_ref, buf, sem):
    i = pl.program_id(0)
    tm = x_ref.shape[0]

    def row_copy(kk, r, s):
        return pltpu.make_async_copy(ys_ref.at[pl.ds(s, 1)], buf.at[kk, pl.ds(r, 1)], sem)

    def issue(r, carry):
        for kk in range(TOP_K):
            row_copy(kk, r, slot_ref[(i * tm + r) * TOP_K + kk]).start(priority=kk % 2)
        return carry

    lax.fori_loop(0, tm, issue, 0)

    def drain(r, carry):
        for kk in range(TOP_K):
            row_copy(0, 0, 0).wait()
        return carry

    lax.fori_loop(0, tm, drain, 0)

    route = route_ref[...]
    y = jnp.zeros(x_ref.shape, F32)
    for kk in range(TOP_K):
        y = y + buf[kk] * route[:, TOP_K + kk:TOP_K + kk + 1]
    x2 = x_ref[...] + g2_ref[0] * y
    ms = jnp.mean(x2 * x2, axis=-1, keepdims=True)
    o_ref[...] = x2 * lax.rsqrt(ms + NORM_EPS) * gf_ref[...]


def _combine(x1, route, slots, gate2, norm_final_g, ys):
    n_tok, d = x1.shape
    seq = n_tok // gate2.shape[0]
    tm = _tile(seq, 256)
    per_b = seq // tm
    grid_spec = pltpu.PrefetchScalarGridSpec(
        num_scalar_prefetch=1,
        grid=(n_tok // tm,),
        in_specs=[pl.BlockSpec((tm, d), lambda i, s: (i, 0)),
                  pl.BlockSpec((tm, LANES), lambda i, s: (i, 0)),
                  pl.BlockSpec((1, 1, d), lambda i, s: (i // per_b, 0, 0)),
                  pl.BlockSpec((1, d), lambda i, s: (0, 0)),
                  pl.BlockSpec(memory_space=pl.ANY)],
        out_specs=pl.BlockSpec((tm, d), lambda i, s: (i, 0)),
        scratch_shapes=[pltpu.VMEM((TOP_K, tm, d), F32), pltpu.SemaphoreType.DMA(())],
    )
    return pl.pallas_call(
        _combine_kernel,
        grid_spec=grid_spec,
        out_shape=jax.ShapeDtypeStruct((n_tok, d), F32),
        compiler_params=_params("arbitrary"),
        name="moe_combine",
    )(slots, x1, route, gate2, norm_final_g.reshape(1, d), ys)


def _pad_cols(w, n):
    return jnp.pad(w, ((0, 0), (0, n - w.shape[1])))


def kernel(x, c, w_ada, b_ada, norm1_g, w_in, rwkv_mu, w_decay_up, decay_w0, w_iclr_up, iclr_a0, w_gate_up_rwkv, rwkv_k_k, rwkv_k_a, rwkv_r_k, rwkv_gn_g, rwkv_gn_b, w_out_a, fox_b_f, fox_q_norm, fox_k_norm, w_out_b, w_out, norm2_g, w_router, b_router, w_expert_gu, b_expert_gu, w_expert_down, b_expert_down, norm_final_g):
    bsz, seq, d = x.shape
    rw_heads, rw_hd = rwkv_r_k.shape
    rw = rw_heads * rw_hd
    dr, ir, gr = w_decay_up.shape[0], w_iclr_up.shape[0], w_gate_up_rwkv.shape[0]
    fh = fox_b_f.shape[0]
    fw = w_out_b.shape[0]
    fhd = fw // fh
    assert fhd == LANES and dr <= LANES and ir <= LANES and gr % LANES == 0
    n_experts = w_router.shape[1]

    mod = _ada(c, w_ada, b_ada)
    shift1, scale1, gate1, shift2, scale2, gate2 = (m[:, None, :] for m in jnp.split(mod, 6, axis=-1))

    o_r = 0
    o_f = 3 * rw + dr + ir + gr
    o_g = o_f + 3 * fw + fh + fw
    col = lambda a, n: w_in[:, a:a + n]
    w_rwkv = jnp.concatenate([col(0, 3 * rw), _pad_cols(col(3 * rw, dr), LANES),
                              _pad_cols(col(3 * rw + dr, ir), LANES), col(3 * rw + dr + ir, gr)],
                             axis=1).astype(BF16)
    mu = jnp.concatenate([rwkv_mu[:3 * rw], jnp.pad(rwkv_mu[3 * rw:3 * rw + dr], (0, LANES - dr)),
                          jnp.pad(rwkv_mu[3 * rw + dr:3 * rw + dr + ir], (0, LANES - ir)),
                          rwkv_mu[3 * rw + dr + ir:]])
    w_qk = col(o_f, 2 * fw).astype(BF16)
    w_v = col(o_f + 2 * fw, fw).astype(BF16)
    w_ogf = jnp.concatenate([col(o_f + 3 * fw + fh, fw), _pad_cols(col(o_f + 3 * fw, fh), LANES)],
                            axis=1).astype(BF16)
    w_gates = col(o_g, 2 * d).astype(BF16)
    qk_gain = jnp.concatenate([jnp.tile(fox_q_norm * (fhd ** -0.5 * LOG2E), fh), jnp.tile(fox_k_norm, fh)])

    def row(extra, n):
        return (jnp.zeros((n,), F32) if extra is None else extra).reshape(1, n)

    gates, h1 = _inproj(x, norm1_g, scale1, shift1, w_gates, row(None, w_gates.shape[1]), BF16,
                        _ep_sigmoid, "inproj_merge_gates")

    def proj(w, extra, dtype, ep, name):
        return _proj(h1, w, row(extra, w.shape[1]), dtype, ep, name)

    p_rwkv = proj(w_rwkv, None, F32, _ep_identity, "inproj_rwkv")
    qk = proj(w_qk, qk_gain, BF16, _ep_headnorm, "inproj_fox_qk")
    v_fox = proj(w_v, None, BF16, _ep_identity, "inproj_fox_v")
    ogf = proj(w_ogf, None, F32, _ep_identity, "inproj_fox_gate_forget")

    wd = jnp.pad(w_decay_up, ((0, LANES - dr), (0, 0)))
    wi = jnp.pad(w_iclr_up, ((0, LANES - ir), (0, 0)))
    r, k2, v, lw, kk, aic, g = _rwkv_prep(p_rwkv, mu, wd, decay_w0, wi, iclr_a0,
                                          w_gate_up_rwkv.astype(BF16), rwkv_k_k, rwkv_k_a, rw, gr)
    y_a = _rwkv_scan(r, k2, v, lw, kk, aic, g, rwkv_r_k, rwkv_gn_g, rwkv_gn_b, rw_hd)

    f_t = jnp.transpose(ogf[:, :, fw:fw + fh], (0, 2, 1))
    cum = _logf_cumsum(f_t, fox_b_f)
    y_b = _fox_attention(qk, v_fox, cum, ogf, fh)

    merged = _merge(y_a, y_b, gates, w_out_a.astype(BF16), w_out_b.astype(BF16))
    x1, route, counts = _route(merged, x, gate1, w_out.astype(BF16), norm2_g, scale2, shift2,
                               w_router, b_router)

    n_tok = bsz * seq
    n_assign = n_tok * TOP_K
    bm = EXPERT_ROWS
    n_blocks = -(-n_assign // bm) + n_experts
    cnt = counts[0, :n_experts].astype(jnp.int32)
    padded = (cnt + bm - 1) // bm * bm
    pad_end = jnp.cumsum(padded)
    pad_start = pad_end - padded
    top_i = route[:, :, 0:TOP_K].astype(jnp.int32)
    rank = route[:, :, 2 * TOP_K:3 * TOP_K].astype(jnp.int32)
    e_ids = jnp.arange(n_experts, dtype=jnp.int32)
    start_of = jnp.sum(jnp.where(top_i[..., None] == e_ids, pad_start, 0), axis=-1)
    slots = (start_of + rank).reshape(n_assign)
    blk_first = jnp.arange(n_blocks, dtype=jnp.int32) * bm
    blk_e = jnp.minimum(jnp.sum((pad_end[None, :] <= blk_first[:, None]).astype(jnp.int32), axis=1),
                        n_experts - 1)
    n_active = (pad_end[-1:] // bm).astype(jnp.int32)

    last_blk = jnp.where(padded > 0, pad_end // bm - 1, -1)
    trail = n_active[0] + jnp.arange(n_experts, dtype=jnp.int32)
    zero_blocks = jnp.concatenate([last_blk, jnp.where(trail < n_blocks, trail, -1)]).astype(jnp.int32)
    x1_tok = x1.reshape(n_tok, d)
    xs = _dispatch(x1_tok, slots, zero_blocks, norm2_g, scale2, shift2, n_blocks * bm)
    ys = _experts(xs, blk_e, n_active, w_expert_gu.astype(BF16), b_expert_gu,
                  w_expert_down.astype(BF16), b_expert_down)
    out = _combine(x1_tok, route.reshape(n_tok, LANES), slots, gate2, norm_final_g, ys)
    return out.reshape(bsz, seq, d)
```
